```python
import math
import jax, jax.numpy as jnp
from jax import lax
import numpy as np

D_MODEL = 1024
BATCH = 4
SEQ = 4096
DEPTH = 2
DEC_BATCH = 32
DEC_SEQ = 8
PAST_LEN = 8192
PAGE_SIZE = 128

N_HEADS_A = 8
HEAD_DIM_A = 64
WIDTH_A = N_HEADS_A * 2 * HEAD_DIM_A
N_HEADS_B = 8
HEAD_DIM_B = 64
WIDTH_B = N_HEADS_B * HEAD_DIM_B
GATE_WIDTH = 2 * D_MODEL
PROJ_SPLITS = (WIDTH_A, 2 * WIDTH_A, 3 * WIDTH_A,
               3 * WIDTH_A + WIDTH_B, 3 * WIDTH_A + 2 * WIDTH_B, 3 * WIDTH_A + 3 * WIDTH_B)
PROJ_WIDTH = 3 * WIDTH_A + 3 * WIDTH_B + GATE_WIDTH
D_FF = 2816
CONV_WIDTH = 3
N_BUCKETS = 32
MAX_DISTANCE = 128
LAMBDA_BASE = 0.8
LAMBDA_SCALE = 0.6
LAMBDA_RATE = 0.3
Q_BLOCK = 128
NORM_EPS = 1e-6
NEG_BIG = -1e30

kernel_name = 'hybrid_diff_stickbreak_convffn_step'


def _rmsnorm(x, g):
    xf = x.astype(jnp.float32)
    y = xf * lax.rsqrt(jnp.mean(xf * xf, axis=-1, keepdims=True) + NORM_EPS)
    return (y * g.astype(jnp.float32)).astype(x.dtype)


def _rel_bucket(q_pos, k_pos):
    n = jnp.maximum(q_pos[:, None] - k_pos[None, :], 0)
    max_exact = N_BUCKETS // 2
    nf = jnp.maximum(n, 1).astype(jnp.float32)
    large = max_exact + (jnp.log(nf / max_exact) / math.log(MAX_DISTANCE / max_exact)
                         * (N_BUCKETS - max_exact)).astype(jnp.int32)
    large = jnp.minimum(large, N_BUCKETS - 1)
    return jnp.where(n < max_exact, n, large)


def _sweep_query_blocks(block_fn, q, q_pos):
    b, t = q.shape[:2]
    qb = min(Q_BLOCK, t)
    nb = -(-t // qb)
    pad = nb * qb - t
    if pad:
        q = jnp.pad(q, [(0, 0), (0, pad)] + [(0, 0)] * (q.ndim - 2))
        q_pos = jnp.pad(q_pos, (0, pad), mode='edge')
    qs = jnp.moveaxis(q.reshape((b, nb, qb) + q.shape[2:]), 1, 0)
    ps = q_pos.reshape(nb, qb)
    out = lax.map(lambda blk: block_fn(blk[0], blk[1]), (qs, ps))
    out = jnp.moveaxis(out, 0, 1)
    out = out.reshape((b, nb * qb) + out.shape[3:])
    return out[:, :t]


def _diff_attn_block(q, q_pos, k, v, k_pos, lam, rel_bias):
    s = jnp.einsum('bqhcd,bkhcd->bhcqk', q, k).astype(jnp.float32) * (HEAD_DIM_A ** -0.5)
    bias = rel_bias[_rel_bucket(q_pos, k_pos)].astype(jnp.float32)
    s = s + jnp.transpose(bias, (2, 0, 1))[None, :, None]
    causal = k_pos[None, :] <= q_pos[:, None]
    p = jax.nn.softmax(jnp.where(causal, s, NEG_BIG), axis=-1)
    w = p[:, :, 0] - lam.astype(jnp.float32) * p[:, :, 1]
    return jnp.einsum('bhqk,bkhe->bqhe', w.astype(v.dtype), v)


def _stick_breaking_block(q, q_pos, k, v, k_pos):
    z = jnp.einsum('bqhd,bkhd->bhqk', q, k).astype(jnp.float32) * (HEAD_DIM_B ** -0.5)
    valid = k_pos[None, :] < q_pos[:, None]
    log_stay = jnp.where(valid, jax.nn.log_sigmoid(-z), 0.0)
    later = lax.cumsum(log_stay, axis=3, reverse=True) - log_stay
    a = jnp.where(valid, jnp.exp(jax.nn.log_sigmoid(z) + later), 0.0)
    return jnp.einsum('bhqk,bkhd->bqhd', a.astype(v.dtype), v)


def _token_mixer(h, q_pos, past, lw, rel_bias, lam_init):
    w_in, q_g, k_g, w_lam, sub_g, w_pa, w_pb, w_o = lw
    b, t, _ = h.shape
    qa, ka, va, qb, kb, vb, gates = jnp.split(h @ w_in, PROJ_SPLITS, axis=-1)
    qa = _rmsnorm(qa.reshape(b, t, N_HEADS_A, 2, HEAD_DIM_A), q_g)
    ka = _rmsnorm(ka.reshape(b, t, N_HEADS_A, 2, HEAD_DIM_A), k_g)
    va = va.reshape(b, t, N_HEADS_A, 2 * HEAD_DIM_A)
    qb = qb.reshape(b, t, N_HEADS_B, HEAD_DIM_B)
    kb = kb.reshape(b, t, N_HEADS_B, HEAD_DIM_B)
    vb = vb.reshape(b, t, N_HEADS_B, HEAD_DIM_B)
    if past is None:
        ka_all, va_all, kb_all, vb_all, k_pos = ka, va, kb, vb, q_pos
    else:
        pka, pva, pkb, pvb = past
        ka_all = jnp.concatenate([pka, ka], axis=1)
        va_all = jnp.concatenate([pva, va], axis=1)
        kb_all = jnp.concatenate([pkb, kb], axis=1)
        vb_all = jnp.concatenate([pvb, vb], axis=1)
        k_pos = jnp.concatenate([jnp.arange(pka.shape[1], dtype=jnp.int32), q_pos])
    lam = jnp.exp(jnp.sum(w_lam[0] * w_lam[1])) - jnp.exp(jnp.sum(w_lam[2] * w_lam[3])) + lam_init
    oa = _sweep_query_blocks(
        lambda qq, pp: _diff_attn_block(qq, pp, ka_all, va_all, k_pos, lam, rel_bias), qa, q_pos)
    oa = _rmsnorm(oa, sub_g) * (1.0 - lam_init)
    ob = _sweep_query_blocks(
        lambda qq, pp: _stick_breaking_block(qq, pp, kb_all, vb_all, k_pos), qb, q_pos)
    g_a, g_b = jnp.split(gates, 2, axis=-1)
    merged = (jax.nn.sigmoid(g_a) * (oa.reshape(b, t, WIDTH_A) @ w_pa)
              + jax.nn.sigmoid(g_b) * (ob.reshape(b, t, WIDTH_B) @ w_pb))
    return merged @ w_o, (ka, va, kb, vb)


def _conv_ffn(h, conv_state, fw):
    w_up, cw, cb, w_down = fw
    b, t, _ = h.shape
    a, g = jnp.split(h @ w_up, 2, axis=-1)
    prev = jnp.zeros((b, CONV_WIDTH - 1, D_FF), a.dtype) if conv_state is None else conv_state
    ext = jnp.concatenate([prev, a], axis=1)
    a_conv = cb
    for j in range(CONV_WIDTH):
        a_conv = a_conv + ext[:, j:j + t] * cw[j]
    y = (jax.nn.gelu(a_conv, approximate=False) * g) @ w_down
    return y, ext[:, -(CONV_WIDTH - 1):]


def _trunk_layer(x, q_pos, past, conv_state, layer, attn_g, lw, ffn_g, fw, rel_bias):
    lam_init = LAMBDA_BASE - LAMBDA_SCALE * math.exp(-LAMBDA_RATE * layer)
    y, rows = _token_mixer(_rmsnorm(x, attn_g), q_pos, past, lw, rel_bias, lam_init)
    x = x + y
    y2, conv_tail = _conv_ffn(_rmsnorm(x, ffn_g), conv_state, fw)
    return x + y2, rows + (conv_tail,)


def setup_inputs(seed: int = 0) -> dict:
    key = jax.random.key(seed)
    ks = jax.random.split(key, 24)
    n_pages = PAST_LEN // PAGE_SIZE
    n_used = DEC_BATCH * n_pages
    n_pool = n_used + max(1, n_used // 4)

    def nrm(k, shape, scale=1.0):
        return scale * jax.random.normal(k, shape, jnp.float32)

    page_table = jax.random.permutation(ks[7], n_pool)[:n_used].reshape(DEC_BATCH, n_pages).astype(jnp.int32)
    return {
        'x_prompt': nrm(ks[0], (BATCH, SEQ, D_MODEL)),
        'x_sample': nrm(ks[1], (DEC_BATCH, DEC_SEQ, D_MODEL)),
        'cache_da_k': nrm(ks[2], (DEPTH, n_pool, PAGE_SIZE, N_HEADS_A, 2, HEAD_DIM_A)),
        'cache_da_v': nrm(ks[3], (DEPTH, n_pool, PAGE_SIZE, N_HEADS_A, 2 * HEAD_DIM_A)),
        'cache_sb_k': nrm(ks[4], (DEPTH, n_pool, PAGE_SIZE, N_HEADS_B, HEAD_DIM_B)),
        'cache_sb_v': nrm(ks[5], (DEPTH, n_pool, PAGE_SIZE, N_HEADS_B, HEAD_DIM_B)),
        'state_conv': nrm(ks[6], (DEPTH, DEC_BATCH, CONV_WIDTH - 1, D_FF)),
        'page_table': page_table,
        'rel_bias': nrm(ks[8], (N_BUCKETS, N_HEADS_A), 0.5),
        'attn_norm_g': 1.0 + nrm(ks[9], (DEPTH, D_MODEL), 0.02),
        'w_in': nrm(ks[10], (DEPTH, D_MODEL, PROJ_WIDTH), D_MODEL ** -0.5),
        'q_norm_g': 1.0 + nrm(ks[11], (DEPTH, HEAD_DIM_A), 0.02),
        'k_norm_g': 1.0 + nrm(ks[12], (DEPTH, HEAD_DIM_A), 0.02),
        'w_lambda': nrm(ks[13], (DEPTH, 4, HEAD_DIM_A), 0.1),
        'subln_g': 1.0 + nrm(ks[14], (DEPTH, 2 * HEAD_DIM_A), 0.02),
        'w_proj_a': nrm(ks[15], (DEPTH, WIDTH_A, D_MODEL), WIDTH_A ** -0.5),
        'w_proj_b': nrm(ks[16], (DEPTH, WIDTH_B, D_MODEL), WIDTH_B ** -0.5),
        'w_out': nrm(ks[17], (DEPTH, D_MODEL, D_MODEL), D_MODEL ** -0.5),
        'ffn_norm_g': 1.0 + nrm(ks[18], (DEPTH, D_MODEL), 0.02),
        'w_up': nrm(ks[19], (DEPTH, D_MODEL, 2 * D_FF), D_MODEL ** -0.5),
        'conv_w': nrm(ks[20], (DEPTH, CONV_WIDTH, D_FF), CONV_WIDTH ** -0.5),
        'conv_b': nrm(ks[21], (DEPTH, D_FF), 0.02),
        'w_down': nrm(ks[22], (DEPTH, D_FF, D_MODEL), D_FF ** -0.5),
    }


def reference(x_prompt, x_sample, cache_da_k, cache_da_v, cache_sb_k, cache_sb_v, state_conv,
              page_table, rel_bias, attn_norm_g, w_in, q_norm_g, k_norm_g, w_lambda, subln_g,
              w_proj_a, w_proj_b, w_out, ffn_norm_g, w_up, conv_w, conv_b, w_down):
    dec_b, n_pages = page_table.shape
    past_len = n_pages * cache_da_k.shape[2]
    pos_prompt = jnp.arange(x_prompt.shape[1], dtype=jnp.int32)
    pos_sample = past_len + jnp.arange(x_sample.shape[1], dtype=jnp.int32)
    xp, xs = x_prompt, x_sample
    rows_p = [[] for _ in range(5)]
    rows_s = [[] for _ in range(5)]
    for l in range(DEPTH):
        lw = (w_in[l], q_norm_g[l], k_norm_g[l], w_lambda[l], subln_g[l],
              w_proj_a[l], w_proj_b[l], w_out[l])
        fw = (w_up[l], conv_w[l], conv_b[l], w_down[l])
        xp, new_p = _trunk_layer(xp, pos_prompt, None, None, l, attn_norm_g[l], lw,
                                 ffn_norm_g[l], fw, rel_bias)
        past = tuple(c[l, page_table].reshape((dec_b, past_len) + c.shape[3:])
                     for c in (cache_da_k, cache_da_v, cache_sb_k, cache_sb_v))
        xs, new_s = _trunk_layer(xs, pos_sample, past, state_conv[l], l, attn_norm_g[l], lw,
                                 ffn_norm_g[l], fw, rel_bias)
        for lst, r in zip(rows_p, new_p):
            lst.append(r)
        for lst, r in zip(rows_s, new_s):
            lst.append(r)
    new_da_k_prompt, new_da_v_prompt, new_sb_k_prompt, new_sb_v_prompt, new_conv_prompt = [jnp.stack(r) for r in rows_p]
    new_da_k_sample, new_da_v_sample, new_sb_k_sample, new_sb_v_sample, new_conv_sample = [jnp.stack(r) for r in rows_s]
    return (xp, xs,
            new_da_k_prompt, new_da_v_prompt, new_sb_k_prompt, new_sb_v_prompt, new_conv_prompt,
            new_da_k_sample, new_da_v_sample, new_sb_k_sample, new_sb_v_sample, new_conv_sample)
```

```python
import functools
import math

import numpy as np
import jax
import jax.numpy as jnp
from jax import lax
from jax.experimental import pallas as pl
from jax.experimental.pallas import tpu as pltpu

F32 = jnp.float32
BF16 = jnp.bfloat16

N_HEADS_A = 8
HEAD_DIM_A = 64
WIDTH_A = N_HEADS_A * 2 * HEAD_DIM_A
N_HEADS_B = 8
HEAD_DIM_B = 64
WIDTH_B = N_HEADS_B * HEAD_DIM_B
N_BUCKETS = 32
MAX_DISTANCE = 128
LAMBDA_BASE = 0.8
LAMBDA_SCALE = 0.6
LAMBDA_RATE = 0.3
NORM_EPS = 1e-6
NEG_BIG = -1e30
CONV_WIDTH = 3

LANES = 128
SUBLANES = 8
MXU_EDGE = 256
VMEM_LIMIT_BYTES = 56 * 1024 * 1024

TOKEN_TILE = 256
PAGES_PER_STEP = 8


def _cparams(sem):
    return pltpu.CompilerParams(dimension_semantics=sem, vmem_limit_bytes=VMEM_LIMIT_BYTES)


def _const_spec(shape):
    nd = len(shape)
    return pl.BlockSpec(shape, lambda *_: (0,) * nd)


_NT = (((1,), (1,)), ((), ()))


def _rms_rows(x, g):
    return x * lax.rsqrt(jnp.mean(x * x, axis=-1, keepdims=True) + NORM_EPS) * g


def _head_norm(y, gain, gm):
    sq = (y * y).astype(BF16)
    outs = []
    for j in range(WIDTH_A // MXU_EDGE):
        sl = slice(j * MXU_EDGE, (j + 1) * MXU_EDGE)
        ms = jnp.dot(sq[:, sl], gm, preferred_element_type=F32)
        outs.append(y[:, sl] * lax.rsqrt(ms + NORM_EPS))
    return jnp.concatenate(outs, axis=1) * gain


def _in_proj_prompt_kernel(x_ref, g_ref, ws_ref, wt_ref, qg_ref, kgc_ref, gm_ref,
                           qa_ref, kat_ref, katb_ref, vai_ref, vab_ref,
                           qb_ref, kbt_ref, kbtb_ref, vbt_ref, vbb_ref, gates_ref):
    tm = x_ref.shape[0]
    nb = _rms_rows(x_ref[...], g_ref[...]).astype(BF16)

    def seg(lo, width):
        return jnp.dot(nb, ws_ref[:, lo:lo + width], preferred_element_type=F32)

    def seg_t(lo, width):
        return lax.dot_general(wt_ref[lo:lo + width, :], nb, _NT, preferred_element_type=F32)

    qa = _head_norm(seg(0, WIDTH_A), qg_ref[...], gm_ref[...]) * (HEAD_DIM_A ** -0.5)
    qa_ref[...] = qa.astype(BF16)
    va = seg(WIDTH_A, WIDTH_A)
    vab_ref[...] = va.astype(BF16)
    for h in range(N_HEADS_A):
        vai_ref[pl.ds(h, tm, stride=N_HEADS_A), :] = va[:, h * LANES:(h + 1) * LANES]
    off = 2 * WIDTH_A
    qb_ref[...] = (seg(off, WIDTH_B) * (HEAD_DIM_B ** -0.5)).astype(BF16)
    vbb_ref[...] = seg(off + WIDTH_B, WIDTH_B).astype(BF16)
    gates_ref[...] = seg(off + 2 * WIDTH_B, gates_ref.shape[1])

    kat = seg_t(0, WIDTH_A).reshape(WIDTH_A // HEAD_DIM_A, HEAD_DIM_A, tm)
    ms = jnp.mean(kat * kat, axis=1, keepdims=True)
    kat = (kat * lax.rsqrt(ms + NORM_EPS) * kgc_ref[...].reshape(1, HEAD_DIM_A, 1)).reshape(WIDTH_A, tm)
    kat_ref[0] = kat
    katb_ref[0] = kat.astype(BF16)
    kbt = seg_t(WIDTH_A, WIDTH_B)
    kbt_ref[0] = kbt
    kbtb_ref[0] = kbt.astype(BF16)
    vbt_ref[0] = seg_t(WIDTH_A + WIDTH_B, WIDTH_B)


def _in_proj_prompt(x, g, ws, wt, qg_t, kg_col, gm, batch, seq):
    n, d = x.shape
    tm = min(TOKEN_TILE, seq)
    tps = seq // tm
    gate_w = ws.shape[1] - 2 * WIDTH_A - 2 * WIDTH_B
    row = lambda w: pl.BlockSpec((tm, w), lambda i: (i, 0))
    t_out = lambda w: pl.BlockSpec((1, w, tm), lambda i: (i // tps, 0, i % tps))
    t_tile = lambda w: pl.BlockSpec((1, w, tm), lambda i: (i, 0, 0))
    out_specs = [row(WIDTH_A), t_out(WIDTH_A), t_tile(WIDTH_A),
                 pl.BlockSpec((tm * N_HEADS_A, LANES), lambda i: (i, 0)), row(WIDTH_A),
                 row(WIDTH_B), t_out(WIDTH_B), t_tile(WIDTH_B), t_out(WIDTH_B), row(WIDTH_B), row(gate_w)]
    out_shape = [jax.ShapeDtypeStruct((n, WIDTH_A), BF16),
                 jax.ShapeDtypeStruct((batch, WIDTH_A, seq), F32),
                 jax.ShapeDtypeStruct((n // tm, WIDTH_A, tm), BF16),
                 jax.ShapeDtypeStruct((n * N_HEADS_A, LANES), F32),
                 jax.ShapeDtypeStruct((n, WIDTH_A), BF16),
                 jax.ShapeDtypeStruct((n, WIDTH_B), BF16),
                 jax.ShapeDtypeStruct((batch, WIDTH_B, seq), F32),
                 jax.ShapeDtypeStruct((n // tm, WIDTH_B, tm), BF16),
                 jax.ShapeDtypeStruct((batch, WIDTH_B, seq), F32),
                 jax.ShapeDtypeStruct((n, WIDTH_B), BF16),
                 jax.ShapeDtypeStruct((n, gate_w), F32)]
    return pl.pallas_call(
        _in_proj_prompt_kernel,
        grid=(n // tm,),
        in_specs=[row(d), _const_spec(g.shape), _const_spec(ws.shape), _const_spec(wt.shape),
                  _const_spec(qg_t.shape), _const_spec(kg_col.shape), _const_spec(gm.shape)],
        out_specs=out_specs,
        out_shape=out_shape,
        compiler_params=_cparams(("arbitrary",)),
        name="in_proj_prompt",
    )(x, g, ws, wt, qg_t, kg_col, gm)


def _in_proj_sample_kernel(x_ref, g_ref, ws_ref, wk_ref, qg_ref, kg_ref, gm_ref,
                           qa_ref, ka_ref, va_ref, qb_ref, kb_ref, vb_ref, gates_ref):
    nb = _rms_rows(x_ref[...], g_ref[...]).astype(BF16)

    def seg(w_ref, lo, width):
        return jnp.dot(nb, w_ref[:, lo:lo + width], preferred_element_type=F32)

    qa_ref[...] = _head_norm(seg(ws_ref, 0, WIDTH_A), qg_ref[...], gm_ref[...]) * (HEAD_DIM_A ** -0.5)
    va_ref[...] = seg(ws_ref, WIDTH_A, WIDTH_A)
    off = 2 * WIDTH_A
    qb_ref[...] = seg(ws_ref, off, WIDTH_B) * (HEAD_DIM_B ** -0.5)
    vb_ref[...] = seg(ws_ref, off + WIDTH_B, WIDTH_B)
    gates_ref[...] = seg(ws_ref, off + 2 * WIDTH_B, gates_ref.shape[1])
    ka_ref[...] = _head_norm(seg(wk_ref, 0, WIDTH_A), kg_ref[...], gm_ref[...])
    kb_ref[...] = seg(wk_ref, WIDTH_A, WIDTH_B)


def _in_proj_sample(x, g, ws, wk, qg_t, kg_t, gm):
    n, d = x.shape
    tm = min(TOKEN_TILE, n)
    gate_w = ws.shape[1] - 2 * WIDTH_A - 2 * WIDTH_B
    row = lambda w: pl.BlockSpec((tm, w), lambda i: (i, 0))
    widths = [WIDTH_A, WIDTH_A, WIDTH_A, WIDTH_B, WIDTH_B, WIDTH_B, gate_w]
    return pl.pallas_call(
        _in_proj_sample_kernel,
        grid=(n // tm,),
        in_specs=[row(d), _const_spec(g.shape), _const_spec(ws.shape), _const_spec(wk.shape),
                  _const_spec(qg_t.shape), _const_spec(kg_t.shape), _const_spec(gm.shape)],
        out_specs=[row(w) for w in widths],
        out_shape=[jax.ShapeDtypeStruct((n, w), F32) for w in widths],
        compiler_params=_cparams(("arbitrary",)),
        name="in_proj_sample",
    )(x, g, ws, wk, qg_t, kg_t, gm)


def _lambda_value(wl, lam_init):
    a = jnp.sum(wl[0:1] * wl[1:2], axis=1, keepdims=True)
    b = jnp.sum(wl[2:3] * wl[3:4], axis=1, keepdims=True)
    return jnp.exp(a) - jnp.exp(b) + lam_init


def _sub_norm(d, gain, lam_init):
    y = d * lax.rsqrt(jnp.mean(d * d, axis=-1, keepdims=True) + NORM_EPS)
    return y * gain * (1.0 - lam_init)


def _softmax_probs(s, m_sc, l_sc):
    m_prev = m_sc[...]
    m_new = jnp.maximum(m_prev, jnp.max(s, axis=-1, keepdims=True))
    alpha = jnp.exp(m_prev - m_new)
    p = jnp.exp(s - m_new)
    l_sc[...] = alpha * l_sc[...] + jnp.sum(p, axis=-1, keepdims=True)
    m_sc[...] = m_new
    return alpha, p.astype(BF16)


def _stick_weights(z, u_hi_lo, carry_sc, valid):
    u = jnp.log(1.0 + jnp.exp(-jnp.abs(z)))
    ls = -(jnp.maximum(z, 0.0) + u)
    if valid is not None:
        ls = jnp.where(valid, ls, 0.0)
    hi = ls.astype(BF16)
    lo = (ls - hi.astype(F32)).astype(BF16)
    later = jnp.dot(jnp.concatenate([hi, lo], axis=1), u_hi_lo, preferred_element_type=F32)
    a = jnp.exp(jnp.minimum(z, 0.0) - u + later + carry_sc[...])
    if valid is not None:
        a = jnp.where(valid, a, 0.0)
    carry_sc[...] = carry_sc[...] + jnp.sum(ls, axis=-1, keepdims=True)
    return a.astype(BF16)


def _split_lanes(q):
    lane = lax.broadcasted_iota(jnp.int32, q.shape, 1)
    zero = jnp.zeros_like(q)
    half = LANES // 2
    return jnp.concatenate([jnp.where(lane < half, q, zero), jnp.where(lane >= half, q, zero)], axis=0)


def _da_prompt_kernel(q_ref, kt_ref, v_ref, bias_ref, wl_ref, sg_ref, o_ref, m_sc, l_sc, acc_sc,
                      *, tq, lam_init):
    qi = pl.program_id(2)
    qq = _split_lanes(q_ref[...])
    m_sc[...] = jnp.full(m_sc.shape, NEG_BIG, F32)
    l_sc[...] = jnp.zeros(l_sc.shape, F32)
    acc_sc[...] = jnp.zeros(acc_sc.shape, F32)

    def block(ki, bias):
        vb = v_ref[pl.ds(pl.multiple_of(ki * tq, tq), tq), :]
        s = jnp.dot(qq, kt_ref[ki], preferred_element_type=F32)
        if bias is not None:
            s = (s.reshape(2, tq, tq) + bias[None]).reshape(2 * tq, tq)
        alpha, p = _softmax_probs(s, m_sc, l_sc)
        acc_sc[...] = alpha * acc_sc[...] + jnp.dot(p, vb, preferred_element_type=F32)

    def far(ki, c):
        block(ki, None)
        return c

    lax.fori_loop(0, qi - 1, far, 0)

    @pl.when(qi >= 1)
    def _():
        block(qi - 1, bias_ref[0, 0])

    block(qi, bias_ref[0, 1])

    o = acc_sc[...] * (1.0 / l_sc[...])
    lam = _lambda_value(wl_ref[...], lam_init)
    d = o[:tq] - lam * o[tq:]
    o_ref[...] = _sub_norm(d, sg_ref[...], lam_init).astype(o_ref.dtype)


def _da_prompt(q, kt, v, bias, wl, sg, batch, seq, lam_init):
    n = q.shape[0]
    tq = kt.shape[2]
    nq = seq // tq
    kern = functools.partial(_da_prompt_kernel, tq=tq, lam_init=lam_init)
    return pl.pallas_call(
        kern,
        grid=(batch, N_HEADS_A, nq),
        in_specs=[pl.BlockSpec((tq, LANES), lambda b, h, i: (b * nq + i, h)),
                  pl.BlockSpec((nq, LANES, tq), lambda b, h, i: (b, h, 0)),
                  pl.BlockSpec((seq, LANES), lambda b, h, i: (b, h)),
                  pl.BlockSpec((1, 2, tq, tq), lambda b, h, i: (h, 0, 0, 0)),
                  _const_spec(wl.shape), _const_spec(sg.shape)],
        out_specs=pl.BlockSpec((tq, LANES), lambda b, h, i: (b * nq + i, h)),
        out_shape=jax.ShapeDtypeStruct((n, WIDTH_A), BF16),
        scratch_shapes=[pltpu.VMEM((2 * tq, 1), F32), pltpu.VMEM((2 * tq, 1), F32),
                        pltpu.VMEM((2 * tq, LANES), F32)],
        compiler_params=_cparams(("arbitrary", "arbitrary", "arbitrary")),
        name="da_prompt",
    )(q, kt, v, bias, wl, sg)


def _sb_prompt_kernel(q_ref, kt_ref, v_ref, u_ref, o_ref, carry_sc, acc_sc, *, tq):
    qi = pl.program_id(2)
    qq = _split_lanes(q_ref[...])
    carry_sc[...] = jnp.zeros(carry_sc.shape, F32)
    acc_sc[...] = jnp.zeros(acc_sc.shape, F32)

    def block(ki, valid):
        vb = v_ref[pl.ds(pl.multiple_of(ki * tq, tq), tq), :]
        z = jnp.dot(qq, kt_ref[ki], preferred_element_type=F32)
        a = _stick_weights(z, u_ref[...], carry_sc, valid)
        acc_sc[...] = acc_sc[...] + jnp.dot(a, vb, preferred_element_type=F32)

    row = lax.broadcasted_iota(jnp.int32, (2, tq, tq), 1).reshape(2 * tq, tq)
    col = lax.broadcasted_iota(jnp.int32, (2 * tq, tq), 1)
    block(qi, col < row)

    def left(i, c):
        block(qi - 1 - i, None)
        return c

    lax.fori_loop(0, qi, left, 0)

    acc = acc_sc[...]
    lane = lax.broadcasted_iota(jnp.int32, (tq, LANES), 1)
    o_ref[...] = jnp.where(lane < LANES // 2, acc[:tq], acc[tq:]).astype(o_ref.dtype)


def _sb_prompt(q, kt, v, u_hi_lo, batch, seq):
    n = q.shape[0]
    tq = kt.shape[2]
    nq = seq // tq
    pairs = WIDTH_B // LANES
    kern = functools.partial(_sb_prompt_kernel, tq=tq)
    return pl.pallas_call(
        kern,
        grid=(batch, pairs, nq),
        in_specs=[pl.BlockSpec((tq, LANES), lambda b, h, i: (b * nq + i, h)),
                  pl.BlockSpec((nq, LANES, tq), lambda b, h, i: (b, h, 0)),
                  pl.BlockSpec((seq, LANES), lambda b, h, i: (b, h)),
                  _const_spec(u_hi_lo.shape)],
        out_specs=pl.BlockSpec((tq, LANES), lambda b, h, i: (b * nq + i, h)),
        out_shape=jax.ShapeDtypeStruct((n, WIDTH_B), BF16),
        scratch_shapes=[pltpu.VMEM((2 * tq, 1), F32), pltpu.VMEM((2 * tq, LANES), F32)],
        compiler_params=_cparams(("arbitrary", "arbitrary", "arbitrary")),
        name="sb_prompt",
    )(q, kt, v, u_hi_lo)


def _block_diag_queries(q, groups, width):
    t = q.shape[0]
    rep = jnp.concatenate([q] * groups, axis=0)
    r = lax.broadcasted_iota(jnp.int32, rep.shape, 0)
    c = lax.broadcasted_iota(jnp.int32, rep.shape, 1)
    keep = (c // width) == (r // t)
    return jnp.where(keep, rep, 0.0).astype(BF16)


def _pad_rows(x, rows):
    return jnp.concatenate([x, jnp.zeros((rows - x.shape[0], x.shape[1]), x.dtype)], axis=0)


def _da_sample_kernel(pt_ref, q_ref, knew_ref, vnew_ref, blast_ref, bnew_ref, wl_ref, sg_ref, *rest,
                      pages, page, lam_init):
    del pt_ref
    kt_refs, v_refs = rest[:pages], rest[pages:2 * pages]
    o_ref, qbd_sc, m_sc, l_sc, acc_sc = rest[2 * pages:]
    g = pl.program_id(1)
    last = pl.num_programs(1) - 1
    t = q_ref.shape[0]
    rows_h = 2 * t

    @pl.when(g == 0)
    def _():
        qbd_sc[...] = _block_diag_queries(q_ref[...], 2 * N_HEADS_A, HEAD_DIM_A)
        m_sc[...] = jnp.full(m_sc.shape, NEG_BIG, F32)
        l_sc[...] = jnp.zeros(l_sc.shape, F32)
        acc_sc[...] = jnp.zeros(acc_sc.shape, F32)

    qbd = qbd_sc[...]
    ktcat = jnp.concatenate([r[0, 0].astype(BF16) for r in kt_refs], axis=1)
    s = jnp.dot(qbd, ktcat, preferred_element_type=F32)
    s = s + blast_ref[...] * (g == last).astype(F32)
    alpha, p = _softmax_probs(s, m_sc, l_sc)
    for h in range(N_HEADS_A):
        vh = jnp.concatenate([r[0, 0, pl.ds(h, page, stride=N_HEADS_A), :] for r in v_refs], axis=0)
        rs = slice(h * rows_h, (h + 1) * rows_h)
        acc_sc[rs, :] = alpha[rs] * acc_sc[rs, :] + jnp.dot(p[rs], vh.astype(BF16), preferred_element_type=F32)

    @pl.when(g == last)
    def _():
        kn = _pad_rows(knew_ref[...], page).astype(BF16)
        vn = _pad_rows(vnew_ref[...], page).astype(BF16)
        s2 = lax.dot_general(qbd, kn, _NT, preferred_element_type=F32) + bnew_ref[...]
        alpha2, p2 = _softmax_probs(s2, m_sc, l_sc)
        o_all = jnp.dot(p2, vn, preferred_element_type=F32)
        inv_l = 1.0 / l_sc[...]
        lam = _lambda_value(wl_ref[...], lam_init)
        outs = []
        for h in range(N_HEADS_A):
            rs = slice(h * rows_h, (h + 1) * rows_h)
            cols = slice(h * 2 * HEAD_DIM_A, (h + 1) * 2 * HEAD_DIM_A)
            o = (alpha2[rs] * acc_sc[rs, :] + o_all[rs, cols]) * inv_l[rs]
            outs.append(_sub_norm(o[:t] - lam * o[t:], sg_ref[...], lam_init))
        o_ref[...] = jnp.concatenate(outs, axis=1)


def _page_specs(layer, n_pages, pages, block, reverse):
    specs = []
    for j in range(pages):
        if reverse:
            idx = lambda b, g, pt, j=j: (layer, pt[b * n_pages + n_pages - (g + 1) * pages + j], 0, 0)
        else:
            idx = lambda b, g, pt, j=j: (layer, pt[b * n_pages + g * pages + j], 0, 0)
        specs.append(pl.BlockSpec((1, 1) + block, idx))
    return specs


def _da_sample(pt, q, knew, vnew, blast, bnew, wl, sg, cache_kt, cache_vi, layer, dec_b, dec_t, lam_init):
    n_pages = pt.shape[0] // dec_b
    page = cache_kt.shape[3]
    pages = min(PAGES_PER_STEP, n_pages)
    rows = 2 * N_HEADS_A * dec_t
    tok = lambda w: pl.BlockSpec((dec_t, w), lambda b, g, pt: (b, 0))
    const = lambda a: pl.BlockSpec(a.shape, lambda b, g, pt: (0,) * a.ndim)
    kern = functools.partial(_da_sample_kernel, pages=pages, page=page, lam_init=lam_init)
    grid_spec = pltpu.PrefetchScalarGridSpec(
        num_scalar_prefetch=1,
        grid=(dec_b, n_pages // pages),
        in_specs=[tok(WIDTH_A), tok(WIDTH_A), tok(WIDTH_A), const(blast), const(bnew), const(wl), const(sg)]
        + _page_specs(layer, n_pages, pages, (WIDTH_A, page), False)
        + _page_specs(layer, n_pages, pages, (page * N_HEADS_A, 2 * HEAD_DIM_A), False),
        out_specs=tok(WIDTH_A),
        scratch_shapes=[pltpu.VMEM((rows, WIDTH_A), BF16), pltpu.VMEM((rows, 1), F32),
                        pltpu.VMEM((rows, 1), F32), pltpu.VMEM((rows, 2 * HEAD_DIM_A), F32)],
    )
    return pl.pallas_call(
        kern,
        grid_spec=grid_spec,
        out_shape=jax.ShapeDtypeStruct((dec_b * dec_t, WIDTH_A), F32),
        compiler_params=_cparams(("arbitrary", "arbitrary")),
        name="da_sample",
    )(pt, q, knew, vnew, blast, bnew, wl, sg, *([cache_kt] * pages), *([cache_vi] * pages))


def _sb_sample_kernel(pt_ref, q_ref, knew_ref, vnew_ref, u_ref, *rest, pages, page):
    del pt_ref
    kt_refs, vt_refs = rest[:pages], rest[pages:2 * pages]
    o_ref, qbd_sc, carry_sc, acc_sc = rest[2 * pages:]
    g = pl.program_id(1)
    last = pl.num_programs(1) - 1
    t = q_ref.shape[0]
    blk = u_ref.shape[1]

    @pl.when(g == 0)
    def _():
        qbd = _block_diag_queries(q_ref[...], N_HEADS_B, HEAD_DIM_B)
        qbd_sc[...] = qbd
        carry_sc[...] = jnp.zeros(carry_sc.shape, F32)
        kn = _pad_rows(knew_ref[...], page).astype(BF16)
        vn = _pad_rows(vnew_ref[...], page).astype(BF16)
        z = lax.dot_general(qbd, kn, _NT, preferred_element_type=F32)
        row_t = lax.broadcasted_iota(jnp.int32, z.shape, 0) % t
        col = lax.broadcasted_iota(jnp.int32, z.shape, 1)
        u_page = jnp.concatenate([u_ref[0:page, 0:page], u_ref[0:page, 0:page]], axis=0)
        a = _stick_weights(z, u_page, carry_sc, col < row_t)
        acc_sc[...] = jnp.dot(a, vn, preferred_element_type=F32)

    qbd = qbd_sc[...]
    ktcat = jnp.concatenate([r[0, 0].astype(BF16) for r in kt_refs], axis=1)
    vtcat = jnp.concatenate([r[0, 0].astype(BF16) for r in vt_refs], axis=1)
    z = jnp.dot(qbd, ktcat, preferred_element_type=F32)
    for b in reversed(range(pages * page // blk)):
        sl = slice(b * blk, (b + 1) * blk)
        a = _stick_weights(z[:, sl], u_ref[...], carry_sc, None)
        acc_sc[...] = acc_sc[...] + lax.dot_general(a, vtcat[:, sl], _NT, preferred_element_type=F32)

    @pl.when(g == last)
    def _():
        acc = acc_sc[...]
        col_h = lax.broadcasted_iota(jnp.int32, (t, WIDTH_B), 1) // HEAD_DIM_B
        out = jnp.zeros((t, WIDTH_B), F32)
        for h in range(N_HEADS_B):
            out = out + jnp.where(col_h == h, acc[h * t:(h + 1) * t, :], 0.0)
        o_ref[...] = out


def _sb_sample(pt, q, knew, vnew, u_hi_lo, cache_kt, cache_vt, layer, dec_b, dec_t):
    n_pages = pt.shape[0] // dec_b
    page = cache_kt.shape[3]
    pages = min(PAGES_PER_STEP, n_pages)
    rows = N_HEADS_B * dec_t
    tok = lambda w: pl.BlockSpec((dec_t, w), lambda b, g, pt: (b, 0))
    const = lambda a: pl.BlockSpec(a.shape, lambda b, g, pt: (0,) * a.ndim)
    kern = functools.partial(_sb_sample_kernel, pages=pages, page=page)
    grid_spec = pltpu.PrefetchScalarGridSpec(
        num_scalar_prefetch=1,
        grid=(dec_b, n_pages // pages),
        in_specs=[tok(WIDTH_B), tok(WIDTH_B), tok(WIDTH_B), const(u_hi_lo)]
        + _page_specs(layer, n_pages, pages, (WIDTH_B, page), True)
        + _page_specs(layer, n_pages, pages, (WIDTH_B, page), True),
        out_specs=tok(WIDTH_B),
        scratch_shapes=[pltpu.VMEM((rows, WIDTH_B), BF16), pltpu.VMEM((rows, 1), F32),
                        pltpu.VMEM((rows, WIDTH_B), F32)],
    )
    return pl.pallas_call(
        kern,
        grid_spec=grid_spec,
        out_shape=jax.ShapeDtypeStruct((dec_b * dec_t, WIDTH_B), F32),
        compiler_params=_cparams(("arbitrary", "arbitrary")),
        name="sb_sample",
    )(pt, q, knew, vnew, u_hi_lo, *([cache_kt] * pages), *([cache_vt] * pages))


def _merge_out_kernel(x_ref, oa_ref, ob_ref, gates_ref, wpa_ref, wpb_ref, wo_ref, o_ref):
    d = x_ref.shape[1]
    ga = jax.nn.sigmoid(gates_ref[:, :d])
    gb = jax.nn.sigmoid(gates_ref[:, d:])
    pa = jnp.dot(oa_ref[...].astype(BF16), wpa_ref[...], preferred_element_type=F32)
    pb = jnp.dot(ob_ref[...].astype(BF16), wpb_ref[...], preferred_element_type=F32)
    merged = ga * pa + gb * pb
    o_ref[...] = x_ref[...] + jnp.dot(merged.astype(BF16), wo_ref[...], preferred_element_type=F32)


def _merge_out(x, oa, ob, gates, wpa, wpb, wo):
    n, d = x.shape
    tm = min(TOKEN_TILE, n)
    row = lambda w: pl.BlockSpec((tm, w), lambda i: (i, 0))
    return pl.pallas_call(
        _merge_out_kernel,
        grid=(n // tm,),
        in_specs=[row(d), row(WIDTH_A), row(WIDTH_B), row(2 * d),
                  _const_spec(wpa.shape), _const_spec(wpb.shape), _const_spec(wo.shape)],
        out_specs=row(d),
        out_shape=jax.ShapeDtypeStruct((n, d), F32),
        compiler_params=_cparams(("arbitrary",)),
        name="merge_out",
    )(x, oa, ob, gates, wpa, wpb, wo)


def _gelu_exact(x):
    return 0.5 * x * (1.0 + lax.erf(x * np.float32(math.sqrt(0.5))))


def _ffn_up(x_ref, g_ref, wup_ref, ff):
    x = x_ref[...]
    n2 = _rms_rows(x, g_ref[...]).astype(BF16)
    a = jnp.dot(n2, wup_ref[:, :ff], preferred_element_type=F32)
    b = jnp.dot(n2, wup_ref[:, ff:], preferred_element_type=F32)
    return x, a, b


def _ffn_down(x, a_m1, a_m2, a, b, cw_ref, cb_ref, wdn_ref):
    conv = cb_ref[...] + a_m2 * cw_ref[0:1, :] + a_m1 * cw_ref[1:2, :] + a * cw_ref[2:3, :]
    h = (_gelu_exact(conv) * b).astype(BF16)
    return x + jnp.dot(h, wdn_ref[...], preferred_element_type=F32)


def _ffn_prompt_kernel(x_ref, g_ref, wup_ref, cw_ref, cb_ref, wdn_ref, y_ref, tail_ref, carry_sc,
                       *, tiles_per_seq):
    ff = wdn_ref.shape[0]
    tm = x_ref.shape[0]
    x, a, b = _ffn_up(x_ref, g_ref, wup_ref, ff)
    first = (pl.program_id(0) % tiles_per_seq) == 0
    prev = jnp.where(first, 0.0, carry_sc[...])
    p6 = prev[SUBLANES - 2:SUBLANES - 1, :]
    p7 = prev[SUBLANES - 1:SUBLANES, :]
    row = lax.broadcasted_iota(jnp.int32, a.shape, 0)
    a_m1 = jnp.where(row == 0, p7, pltpu.roll(a, 1, 0))
    a_m2 = jnp.where(row == 0, p6, jnp.where(row == 1, p7, pltpu.roll(a, 2, 0)))
    last_rows = a[tm - SUBLANES:, :]
    carry_sc[...] = last_rows
    tail_ref[0] = last_rows
    y_ref[...] = _ffn_down(x, a_m1, a_m2, a, b, cw_ref, cb_ref, wdn_ref)


def _ffn_sample_kernel(x_ref, g_ref, wup_ref, cw_ref, cb_ref, wdn_ref, s1_ref, s2_ref, y_ref, a_ref,
                       *, dec_t):
    ff = wdn_ref.shape[0]
    x, a, b = _ffn_up(x_ref, g_ref, wup_ref, ff)
    t = lax.broadcasted_iota(jnp.int32, a.shape, 0) % dec_t
    a_m1 = jnp.where(t >= 1, pltpu.roll(a, 1, 0), s1_ref[...])
    a_m2 = jnp.where(t >= 2, pltpu.roll(a, 2, 0), s2_ref[...])
    a_ref[...] = a
    y_ref[...] = _ffn_down(x, a_m1, a_m2, a, b, cw_ref, cb_ref, wdn_ref)


def _ffn_prompt(x, g, wup, cw, cb, wdn, batch, seq):
    n, d = x.shape
    ff = wdn.shape[0]
    tm = min(TOKEN_TILE, seq)
    tiles_per_seq = seq // tm
    row = lambda w: pl.BlockSpec((tm, w), lambda i: (i, 0))
    kern = functools.partial(_ffn_prompt_kernel, tiles_per_seq=tiles_per_seq)
    return pl.pallas_call(
        kern,
        grid=(n // tm,),
        in_specs=[row(d), _const_spec(g.shape), _const_spec(wup.shape), _const_spec(cw.shape),
                  _const_spec(cb.shape), _const_spec(wdn.shape)],
        out_specs=[row(d), pl.BlockSpec((1, SUBLANES, ff), lambda i: (i // tiles_per_seq, 0, 0))],
        out_shape=[jax.ShapeDtypeStruct((n, d), F32), jax.ShapeDtypeStruct((batch, SUBLANES, ff), F32)],
        scratch_shapes=[pltpu.VMEM((SUBLANES, ff), F32)],
        compiler_params=_cparams(("arbitrary",)),
        name="ffn_prompt",
    )(x, g, wup, cw, cb, wdn)


def _ffn_sample(x, g, wup, cw, cb, wdn, s1, s2, dec_t):
    n, d = x.shape
    ff = wdn.shape[0]
    tm = min(TOKEN_TILE, n)
    row = lambda w: pl.BlockSpec((tm, w), lambda i: (i, 0))
    kern = functools.partial(_ffn_sample_kernel, dec_t=dec_t)
    return pl.pallas_call(
        kern,
        grid=(n // tm,),
        in_specs=[row(d), _const_spec(g.shape), _const_spec(wup.shape), _const_spec(cw.shape),
                  _const_spec(cb.shape), _const_spec(wdn.shape), row(ff), row(ff)],
        out_specs=[row(d), row(ff)],
        out_shape=[jax.ShapeDtypeStruct((n, d), F32), jax.ShapeDtypeStruct((n, ff), F32)],
        compiler_params=_cparams(("arbitrary",)),
        name="ffn_sample",
    )(x, g, wup, cw, cb, wdn, s1, s2)


def _bucket_table(max_dist):
    n = np.arange(max_dist + 1, dtype=np.int32)
    max_exact = N_BUCKETS // 2
    nf = np.maximum(n, 1).astype(np.float32)
    large = max_exact + (np.log(nf / np.float32(max_exact)) / np.float32(math.log(MAX_DISTANCE / max_exact))
                         * np.float32(N_BUCKETS - max_exact)).astype(np.int32)
    large = np.minimum(large, N_BUCKETS - 1)
    return np.where(n < max_exact, n, large).astype(np.int32)


def _bias_tile(rel_shift, dist):
    d = np.asarray(dist)
    buckets = _bucket_table(int(max(d.max(), 0)))[np.maximum(d, 0)]
    tile = jnp.moveaxis(rel_shift[buckets], -1, 0)
    return jnp.where(jnp.asarray(d >= 0)[None], tile, NEG_BIG)


def _lower_triangles(w):
    j = np.arange(w)[:, None]
    s = np.arange(w)[None, :]
    u = (j > s).astype(np.float32)
    return jnp.asarray(np.concatenate([u, u], axis=0), dtype=BF16)


def _group_mean_matrix():
    i = np.arange(MXU_EDGE)
    m = (i[:, None] // HEAD_DIM_A == i[None, :] // HEAD_DIM_A).astype(np.float32) / HEAD_DIM_A
    return jnp.asarray(m, dtype=BF16)


def kernel(x_prompt, x_sample, cache_da_k, cache_da_v, cache_sb_k, cache_sb_v, state_conv, page_table,
           rel_bias, attn_norm_g, w_in, q_norm_g, k_norm_g, w_lambda, subln_g, w_proj_a, w_proj_b, w_out,
           ffn_norm_g, w_up, conv_w, conv_b, w_down):
    batch, seq, d = x_prompt.shape
    dec_b, dec_t, _ = x_sample.shape
    depth = w_in.shape[0]
    n_pool, page = cache_da_k.shape[1], cache_da_k.shape[2]
    n_pages = page_table.shape[1]
    past_len = n_pages * page
    ff = w_down.shape[1]

    pt = page_table.reshape(-1).astype(jnp.int32)
    cda_kt = jnp.swapaxes(cache_da_k.reshape(depth, n_pool, page, WIDTH_A), 2, 3)
    cda_vi = cache_da_v.reshape(depth, n_pool, page * N_HEADS_A, 2 * HEAD_DIM_A)
    csb_kt = jnp.swapaxes(cache_sb_k.reshape(depth, n_pool, page, WIDTH_B), 2, 3)
    csb_vt = jnp.swapaxes(cache_sb_v.reshape(depth, n_pool, page, WIDTH_B), 2, 3)

    rel_shift = (rel_bias - rel_bias[N_BUCKETS - 1]).astype(F32)
    tq = min(TOKEN_TILE, seq)
    ql = np.arange(tq)[:, None]
    kl = np.arange(tq)[None, :]
    bias_prompt = jnp.stack([_bias_tile(rel_shift, ql - kl + tq), _bias_tile(rel_shift, ql - kl)], axis=1)

    pages = min(PAGES_PER_STEP, n_pages)
    groups = 2 * N_HEADS_A
    t_of_row = np.tile(np.arange(dec_t), groups)[:, None]
    head_of_row = np.repeat(np.arange(N_HEADS_A), 2 * dec_t)
    step_keys = (n_pages - pages) * page + np.arange(pages * page)[None, :]
    dist_last = past_len + t_of_row - step_keys
    blast = _bias_tile(rel_shift, dist_last)[head_of_row, np.arange(groups * dec_t)]
    u_new = np.arange(page)[None, :]
    dist_new = np.where(u_new < dec_t, t_of_row - u_new, -1)
    bnew = _bias_tile(rel_shift, dist_new)[head_of_row, np.arange(groups * dec_t)]

    u_hi_lo = _lower_triangles(tq)
    gm = _group_mean_matrix()

    c_qa, c_ka, c_va = 0, WIDTH_A, 2 * WIDTH_A
    c_qb, c_kb, c_vb, c_g = 3 * WIDTH_A, 3 * WIDTH_A + WIDTH_B, 3 * WIDTH_A + 2 * WIDTH_B, 3 * WIDTH_A + 3 * WIDTH_B

    xp = x_prompt.reshape(batch * seq, d)
    xs = x_sample.reshape(dec_b * dec_t, d)
    rows_p = [[] for _ in range(5)]
    rows_s = [[] for _ in range(5)]
    for l in range(depth):
        lam_init = LAMBDA_BASE - LAMBDA_SCALE * math.exp(-LAMBDA_RATE * l)
        g_attn = attn_norm_g[l].reshape(1, d)
        wl_in = w_in[l].astype(BF16)
        w_std = jnp.concatenate([wl_in[:, c_qa:c_ka], wl_in[:, c_va:c_qb], wl_in[:, c_qb:c_kb],
                                 wl_in[:, c_vb:c_g], wl_in[:, c_g:]], axis=1)
        w_keys = jnp.concatenate([wl_in[:, c_ka:c_va], wl_in[:, c_kb:c_vb]], axis=1)
        w_t = jnp.concatenate([w_keys, wl_in[:, c_vb:c_g]], axis=1).T
        qg_t = jnp.tile(q_norm_g[l], WIDTH_A // HEAD_DIM_A).reshape(1, WIDTH_A)
        kg_t = jnp.tile(k_norm_g[l], WIDTH_A // HEAD_DIM_A).reshape(1, WIDTH_A)
        kg_col = k_norm_g[l].reshape(HEAD_DIM_A, 1)
        wl = w_lambda[l]
        sg = subln_g[l].reshape(1, 2 * HEAD_DIM_A)
        wpa, wpb, wo = w_proj_a[l].astype(BF16), w_proj_b[l].astype(BF16), w_out[l].astype(BF16)
        g_ffn = ffn_norm_g[l].reshape(1, d)
        wup, wdn = w_up[l].astype(BF16), w_down[l].astype(BF16)
        cw, cb = conv_w[l], conv_b[l].reshape(1, ff)

        (qa, kat, katb, vai, vab, qb, kbt, kbtb, vbt, vbb, gates) = _in_proj_prompt(
            xp, g_attn, w_std, w_t, qg_t, kg_col, gm, batch, seq)
        oa = _da_prompt(qa, katb, vab, bias_prompt, wl, sg, batch, seq, lam_init)
        ob = _sb_prompt(qb, kbtb, vbb, u_hi_lo, batch, seq)
        xp = _merge_out(xp, oa, ob, gates, wpa, wpb, wo)
        xp, tail = _ffn_prompt(xp, g_ffn, wup, cw, cb, wdn, batch, seq)
        new_p = (jnp.transpose(kat.reshape(batch, N_HEADS_A, 2, HEAD_DIM_A, seq), (0, 4, 1, 2, 3)),
                 vai.reshape(batch, seq, N_HEADS_A, 2 * HEAD_DIM_A),
                 jnp.transpose(kbt.reshape(batch, N_HEADS_B, HEAD_DIM_B, seq), (0, 3, 1, 2)),
                 jnp.transpose(vbt.reshape(batch, N_HEADS_B, HEAD_DIM_B, seq), (0, 3, 1, 2)),
                 tail[:, SUBLANES - (CONV_WIDTH - 1):, :])

        qa, ka, va, qb, kb, vb, gates = _in_proj_sample(xs, g_attn, w_std, w_keys, qg_t, kg_t, gm)
        oa = _da_sample(pt, qa, ka, va, blast, bnew, wl, sg, cda_kt, cda_vi, l, dec_b, dec_t, lam_init)
        ob = _sb_sample(pt, qb, kb, vb, u_hi_lo, csb_kt, csb_vt, l, dec_b, dec_t)
        xs = _merge_out(xs, oa, ob, gates, wpa, wpb, wo)
        st = state_conv[l]
        zeros = jnp.zeros((dec_b, dec_t - 1, ff), F32)
        s1 = jnp.concatenate([st[:, 1:2], zeros], axis=1).reshape(dec_b * dec_t, ff)
        s2 = jnp.concatenate([st, zeros[:, 1:]], axis=1).reshape(dec_b * dec_t, ff)
        xs, a_s = _ffn_sample(xs, g_ffn, wup, cw, cb, wdn, s1, s2, dec_t)
        new_s = (ka.reshape(dec_b, dec_t, N_HEADS_A, 2, HEAD_DIM_A),
                 va.reshape(dec_b, dec_t, N_HEADS_A, 2 * HEAD_DIM_A),
                 kb.reshape(dec_b, dec_t, N_HEADS_B, HEAD_DIM_B),
                 vb.reshape(dec_b, dec_t, N_HEADS_B, HEAD_DIM_B),
                 a_s.reshape(dec_b, dec_t, ff)[:, dec_t - (CONV_WIDTH - 1):, :])
        for lst, r in zip(rows_p, new_p):
            lst.append(r)
        for lst, r in zip(rows_s, new_s):
            lst.append(r)

    out_p = [jnp.stack(r) for r in rows_p]
    out_s = [jnp.stack(r) for r in rows_s]
    return (xp.reshape(batch, seq, d), xs.reshape(dec_b, dec_t, d), *out_p, *out_s)
```

```python
import functools
import math

import numpy as np
import jax
import jax.numpy as jnp
from jax import lax
from jax.experimental import pallas as pl
from jax.experimental.pallas import tpu as pltpu

F32 = jnp.float32
BF16 = jnp.bfloat16

N_HEADS_A = 8
HEAD_DIM_A = 64
WIDTH_A = N_HEADS_A * 2 * HEAD_DIM_A
N_HEADS_B = 8
HEAD_DIM_B = 64
WIDTH_B = N_HEADS_B * HEAD_DIM_B
N_BUCKETS = 32
MAX_DISTANCE = 128
LAMBDA_BASE = 0.8
LAMBDA_SCALE = 0.6
LAMBDA_RATE = 0.3
NORM_EPS = 1e-6
NEG_BIG = -1e30
CONV_WIDTH = 3

LANES = 128
SUBLANES = 8
MXU_EDGE = 256
VMEM_LIMIT_BYTES = 56 * 1024 * 1024

TOKEN_TILE = 256
DA_TILE = 512
SB_TILE = MXU_EDGE
PAGES_PER_STEP = 8

LOG2E = math.log2(math.e)
STICK_UNDERFLOW = -104.0


def _cparams(sem):
    return pltpu.CompilerParams(dimension_semantics=sem, vmem_limit_bytes=VMEM_LIMIT_BYTES)


def _const_spec(shape):
    nd = len(shape)
    return pl.BlockSpec(shape, lambda *_: (0,) * nd)


_NT = (((1,), (1,)), ((), ()))


def _rms_rows(x, g):
    return x * lax.rsqrt(jnp.mean(x * x, axis=-1, keepdims=True) + NORM_EPS) * g


def _head_norm(y, gain, gm):
    sq = (y * y).astype(BF16)
    outs = []
    for j in range(WIDTH_A // MXU_EDGE):
        sl = slice(j * MXU_EDGE, (j + 1) * MXU_EDGE)
        ms = jnp.dot(sq[:, sl], gm, preferred_element_type=F32)
        outs.append(y[:, sl] * lax.rsqrt(ms + NORM_EPS))
    return jnp.concatenate(outs, axis=1) * gain


def _in_proj_prompt_kernel(x_ref, g_ref, ws_ref, wt_ref, qg_ref, kgc_ref, gm_ref,
                           qa_ref, kat_ref, katb_ref, vai_ref, vab_ref,
                           qb_ref, kbt_ref, kbtb_ref, vbt_ref, vbb_ref, gates_ref):
    tm = x_ref.shape[0]
    nb = _rms_rows(x_ref[...], g_ref[...]).astype(BF16)

    def seg(lo, width):
        return jnp.dot(nb, ws_ref[:, lo:lo + width], preferred_element_type=F32)

    def seg_t(lo, width):
        return lax.dot_general(wt_ref[lo:lo + width, :], nb, _NT, preferred_element_type=F32)

    qa = _head_norm(seg(0, WIDTH_A), qg_ref[...], gm_ref[...]) * (HEAD_DIM_A ** -0.5 * LOG2E)
    qa_ref[...] = qa.astype(BF16)
    va = seg(WIDTH_A, WIDTH_A)
    vab_ref[...] = va.astype(BF16)
    for h in range(N_HEADS_A):
        vai_ref[pl.ds(h, tm, stride=N_HEADS_A), :] = va[:, h * LANES:(h + 1) * LANES]
    off = 2 * WIDTH_A
    qb_ref[...] = (seg(off, WIDTH_B) * (HEAD_DIM_B ** -0.5)).astype(BF16)
    vbb_ref[...] = seg(off + WIDTH_B, WIDTH_B).astype(BF16)
    gates_ref[...] = seg(off + 2 * WIDTH_B, gates_ref.shape[1])

    kat = seg_t(0, WIDTH_A).reshape(WIDTH_A // HEAD_DIM_A, HEAD_DIM_A, tm)
    ms = jnp.mean(kat * kat, axis=1, keepdims=True)
    kat = (kat * lax.rsqrt(ms + NORM_EPS) * kgc_ref[...].reshape(1, HEAD_DIM_A, 1)).reshape(WIDTH_A, tm)
    kat_ref[0] = kat
    katb_ref[0] = kat.astype(BF16)
    kbt = seg_t(WIDTH_A, WIDTH_B)
    kbt_ref[0] = kbt
    kbtb_ref[0] = kbt.astype(BF16)
    vbt_ref[0] = seg_t(WIDTH_A + WIDTH_B, WIDTH_B)


def _in_proj_prompt(x, g, ws, wt, qg_t, kg_col, gm, batch, seq):
    n, d = x.shape
    tm = min(TOKEN_TILE, seq)
    tps = seq // tm
    gate_w = ws.shape[1] - 2 * WIDTH_A - 2 * WIDTH_B
    row = lambda w: pl.BlockSpec((tm, w), lambda i: (i, 0))
    t_out = lambda w: pl.BlockSpec((1, w, tm), lambda i: (i // tps, 0, i % tps))
    da_tile, sb_tile = min(DA_TILE, seq), min(SB_TILE, seq)
    t_tile = lambda w, kt: pl.BlockSpec((1, w, tm), lambda i: (i // (kt // tm), 0, i % (kt // tm)))
    out_specs = [row(WIDTH_A), t_out(WIDTH_A), t_tile(WIDTH_A, da_tile),
                 pl.BlockSpec((tm * N_HEADS_A, LANES), lambda i: (i, 0)), row(WIDTH_A),
                 row(WIDTH_B), t_out(WIDTH_B), t_tile(WIDTH_B, sb_tile), t_out(WIDTH_B), row(WIDTH_B), row(gate_w)]
    out_shape = [jax.ShapeDtypeStruct((n, WIDTH_A), BF16),
                 jax.ShapeDtypeStruct((batch, WIDTH_A, seq), F32),
                 jax.ShapeDtypeStruct((n // da_tile, WIDTH_A, da_tile), BF16),
                 jax.ShapeDtypeStruct((n * N_HEADS_A, LANES), F32),
                 jax.ShapeDtypeStruct((n, WIDTH_A), BF16),
                 jax.ShapeDtypeStruct((n, WIDTH_B), BF16),
                 jax.ShapeDtypeStruct((batch, WIDTH_B, seq), F32),
                 jax.ShapeDtypeStruct((n // sb_tile, WIDTH_B, sb_tile), BF16),
                 jax.ShapeDtypeStruct((batch, WIDTH_B, seq), F32),
                 jax.ShapeDtypeStruct((n, WIDTH_B), BF16),
                 jax.ShapeDtypeStruct((n, gate_w), F32)]
    return pl.pallas_call(
        _in_proj_prompt_kernel,
        grid=(n // tm,),
        in_specs=[row(d), _const_spec(g.shape), _const_spec(ws.shape), _const_spec(wt.shape),
                  _const_spec(qg_t.shape), _const_spec(kg_col.shape), _const_spec(gm.shape)],
        out_specs=out_specs,
        out_shape=out_shape,
        compiler_params=_cparams(("arbitrary",)),
        name="in_proj_prompt",
    )(x, g, ws, wt, qg_t, kg_col, gm)


def _in_proj_sample_kernel(x_ref, g_ref, ws_ref, wk_ref, qg_ref, kg_ref, gm_ref,
                           qa_ref, ka_ref, va_ref, qb_ref, kb_ref, vb_ref, gates_ref):
    nb = _rms_rows(x_ref[...], g_ref[...]).astype(BF16)

    def seg(w_ref, lo, width):
        return jnp.dot(nb, w_ref[:, lo:lo + width], preferred_element_type=F32)

    qa_ref[...] = _head_norm(seg(ws_ref, 0, WIDTH_A), qg_ref[...], gm_ref[...]) * (HEAD_DIM_A ** -0.5 * LOG2E)
    va_ref[...] = seg(ws_ref, WIDTH_A, WIDTH_A)
    off = 2 * WIDTH_A
    qb_ref[...] = seg(ws_ref, off, WIDTH_B) * (HEAD_DIM_B ** -0.5)
    vb_ref[...] = seg(ws_ref, off + WIDTH_B, WIDTH_B)
    gates_ref[...] = seg(ws_ref, off + 2 * WIDTH_B, gates_ref.shape[1])
    ka_ref[...] = _head_norm(seg(wk_ref, 0, WIDTH_A), kg_ref[...], gm_ref[...])
    kb_ref[...] = seg(wk_ref, WIDTH_A, WIDTH_B)


def _in_proj_sample(x, g, ws, wk, qg_t, kg_t, gm):
    n, d = x.shape
    tm = min(TOKEN_TILE, n)
    gate_w = ws.shape[1] - 2 * WIDTH_A - 2 * WIDTH_B
    row = lambda w: pl.BlockSpec((tm, w), lambda i: (i, 0))
    widths = [WIDTH_A, WIDTH_A, WIDTH_A, WIDTH_B, WIDTH_B, WIDTH_B, gate_w]
    return pl.pallas_call(
        _in_proj_sample_kernel,
        grid=(n // tm,),
        in_specs=[row(d), _const_spec(g.shape), _const_spec(ws.shape), _const_spec(wk.shape),
                  _const_spec(qg_t.shape), _const_spec(kg_t.shape), _const_spec(gm.shape)],
        out_specs=[row(w) for w in widths],
        out_shape=[jax.ShapeDtypeStruct((n, w), F32) for w in widths],
        compiler_params=_cparams(("arbitrary",)),
        name="in_proj_sample",
    )(x, g, ws, wk, qg_t, kg_t, gm)


def _lambda_value(wl, lam_init):
    a = jnp.sum(wl[0:1] * wl[1:2], axis=1, keepdims=True)
    b = jnp.sum(wl[2:3] * wl[3:4], axis=1, keepdims=True)
    return jnp.exp(a) - jnp.exp(b) + lam_init


def _sub_norm(d, gain, lam_init):
    y = d * lax.rsqrt(jnp.mean(d * d, axis=-1, keepdims=True) + NORM_EPS)
    return y * gain * (1.0 - lam_init)


def _softmax_probs(s, m_sc, l_sc):
    m_prev = m_sc[...]
    m_new = jnp.maximum(m_prev, jnp.max(s, axis=-1, keepdims=True))
    alpha = jnp.exp2(m_prev - m_new)
    p = jnp.exp2(s - m_new)
    l_sc[...] = alpha * l_sc[...] + jnp.sum(p, axis=-1, keepdims=True)
    m_sc[...] = m_new
    return alpha, p.astype(BF16)


def _lane_tile(x, width):
    reps = width // x.shape[1]
    return x if reps == 1 else jnp.concatenate([x] * reps, axis=1)


def _stick_weights(z, u_hi_lo, carry_sc, valid):
    width = z.shape[1]
    u = jnp.log(1.0 + jnp.exp(-jnp.abs(z)))
    ls = -(jnp.maximum(z, 0.0) + u)
    if valid is not None:
        ls = jnp.where(valid, ls, 0.0)
    hi = ls.astype(BF16)
    lo = (ls - hi.astype(F32)).astype(BF16)
    later = jnp.dot(jnp.concatenate([hi, lo], axis=1), u_hi_lo, preferred_element_type=F32)
    carry = carry_sc[...]
    a = jnp.exp(jnp.minimum(z, 0.0) - u + later + _lane_tile(carry, width))
    if valid is not None:
        a = jnp.where(valid, a, 0.0)
    carry_sc[...] = carry + jnp.sum(ls, axis=-1, keepdims=True)
    return a.astype(BF16)


def _split_lanes(q):
    lane = lax.broadcasted_iota(jnp.int32, q.shape, 1)
    zero = jnp.zeros_like(q)
    half = LANES // 2
    return jnp.concatenate([jnp.where(lane < half, q, zero), jnp.where(lane >= half, q, zero)], axis=0)


def _da_prompt_kernel(q_ref, kt_ref, v_ref, bias_ref, wl_ref, sg_ref, o_ref, m_sc, acc_sc,
                      *, tq, lam_init):
    qi = pl.program_id(2)
    qq = _split_lanes(q_ref[...])
    m_sc[...] = jnp.full(m_sc.shape, NEG_BIG, F32)
    acc_sc[...] = jnp.zeros(acc_sc.shape, F32)
    ones = jnp.ones((tq, LANES), BF16)

    def block(ki, bias):
        v_ext = jnp.concatenate([v_ref[pl.ds(pl.multiple_of(ki * tq, tq), tq), :], ones], axis=1)
        s = jnp.dot(qq, kt_ref[ki], preferred_element_type=F32)
        if bias is not None:
            s = (s.reshape(2, tq, tq) + bias[None]).reshape(2 * tq, tq)
        m_prev = m_sc[...]
        m_new = jnp.maximum(m_prev, jnp.max(s, axis=-1, keepdims=True))
        alpha = jnp.exp2(m_prev - m_new)
        p = jnp.concatenate([jnp.exp2(s[:, j * LANES:(j + 1) * LANES] - m_new).astype(BF16)
                             for j in range(tq // LANES)], axis=1)
        acc_sc[...] = _lane_tile(alpha, 2 * LANES) * acc_sc[...] + jnp.dot(p, v_ext, preferred_element_type=F32)
        m_sc[...] = m_new

    def far(ki, c):
        block(ki, None)
        return c

    lax.fori_loop(0, qi - 1, far, 0)

    @pl.when(qi >= 1)
    def _():
        block(qi - 1, bias_ref[0, 0])

    block(qi, bias_ref[0, 1])

    acc = acc_sc[...]
    o = acc[:, :LANES] / acc[:, LANES:]
    lam = _lambda_value(wl_ref[...], lam_init)
    d = o[:tq] - lam * o[tq:]
    o_ref[...] = _sub_norm(d, sg_ref[...], lam_init).astype(o_ref.dtype)


def _da_prompt(q, kt, v, bias, wl, sg, batch, seq, lam_init):
    n = q.shape[0]
    tq = kt.shape[2]
    nq = seq // tq
    kern = functools.partial(_da_prompt_kernel, tq=tq, lam_init=lam_init)
    return pl.pallas_call(
        kern,
        grid=(batch, N_HEADS_A, nq),
        in_specs=[pl.BlockSpec((tq, LANES), lambda b, h, i: (b * nq + i, h)),
                  pl.BlockSpec((nq, LANES, tq), lambda b, h, i: (b, h, 0)),
                  pl.BlockSpec((seq, LANES), lambda b, h, i: (b, h)),
                  pl.BlockSpec((1, 2, tq, tq), lambda b, h, i: (h, 0, 0, 0)),
                  _const_spec(wl.shape), _const_spec(sg.shape)],
        out_specs=pl.BlockSpec((tq, LANES), lambda b, h, i: (b * nq + i, h)),
        out_shape=jax.ShapeDtypeStruct((n, WIDTH_A), BF16),
        scratch_shapes=[pltpu.VMEM((2 * tq, LANES), F32), pltpu.VMEM((2 * tq, 2 * LANES), F32)],
        compiler_params=_cparams(("arbitrary", "arbitrary", "arbitrary")),
        name="da_prompt",
    )(q, kt, v, bias, wl, sg)


def _sb_prompt_kernel(q_ref, kt_ref, v_ref, u_ref, o_ref, carry_sc, acc_sc, *, tq):
    qi = pl.program_id(2)
    qq = _split_lanes(q_ref[...])
    carry_sc[...] = jnp.zeros(carry_sc.shape, F32)
    acc_sc[...] = jnp.zeros(acc_sc.shape, F32)

    def block(ki, valid):
        vb = v_ref[pl.ds(pl.multiple_of(ki * tq, tq), tq), :]
        z = jnp.dot(qq, kt_ref[ki], preferred_element_type=F32)
        a = _stick_weights(z, u_ref[...], carry_sc, valid)
        acc_sc[...] = acc_sc[...] + jnp.dot(a, vb, preferred_element_type=F32)
        return jnp.max(carry_sc[...])

    row = lax.broadcasted_iota(jnp.int32, (2, tq, tq), 1).reshape(2 * tq, tq)
    col = lax.broadcasted_iota(jnp.int32, (2 * tq, tq), 1)
    top = block(qi, col < row)

    def more(c):
        return jnp.logical_and(c[0] < qi, c[1] > STICK_UNDERFLOW)

    def left(c):
        return c[0] + 1, block(qi - 1 - c[0], None)

    lax.while_loop(more, left, (jnp.int32(0), top))

    acc = acc_sc[...]
    lane = lax.broadcasted_iota(jnp.int32, (tq, LANES), 1)
    o_ref[...] = jnp.where(lane < LANES // 2, acc[:tq], acc[tq:]).astype(o_ref.dtype)


def _sb_prompt(q, kt, v, u_hi_lo, batch, seq):
    n = q.shape[0]
    tq = kt.shape[2]
    nq = seq // tq
    pairs = WIDTH_B // LANES
    kern = functools.partial(_sb_prompt_kernel, tq=tq)
    return pl.pallas_call(
        kern,
        grid=(batch, pairs, nq),
        in_specs=[pl.BlockSpec((tq, LANES), lambda b, h, i: (b * nq + i, h)),
                  pl.BlockSpec((nq, LANES, tq), lambda b, h, i: (b, h, 0)),
                  pl.BlockSpec((seq, LANES), lambda b, h, i: (b, h)),
                  _const_spec(u_hi_lo.shape)],
        out_specs=pl.BlockSpec((tq, LANES), lambda b, h, i: (b * nq + i, h)),
        out_shape=jax.ShapeDtypeStruct((n, WIDTH_B), BF16),
        scratch_shapes=[pltpu.VMEM((2 * tq, LANES), F32), pltpu.VMEM((2 * tq, LANES), F32)],
        compiler_params=_cparams(("arbitrary", "arbitrary", "arbitrary")),
        name="sb_prompt",
    )(q, kt, v, u_hi_lo)


def _block_diag_queries(q, groups, width):
    t = q.shape[0]
    rep = jnp.concatenate([q] * groups, axis=0)
    r = lax.broadcasted_iota(jnp.int32, rep.shape, 0)
    c = lax.broadcasted_iota(jnp.int32, rep.shape, 1)
    keep = (c // width) == (r // t)
    return jnp.where(keep, rep, 0.0).astype(BF16)


def _pad_rows(x, rows):
    return jnp.concatenate([x, jnp.zeros((rows - x.shape[0], x.shape[1]), x.dtype)], axis=0)


def _da_sample_kernel(pt_ref, q_ref, knew_ref, vnew_ref, blast_ref, bnew_ref, wl_ref, sg_ref, *rest,
                      pages, page, lam_init):
    del pt_ref
    kt_refs, v_refs = rest[:pages], rest[pages:2 * pages]
    o_ref, qbd_sc, m_sc, l_sc, acc_sc = rest[2 * pages:]
    g = pl.program_id(1)
    last = pl.num_programs(1) - 1
    t = q_ref.shape[0]
    rows_h = 2 * t

    @pl.when(g == 0)
    def _():
        qbd_sc[...] = _block_diag_queries(q_ref[...], 2 * N_HEADS_A, HEAD_DIM_A)
        m_sc[...] = jnp.full(m_sc.shape, NEG_BIG, F32)
        l_sc[...] = jnp.zeros(l_sc.shape, F32)
        acc_sc[...] = jnp.zeros(acc_sc.shape, F32)

    qbd = qbd_sc[...]
    ktcat = jnp.concatenate([r[0, 0].astype(BF16) for r in kt_refs], axis=1)
    s = jnp.dot(qbd, ktcat, preferred_element_type=F32)
    s = s + blast_ref[...] * (g == last).astype(F32)
    alpha, p = _softmax_probs(s, m_sc, l_sc)
    for h in range(N_HEADS_A):
        vh = jnp.concatenate([r[0, 0, pl.ds(h, page, stride=N_HEADS_A), :] for r in v_refs], axis=0)
        rs = slice(h * rows_h, (h + 1) * rows_h)
        acc_sc[rs, :] = alpha[rs] * acc_sc[rs, :] + jnp.dot(p[rs], vh.astype(BF16), preferred_element_type=F32)

    @pl.when(g == last)
    def _():
        kn = _pad_rows(knew_ref[...], page).astype(BF16)
        vn = _pad_rows(vnew_ref[...], page).astype(BF16)
        s2 = lax.dot_general(qbd, kn, _NT, preferred_element_type=F32) + bnew_ref[...]
        alpha2, p2 = _softmax_probs(s2, m_sc, l_sc)
        o_all = jnp.dot(p2, vn, preferred_element_type=F32)
        inv_l = 1.0 / l_sc[...]
        lam = _lambda_value(wl_ref[...], lam_init)
        outs = []
        for h in range(N_HEADS_A):
            rs = slice(h * rows_h, (h + 1) * rows_h)
            cols = slice(h * 2 * HEAD_DIM_A, (h + 1) * 2 * HEAD_DIM_A)
            o = (alpha2[rs] * acc_sc[rs, :] + o_all[rs, cols]) * inv_l[rs]
            outs.append(_sub_norm(o[:t] - lam * o[t:], sg_ref[...], lam_init))
        o_ref[...] = jnp.concatenate(outs, axis=1)


def _page_specs(layer, n_pages, pages, block, reverse):
    specs = []
    for j in range(pages):
        if reverse:
            idx = lambda b, g, pt, j=j: (layer, pt[b * n_pages + n_pages - (g + 1) * pages + j], 0, 0)
        else:
            idx = lambda b, g, pt, j=j: (layer, pt[b * n_pages + g * pages + j], 0, 0)
        specs.append(pl.BlockSpec((1, 1) + block, idx))
    return specs


def _da_sample(pt, q, knew, vnew, blast, bnew, wl, sg, cache_kt, cache_vi, layer, dec_b, dec_t, lam_init):
    n_pages = pt.shape[0] // dec_b
    page = cache_kt.shape[3]
    pages = min(PAGES_PER_STEP, n_pages)
    rows = 2 * N_HEADS_A * dec_t
    tok = lambda w: pl.BlockSpec((dec_t, w), lambda b, g, pt: (b, 0))
    const = lambda a: pl.BlockSpec(a.shape, lambda b, g, pt: (0,) * a.ndim)
    kern = functools.partial(_da_sample_kernel, pages=pages, page=page, lam_init=lam_init)
    grid_spec = pltpu.PrefetchScalarGridSpec(
        num_scalar_prefetch=1,
        grid=(dec_b, n_pages // pages),
        in_specs=[tok(WIDTH_A), tok(WIDTH_A), tok(WIDTH_A), const(blast), const(bnew), const(wl), const(sg)]
        + _page_specs(layer, n_pages, pages, (WIDTH_A, page), False)
        + _page_specs(layer, n_pages, pages, (page * N_HEADS_A, 2 * HEAD_DIM_A), False),
        out_specs=tok(WIDTH_A),
        scratch_shapes=[pltpu.VMEM((rows, WIDTH_A), BF16), pltpu.VMEM((rows, 1), F32),
                        pltpu.VMEM((rows, 1), F32), pltpu.VMEM((rows, 2 * HEAD_DIM_A), F32)],
    )
    return pl.pallas_call(
        kern,
        grid_spec=grid_spec,
        out_shape=jax.ShapeDtypeStruct((dec_b * dec_t, WIDTH_A), F32),
        compiler_params=_cparams(("arbitrary", "arbitrary")),
        name="da_sample",
    )(pt, q, knew, vnew, blast, bnew, wl, sg, *([cache_kt] * pages), *([cache_vi] * pages))


def _sb_sample_kernel(pt_ref, q_ref, knew_ref, vnew_ref, u_ref, *rest, pages, page):
    del pt_ref
    kt_refs, vt_refs = rest[:pages], rest[pages:2 * pages]
    o_ref, qbd_sc, carry_sc, acc_sc = rest[2 * pages:]
    g = pl.program_id(1)
    last = pl.num_programs(1) - 1
    t = q_ref.shape[0]
    blk = u_ref.shape[1]

    @pl.when(g == 0)
    def _():
        qbd = _block_diag_queries(q_ref[...], N_HEADS_B, HEAD_DIM_B)
        qbd_sc[...] = qbd
        carry_sc[...] = jnp.zeros(carry_sc.shape, F32)
        kn = _pad_rows(knew_ref[...], page).astype(BF16)
        vn = _pad_rows(vnew_ref[...], page).astype(BF16)
        z = lax.dot_general(qbd, kn, _NT, preferred_element_type=F32)
        row_t = lax.broadcasted_iota(jnp.int32, z.shape, 0) % t
        col = lax.broadcasted_iota(jnp.int32, z.shape, 1)
        u_page = jnp.concatenate([u_ref[0:page, 0:page], u_ref[0:page, 0:page]], axis=0)
        a = _stick_weights(z, u_page, carry_sc, col < row_t)
        acc_sc[...] = jnp.dot(a, vn, preferred_element_type=F32)

    qbd = qbd_sc[...]
    ktcat = jnp.concatenate([r[0, 0].astype(BF16) for r in kt_refs], axis=1)
    vtcat = jnp.concatenate([r[0, 0].astype(BF16) for r in vt_refs], axis=1)
    z = jnp.dot(qbd, ktcat, preferred_element_type=F32)
    for b in reversed(range(pages * page // blk)):
        sl = slice(b * blk, (b + 1) * blk)
        a = _stick_weights(z[:, sl], u_ref[...], carry_sc, None)
        acc_sc[...] = acc_sc[...] + lax.dot_general(a, vtcat[:, sl], _NT, preferred_element_type=F32)

    @pl.when(g == last)
    def _():
        acc = acc_sc[...]
        col_h = lax.broadcasted_iota(jnp.int32, (t, WIDTH_B), 1) // HEAD_DIM_B
        out = jnp.zeros((t, WIDTH_B), F32)
        for h in range(N_HEADS_B):
            out = out + jnp.where(col_h == h, acc[h * t:(h + 1) * t, :], 0.0)
        o_ref[...] = out


def _sb_sample(pt, q, knew, vnew, u_hi_lo, cache_kt, cache_vt, layer, dec_b, dec_t):
    n_pages = pt.shape[0] // dec_b
    page = cache_kt.shape[3]
    pages = min(PAGES_PER_STEP, n_pages)
    rows = N_HEADS_B * dec_t
    tok = lambda w: pl.BlockSpec((dec_t, w), lambda b, g, pt: (b, 0))
    const = lambda a: pl.BlockSpec(a.shape, lambda b, g, pt: (0,) * a.ndim)
    kern = functools.partial(_sb_sample_kernel, pages=pages, page=page)
    grid_spec = pltpu.PrefetchScalarGridSpec(
        num_scalar_prefetch=1,
        grid=(dec_b, n_pages // pages),
        in_specs=[tok(WIDTH_B), tok(WIDTH_B), tok(WIDTH_B), const(u_hi_lo)]
        + _page_specs(layer, n_pages, pages, (WIDTH_B, page), True)
        + _page_specs(layer, n_pages, pages, (WIDTH_B, page), True),
        out_specs=tok(WIDTH_B),
        scratch_shapes=[pltpu.VMEM((rows, WIDTH_B), BF16), pltpu.VMEM((rows, LANES), F32),
                        pltpu.VMEM((rows, WIDTH_B), F32)],
    )
    return pl.pallas_call(
        kern,
        grid_spec=grid_spec,
        out_shape=jax.ShapeDtypeStruct((dec_b * dec_t, WIDTH_B), F32),
        compiler_params=_cparams(("arbitrary", "arbitrary")),
        name="sb_sample",
    )(pt, q, knew, vnew, u_hi_lo, *([cache_kt] * pages), *([cache_vt] * pages))


def _merge_out_kernel(x_ref, oa_ref, ob_ref, gates_ref, wpa_ref, wpb_ref, wo_ref, o_ref):
    d = x_ref.shape[1]
    ga = jax.nn.sigmoid(gates_ref[:, :d])
    gb = jax.nn.sigmoid(gates_ref[:, d:])
    pa = jnp.dot(oa_ref[...].astype(BF16), wpa_ref[...], preferred_element_type=F32)
    pb = jnp.dot(ob_ref[...].astype(BF16), wpb_ref[...], preferred_element_type=F32)
    merged = ga * pa + gb * pb
    o_ref[...] = x_ref[...] + jnp.dot(merged.astype(BF16), wo_ref[...], preferred_element_type=F32)


def _merge_out(x, oa, ob, gates, wpa, wpb, wo):
    n, d = x.shape
    tm = min(TOKEN_TILE, n)
    row = lambda w: pl.BlockSpec((tm, w), lambda i: (i, 0))
    return pl.pallas_call(
        _merge_out_kernel,
        grid=(n // tm,),
        in_specs=[row(d), row(WIDTH_A), row(WIDTH_B), row(2 * d),
                  _const_spec(wpa.shape), _const_spec(wpb.shape), _const_spec(wo.shape)],
        out_specs=row(d),
        out_shape=jax.ShapeDtypeStruct((n, d), F32),
        compiler_params=_cparams(("arbitrary",)),
        name="merge_out",
    )(x, oa, ob, gates, wpa, wpb, wo)


def _gelu_exact(x):
    return 0.5 * x * (1.0 + lax.erf(x * np.float32(math.sqrt(0.5))))


def _ffn_up(x_ref, g_ref, wup_ref, ff):
    x = x_ref[...]
    n2 = _rms_rows(x, g_ref[...]).astype(BF16)
    a = jnp.dot(n2, wup_ref[:, :ff], preferred_element_type=F32)
    b = jnp.dot(n2, wup_ref[:, ff:], preferred_element_type=F32)
    return x, a, b


def _ffn_down(x, a_m1, a_m2, a, b, cw_ref, cb_ref, wdn_ref):
    conv = cb_ref[...] + a_m2 * cw_ref[0:1, :] + a_m1 * cw_ref[1:2, :] + a * cw_ref[2:3, :]
    h = (_gelu_exact(conv) * b).astype(BF16)
    return x + jnp.dot(h, wdn_ref[...], preferred_element_type=F32)


def _ffn_prompt_kernel(x_ref, g_ref, wup_ref, cw_ref, cb_ref, wdn_ref, y_ref, tail_ref, carry_sc,
                       *, tiles_per_seq):
    ff = wdn_ref.shape[0]
    tm = x_ref.shape[0]
    x, a, b = _ffn_up(x_ref, g_ref, wup_ref, ff)
    first = (pl.program_id(0) % tiles_per_seq) == 0
    prev = jnp.where(first, 0.0, carry_sc[...])
    p6 = prev[SUBLANES - 2:SUBLANES - 1, :]
    p7 = prev[SUBLANES - 1:SUBLANES, :]
    row = lax.broadcasted_iota(jnp.int32, a.shape, 0)
    a_m1 = jnp.where(row == 0, p7, pltpu.roll(a, 1, 0))
    a_m2 = jnp.where(row == 0, p6, jnp.where(row == 1, p7, pltpu.roll(a, 2, 0)))
    last_rows = a[tm - SUBLANES:, :]
    carry_sc[...] = last_rows
    tail_ref[0] = last_rows
    y_ref[...] = _ffn_down(x, a_m1, a_m2, a, b, cw_ref, cb_ref, wdn_ref)


def _ffn_sample_kernel(x_ref, g_ref, wup_ref, cw_ref, cb_ref, wdn_ref, s1_ref, s2_ref, y_ref, a_ref,
                       *, dec_t):
    ff = wdn_ref.shape[0]
    x, a, b = _ffn_up(x_ref, g_ref, wup_ref, ff)
    t = lax.broadcasted_iota(jnp.int32, a.shape, 0) % dec_t
    a_m1 = jnp.where(t >= 1, pltpu.roll(a, 1, 0), s1_ref[...])
    a_m2 = jnp.where(t >= 2, pltpu.roll(a, 2, 0), s2_ref[...])
    a_ref[...] = a
    y_ref[...] = _ffn_down(x, a_m1, a_m2, a, b, cw_ref, cb_ref, wdn_ref)


def _ffn_prompt(x, g, wup, cw, cb, wdn, batch, seq):
    n, d = x.shape
    ff = wdn.shape[0]
    tm = min(TOKEN_TILE, seq)
    tiles_per_seq = seq // tm
    row = lambda w: pl.BlockSpec((tm, w), lambda i: (i, 0))
    kern = functools.partial(_ffn_prompt_kernel, tiles_per_seq=tiles_per_seq)
    return pl.pallas_call(
        kern,
        grid=(n // tm,),
        in_specs=[row(d), _const_spec(g.shape), _const_spec(wup.shape), _const_spec(cw.shape),
                  _const_spec(cb.shape), _const_spec(wdn.shape)],
        out_specs=[row(d), pl.BlockSpec((1, SUBLANES, ff), lambda i: (i // tiles_per_seq, 0, 0))],
        out_shape=[jax.ShapeDtypeStruct((n, d), F32), jax.ShapeDtypeStruct((batch, SUBLANES, ff), F32)],
        scratch_shapes=[pltpu.VMEM((SUBLANES, ff), F32)],
        compiler_params=_cparams(("arbitrary",)),
        name="ffn_prompt",
    )(x, g, wup, cw, cb, wdn)


def _ffn_sample(x, g, wup, cw, cb, wdn, s1, s2, dec_t):
    n, d = x.shape
    ff = wdn.shape[0]
    tm = min(TOKEN_TILE, n)
    row = lambda w: pl.BlockSpec((tm, w), lambda i: (i, 0))
    kern = functools.partial(_ffn_sample_kernel, dec_t=dec_t)
    return pl.pallas_call(
        kern,
        grid=(n // tm,),
        in_specs=[row(d), _const_spec(g.shape), _const_spec(wup.shape), _const_spec(cw.shape),
                  _const_spec(cb.shape), _const_spec(wdn.shape), row(ff), row(ff)],
        out_specs=[row(d), row(ff)],
        out_shape=[jax.ShapeDtypeStruct((n, d), F32), jax.ShapeDtypeStruct((n, ff), F32)],
        compiler_params=_cparams(("arbitrary",)),
        name="ffn_sample",
    )(x, g, wup, cw, cb, wdn, s1, s2)


def _bucket_table(max_dist):
    n = np.arange(max_dist + 1, dtype=np.int32)
    max_exact = N_BUCKETS // 2
    nf = np.maximum(n, 1).astype(np.float32)
    large = max_exact + (np.log(nf / np.float32(max_exact)) / np.float32(math.log(MAX_DISTANCE / max_exact))
                         * np.float32(N_BUCKETS - max_exact)).astype(np.int32)
    large = np.minimum(large, N_BUCKETS - 1)
    return np.where(n < max_exact, n, large).astype(np.int32)


def _bias_kernel(rs_ref, prompt_ref, last_ref, new_ref, *, upper, tq, dec_t, page):
    h = pl.program_id(0)

    def lookup(dist):
        val = jnp.zeros(dist.shape, F32) + rs_ref[N_BUCKETS - 1, h]
        for b in range(N_BUCKETS - 2, -1, -1):
            val = jnp.where(dist <= upper[b], rs_ref[b, h], val)
        return jnp.where(dist >= 0, val, NEG_BIG)

    ql = lax.broadcasted_iota(jnp.int32, (tq, tq), 0)
    kl = lax.broadcasted_iota(jnp.int32, (tq, tq), 1)
    prompt_ref[0, 0] = lookup(ql - kl + tq)
    prompt_ref[0, 1] = lookup(ql - kl)
    t = lax.broadcasted_iota(jnp.int32, (2 * dec_t, page), 0) % dec_t
    j = lax.broadcasted_iota(jnp.int32, (2 * dec_t, page), 1)
    last_ref[...] = lookup(page + t - j)
    new_ref[...] = lookup(t - j)


def _bias_tiles(rel_shift, tq, dec_t, page):
    table = _bucket_table(2 * MAX_DISTANCE)
    upper = tuple(int(np.nonzero(table <= b)[0].max()) for b in range(N_BUCKETS - 1))
    heads = rel_shift.shape[1]
    kern = functools.partial(_bias_kernel, upper=upper, tq=tq, dec_t=dec_t, page=page)
    return pl.pallas_call(
        kern,
        grid=(heads,),
        in_specs=[pl.BlockSpec(memory_space=pltpu.SMEM)],
        out_specs=[pl.BlockSpec((1, 2, tq, tq), lambda h: (h, 0, 0, 0)),
                   pl.BlockSpec((2 * dec_t, page), lambda h: (h, 0)),
                   pl.BlockSpec((2 * dec_t, page), lambda h: (h, 0))],
        out_shape=[jax.ShapeDtypeStruct((heads, 2, tq, tq), F32),
                   jax.ShapeDtypeStruct((heads * 2 * dec_t, page), F32),
                   jax.ShapeDtypeStruct((heads * 2 * dec_t, page), F32)],
        compiler_params=_cparams(("arbitrary",)),
        name="bias_tiles",
    )(rel_shift)


def _lower_triangles(w):
    j = np.arange(w)[:, None]
    s = np.arange(w)[None, :]
    u = (j > s).astype(np.float32)
    return jnp.asarray(np.concatenate([u, u], axis=0), dtype=BF16)


def _group_mean_matrix():
    i = np.arange(MXU_EDGE)
    m = (i[:, None] // HEAD_DIM_A == i[None, :] // HEAD_DIM_A).astype(np.float32) / HEAD_DIM_A
    return jnp.asarray(m, dtype=BF16)


def kernel(x_prompt, x_sample, cache_da_k, cache_da_v, cache_sb_k, cache_sb_v, state_conv, page_table,
           rel_bias, attn_norm_g, w_in, q_norm_g, k_norm_g, w_lambda, subln_g, w_proj_a, w_proj_b, w_out,
           ffn_norm_g, w_up, conv_w, conv_b, w_down):
    batch, seq, d = x_prompt.shape
    dec_b, dec_t, _ = x_sample.shape
    depth = w_in.shape[0]
    n_pool, page = cache_da_k.shape[1], cache_da_k.shape[2]
    n_pages = page_table.shape[1]
    past_len = n_pages * page
    ff = w_down.shape[1]

    pt = page_table.reshape(-1).astype(jnp.int32)
    cda_kt = jnp.swapaxes(cache_da_k.reshape(depth, n_pool, page, WIDTH_A), 2, 3)
    cda_vi = cache_da_v.reshape(depth, n_pool, page * N_HEADS_A, 2 * HEAD_DIM_A)
    csb_kt = jnp.swapaxes(cache_sb_k.reshape(depth, n_pool, page, WIDTH_B), 2, 3)
    csb_vt = jnp.swapaxes(cache_sb_v.reshape(depth, n_pool, page, WIDTH_B), 2, 3)

    rel_shift = ((rel_bias - rel_bias[N_BUCKETS - 1]) * LOG2E).astype(F32)
    pages = min(PAGES_PER_STEP, n_pages)
    bias_prompt, blast, bnew = _bias_tiles(rel_shift, min(DA_TILE, seq), dec_t, page)
    blast = jnp.pad(blast, ((0, 0), ((pages - 1) * page, 0)))

    u_hi_lo = _lower_triangles(min(SB_TILE, seq))
    gm = _group_mean_matrix()

    c_qa, c_ka, c_va = 0, WIDTH_A, 2 * WIDTH_A
    c_qb, c_kb, c_vb, c_g = 3 * WIDTH_A, 3 * WIDTH_A + WIDTH_B, 3 * WIDTH_A + 2 * WIDTH_B, 3 * WIDTH_A + 3 * WIDTH_B

    xp = x_prompt.reshape(batch * seq, d)
    xs = x_sample.reshape(dec_b * dec_t, d)
    rows_p = [[] for _ in range(5)]
    rows_s = [[] for _ in range(5)]
    for l in range(depth):
        lam_init = LAMBDA_BASE - LAMBDA_SCALE * math.exp(-LAMBDA_RATE * l)
        g_attn = attn_norm_g[l].reshape(1, d)
        wl_in = w_in[l].astype(BF16)
        w_std = jnp.concatenate([wl_in[:, c_qa:c_ka], wl_in[:, c_va:c_qb], wl_in[:, c_qb:c_kb],
                                 wl_in[:, c_vb:c_g], wl_in[:, c_g:]], axis=1)
        w_keys = jnp.concatenate([wl_in[:, c_ka:c_va], wl_in[:, c_kb:c_vb]], axis=1)
        w_t = jnp.concatenate([w_keys, wl_in[:, c_vb:c_g]], axis=1).T
        qg_t = jnp.tile(q_norm_g[l], WIDTH_A // HEAD_DIM_A).reshape(1, WIDTH_A)
        kg_t = jnp.tile(k_norm_g[l], WIDTH_A // HEAD_DIM_A).reshape(1, WIDTH_A)
        kg_col = k_norm_g[l].reshape(HEAD_DIM_A, 1)
        wl = w_lambda[l]
        sg = subln_g[l].reshape(1, 2 * HEAD_DIM_A)
        wpa, wpb, wo = w_proj_a[l].astype(BF16), w_proj_b[l].astype(BF16), w_out[l].astype(BF16)
        g_ffn = ffn_norm_g[l].reshape(1, d)
        wup, wdn = w_up[l].astype(BF16), w_down[l].astype(BF16)
        cw, cb = conv_w[l], conv_b[l].reshape(1, ff)

        (qa, kat, katb, vai, vab, qb, kbt, kbtb, vbt, vbb, gates) = _in_proj_prompt(
            xp, g_attn, w_std, w_t, qg_t, kg_col, gm, batch, seq)
        oa = _da_prompt(qa, katb, vab, bias_prompt, wl, sg, batch, seq, lam_init)
        ob = _sb_prompt(qb, kbtb, vbb, u_hi_lo, batch, seq)
        xp = _merge_out(xp, oa, ob, gates, wpa, wpb, wo)
        xp, tail = _ffn_prompt(xp, g_ffn, wup, cw, cb, wdn, batch, seq)
        new_p = (jnp.transpose(kat.reshape(batch, N_HEADS_A, 2, HEAD_DIM_A, seq), (0, 4, 1, 2, 3)),
                 vai.reshape(batch, seq, N_HEADS_A, 2 * HEAD_DIM_A),
                 jnp.transpose(kbt.reshape(batch, N_HEADS_B, HEAD_DIM_B, seq), (0, 3, 1, 2)),
                 jnp.transpose(vbt.reshape(batch, N_HEADS_B, HEAD_DIM_B, seq), (0, 3, 1, 2)),
                 tail[:, SUBLANES - (CONV_WIDTH - 1):, :])

        qa, ka, va, qb, kb, vb, gates = _in_proj_sample(xs, g_attn, w_std, w_keys, qg_t, kg_t, gm)
        oa = _da_sample(pt, qa, ka, va, blast, bnew, wl, sg, cda_kt, cda_vi, l, dec_b, dec_t, lam_init)
        ob = _sb_sample(pt, qb, kb, vb, u_hi_lo, csb_kt, csb_vt, l, dec_b, dec_t)
        xs = _merge_out(xs, oa, ob, gates, wpa, wpb, wo)
        st = state_conv[l]
        zeros = jnp.zeros((dec_b, dec_t - 1, ff), F32)
        s1 = jnp.concatenate([st[:, 1:2], zeros], axis=1).reshape(dec_b * dec_t, ff)
        s2 = jnp.concatenate([st, zeros[:, 1:]], axis=1).reshape(dec_b * dec_t, ff)
        xs, a_s = _ffn_sample(xs, g_ffn, wup, cw, cb, wdn, s1, s2, dec_t)
        new_s = (ka.reshape(dec_b, dec_t, N_HEADS_A, 2, HEAD_DIM_A),
                 va.reshape(dec_b, dec_t, N_HEADS_A, 2 * HEAD_DIM_A),
                 kb.reshape(dec_b, dec_t, N_HEADS_B, HEAD_DIM_B),
                 vb.reshape(dec_b, dec_t, N_HEADS_B, HEAD_DIM_B),
                 a_s.reshape(dec_b, dec_t, ff)[:, dec_t - (CONV_WIDTH - 1):, :])
        for lst, r in zip(rows_p, new_p):
            lst.append(r)
        for lst, r in zip(rows_s, new_s):
            lst.append(r)

    out_p = [jnp.stack(r) for r in rows_p]
    out_s = [jnp.stack(r) for r in rows_s]
    return (xp.reshape(batch, seq, d), xs.reshape(dec_b, dec_t, d), *out_p, *out_s)
```

```python
import functools
import math

import numpy as np
import jax
import jax.numpy as jnp
from jax import lax
from jax.experimental import pallas as pl
from jax.experimental.pallas import tpu as pltpu

F32 = jnp.float32
BF16 = jnp.bfloat16

N_HEADS_A = 8
HEAD_DIM_A = 64
WIDTH_A = N_HEADS_A * 2 * HEAD_DIM_A
N_HEADS_B = 8
HEAD_DIM_B = 64
WIDTH_B = N_HEADS_B * HEAD_DIM_B
N_BUCKETS = 32
MAX_DISTANCE = 128
LAMBDA_BASE = 0.8
LAMBDA_SCALE = 0.6
LAMBDA_RATE = 0.3
NORM_EPS = 1e-6
NEG_BIG = -1e30
CONV_WIDTH = 3

LANES = 128
SUBLANES = 8
MXU_EDGE = 256
VMEM_LIMIT_BYTES = 56 * 1024 * 1024

TOKEN_TILE = 256
DA_TILE = 512
SB_TILE = MXU_EDGE
PAGES_PER_STEP = 16
SB_NEAR_PAGES = 8
SB_FAR_PAGES = 14

LOG2E = math.log2(math.e)
STICK_UNDERFLOW = -104.0


def _cparams(sem):
    return pltpu.CompilerParams(dimension_semantics=sem, vmem_limit_bytes=VMEM_LIMIT_BYTES)


def _const_spec(shape):
    nd = len(shape)
    return pl.BlockSpec(shape, lambda *_: (0,) * nd)


_NT = (((1,), (1,)), ((), ()))


def _rms_rows(x, g):
    return x * lax.rsqrt(jnp.mean(x * x, axis=-1, keepdims=True) + NORM_EPS) * g


def _head_norm(y, gain, gm):
    sq = (y * y).astype(BF16)
    outs = []
    for j in range(WIDTH_A // MXU_EDGE):
        sl = slice(j * MXU_EDGE, (j + 1) * MXU_EDGE)
        ms = jnp.dot(sq[:, sl], gm, preferred_element_type=F32)
        outs.append(y[:, sl] * lax.rsqrt(ms + NORM_EPS))
    return jnp.concatenate(outs, axis=1) * gain


def _in_proj_prompt_kernel(x_ref, g_ref, ws_ref, wt_ref, qg_ref, kgc_ref, gm_ref, *refs, n_prev):
    prev_refs, refs = refs[:4 if n_prev else 0], refs[4 if n_prev else 0:]
    qa_ref, kat_ref, katb_ref, vai_ref, vab_ref, qb_ref, kbt_ref, kbtb_ref, vbt_ref, vbb_ref, gates_ref = refs
    for dst, src in zip((kat_ref, vai_ref, kbt_ref, vbt_ref), prev_refs):
        dst[0:n_prev] = src[...]
    tm = x_ref.shape[0]
    nb = _rms_rows(x_ref[...], g_ref[...]).astype(BF16)

    def seg(lo, width):
        return jnp.dot(nb, ws_ref[:, lo:lo + width], preferred_element_type=F32)

    def seg_t(lo, width):
        return lax.dot_general(wt_ref[lo:lo + width, :], nb, _NT, preferred_element_type=F32)

    qa = _head_norm(seg(0, WIDTH_A), qg_ref[...], gm_ref[...]) * (HEAD_DIM_A ** -0.5 * LOG2E)
    qa_ref[...] = qa.astype(BF16)
    va = seg(WIDTH_A, WIDTH_A)
    vab_ref[...] = va.astype(BF16)
    for h in range(N_HEADS_A):
        vai_ref[n_prev, pl.ds(h, tm, stride=N_HEADS_A), :] = va[:, h * LANES:(h + 1) * LANES]
    off = 2 * WIDTH_A
    qb_ref[...] = (seg(off, WIDTH_B) * (HEAD_DIM_B ** -0.5)).astype(BF16)
    vbb_ref[...] = seg(off + WIDTH_B, WIDTH_B).astype(BF16)
    gates_ref[...] = seg(off + 2 * WIDTH_B, gates_ref.shape[1])

    kat = seg_t(0, WIDTH_A).reshape(WIDTH_A // HEAD_DIM_A, HEAD_DIM_A, tm)
    ms = jnp.mean(kat * kat, axis=1, keepdims=True)
    kat = (kat * lax.rsqrt(ms + NORM_EPS) * kgc_ref[...].reshape(1, HEAD_DIM_A, 1)).reshape(WIDTH_A, tm)
    kat_ref[n_prev, 0] = kat
    katb_ref[0] = kat.astype(BF16)
    kbt = seg_t(WIDTH_A, WIDTH_B)
    kbt_ref[n_prev, 0] = kbt
    kbtb_ref[0] = kbt.astype(BF16)
    vbt_ref[n_prev, 0] = seg_t(WIDTH_A + WIDTH_B, WIDTH_B)


def _in_proj_prompt(x, g, ws, wt, qg_t, kg_col, gm, batch, seq, prev):
    n, d = x.shape
    tm = min(TOKEN_TILE, seq)
    tps = seq // tm
    n_prev = 0 if prev is None else prev[0].shape[0]
    gate_w = ws.shape[1] - 2 * WIDTH_A - 2 * WIDTH_B
    row = lambda w: pl.BlockSpec((tm, w), lambda i: (i, 0))
    t_blk = lambda layers, w: pl.BlockSpec((layers, 1, w, tm), lambda i: (0, i // tps, 0, i % tps))
    vi_blk = lambda layers: pl.BlockSpec((layers, tm * N_HEADS_A, LANES), lambda i: (0, i, 0))
    t_out = lambda w: t_blk(n_prev + 1, w)
    prev_specs = [] if prev is None else [t_blk(n_prev, WIDTH_A), vi_blk(n_prev),
                                          t_blk(n_prev, WIDTH_B), t_blk(n_prev, WIDTH_B)]
    da_tile, sb_tile = min(DA_TILE, seq), min(SB_TILE, seq)
    t_tile = lambda w, kt: pl.BlockSpec((1, w, tm), lambda i: (i // (kt // tm), 0, i % (kt // tm)))
    out_specs = [row(WIDTH_A), t_out(WIDTH_A), t_tile(WIDTH_A, da_tile),
                 vi_blk(n_prev + 1), row(WIDTH_A),
                 row(WIDTH_B), t_out(WIDTH_B), t_tile(WIDTH_B, sb_tile), t_out(WIDTH_B), row(WIDTH_B), row(gate_w)]
    layers = n_prev + 1
    out_shape = [jax.ShapeDtypeStruct((n, WIDTH_A), BF16),
                 jax.ShapeDtypeStruct((layers, batch, WIDTH_A, seq), F32),
                 jax.ShapeDtypeStruct((n // da_tile, WIDTH_A, da_tile), BF16),
                 jax.ShapeDtypeStruct((layers, n * N_HEADS_A, LANES), F32),
                 jax.ShapeDtypeStruct((n, WIDTH_A), BF16),
                 jax.ShapeDtypeStruct((n, WIDTH_B), BF16),
                 jax.ShapeDtypeStruct((layers, batch, WIDTH_B, seq), F32),
                 jax.ShapeDtypeStruct((n // sb_tile, WIDTH_B, sb_tile), BF16),
                 jax.ShapeDtypeStruct((layers, batch, WIDTH_B, seq), F32),
                 jax.ShapeDtypeStruct((n, WIDTH_B), BF16),
                 jax.ShapeDtypeStruct((n, gate_w), F32)]
    return pl.pallas_call(
        functools.partial(_in_proj_prompt_kernel, n_prev=n_prev),
        grid=(n // tm,),
        in_specs=[row(d), _const_spec(g.shape), _const_spec(ws.shape), _const_spec(wt.shape),
                  _const_spec(qg_t.shape), _const_spec(kg_col.shape), _const_spec(gm.shape)] + prev_specs,
        out_specs=out_specs,
        out_shape=out_shape,
        compiler_params=_cparams(("arbitrary",)),
        name="in_proj_prompt",
    )(x, g, ws, wt, qg_t, kg_col, gm, *(prev or ()))


def _in_proj_sample_kernel(x_ref, g_ref, ws_ref, wk_ref, qg_ref, kg_ref, gm_ref,
                           qa_ref, ka_ref, va_ref, qb_ref, kb_ref, vb_ref, gates_ref):
    nb = _rms_rows(x_ref[...], g_ref[...]).astype(BF16)

    def seg(w_ref, lo, width):
        return jnp.dot(nb, w_ref[:, lo:lo + width], preferred_element_type=F32)

    qa_ref[...] = _head_norm(seg(ws_ref, 0, WIDTH_A), qg_ref[...], gm_ref[...]) * (HEAD_DIM_A ** -0.5 * LOG2E)
    va_ref[...] = seg(ws_ref, WIDTH_A, WIDTH_A)
    off = 2 * WIDTH_A
    qb_ref[...] = seg(ws_ref, off, WIDTH_B) * (HEAD_DIM_B ** -0.5)
    vb_ref[...] = seg(ws_ref, off + WIDTH_B, WIDTH_B)
    gates_ref[...] = seg(ws_ref, off + 2 * WIDTH_B, gates_ref.shape[1])
    ka_ref[...] = _head_norm(seg(wk_ref, 0, WIDTH_A), kg_ref[...], gm_ref[...])
    kb_ref[...] = seg(wk_ref, WIDTH_A, WIDTH_B)


def _in_proj_sample(x, g, ws, wk, qg_t, kg_t, gm):
    n, d = x.shape
    tm = min(TOKEN_TILE, n)
    gate_w = ws.shape[1] - 2 * WIDTH_A - 2 * WIDTH_B
    row = lambda w: pl.BlockSpec((tm, w), lambda i: (i, 0))
    widths = [WIDTH_A, WIDTH_A, WIDTH_A, WIDTH_B, WIDTH_B, WIDTH_B, gate_w]
    return pl.pallas_call(
        _in_proj_sample_kernel,
        grid=(n // tm,),
        in_specs=[row(d), _const_spec(g.shape), _const_spec(ws.shape), _const_spec(wk.shape),
                  _const_spec(qg_t.shape), _const_spec(kg_t.shape), _const_spec(gm.shape)],
        out_specs=[row(w) for w in widths],
        out_shape=[jax.ShapeDtypeStruct((n, w), F32) for w in widths],
        compiler_params=_cparams(("arbitrary",)),
        name="in_proj_sample",
    )(x, g, ws, wk, qg_t, kg_t, gm)


def _lambda_value(wl, lam_init):
    a = jnp.sum(wl[0:1] * wl[1:2], axis=1, keepdims=True)
    b = jnp.sum(wl[2:3] * wl[3:4], axis=1, keepdims=True)
    return jnp.exp(a) - jnp.exp(b) + lam_init


def _sub_norm(d, gain, lam_init):
    y = d * lax.rsqrt(jnp.mean(d * d, axis=-1, keepdims=True) + NORM_EPS)
    return y * gain * (1.0 - lam_init)


def _softmax_probs(s, m_sc, l_sc):
    m_prev = m_sc[...]
    m_new = jnp.maximum(m_prev, jnp.max(s, axis=-1, keepdims=True))
    alpha = jnp.exp2(m_prev - m_new)
    p = jnp.exp2(s - m_new)
    l_sc[...] = alpha * l_sc[...] + jnp.sum(p, axis=-1, keepdims=True)
    m_sc[...] = m_new
    return alpha, p.astype(BF16)


def _lane_tile(x, width):
    reps = width // x.shape[1]
    return x if reps == 1 else jnp.concatenate([x] * reps, axis=1)


def _stick_weights(z, u_hi_lo, carry_sc, valid):
    width = z.shape[1]
    u = jnp.log(1.0 + jnp.exp(-jnp.abs(z)))
    ls = -(jnp.maximum(z, 0.0) + u)
    if valid is not None:
        ls = jnp.where(valid, ls, 0.0)
    hi = ls.astype(BF16)
    lo = (ls - hi.astype(F32)).astype(BF16)
    later = jnp.dot(jnp.concatenate([hi, lo], axis=1), u_hi_lo, preferred_element_type=F32)
    carry = carry_sc[...]
    a = jnp.exp(jnp.minimum(z, 0.0) - u + later + _lane_tile(carry, width))
    if valid is not None:
        a = jnp.where(valid, a, 0.0)
    carry_sc[...] = carry + jnp.sum(ls, axis=-1, keepdims=True)
    return a.astype(BF16)


def _split_lanes(q):
    lane = lax.broadcasted_iota(jnp.int32, q.shape, 1)
    zero = jnp.zeros_like(q)
    half = LANES // 2
    return jnp.concatenate([jnp.where(lane < half, q, zero), jnp.where(lane >= half, q, zero)], axis=0)


def _da_prompt_kernel(q_ref, kt_ref, v_ref, bias_ref, wl_ref, sg_ref, o_ref, m_sc, acc_sc,
                      *, tq, lam_init):
    qi = pl.program_id(2)
    qq = _split_lanes(q_ref[...])
    m_sc[...] = jnp.full(m_sc.shape, NEG_BIG, F32)
    acc_sc[...] = jnp.zeros(acc_sc.shape, F32)
    ones = jnp.ones((tq, LANES), BF16)

    def block(ki, bias):
        v_ext = jnp.concatenate([v_ref[pl.ds(pl.multiple_of(ki * tq, tq), tq), :], ones], axis=1)
        s = jnp.dot(qq, kt_ref[ki], preferred_element_type=F32)
        if bias is not None:
            s = (s.reshape(2, tq, tq) + bias[None]).reshape(2 * tq, tq)
        m_prev = m_sc[...]
        m_new = jnp.maximum(m_prev, jnp.max(s, axis=-1, keepdims=True))
        alpha = jnp.exp2(m_prev - m_new)
        p = jnp.concatenate([jnp.exp2(s[:, j * LANES:(j + 1) * LANES] - m_new).astype(BF16)
                             for j in range(tq // LANES)], axis=1)
        acc_sc[...] = _lane_tile(alpha, 2 * LANES) * acc_sc[...] + jnp.dot(p, v_ext, preferred_element_type=F32)
        m_sc[...] = m_new

    def far(ki, c):
        block(ki, None)
        return c

    lax.fori_loop(0, qi - 1, far, 0)

    @pl.when(qi >= 1)
    def _():
        block(qi - 1, bias_ref[0, 0])

    block(qi, bias_ref[0, 1])

    acc = acc_sc[...]
    o = acc[:, :LANES] / acc[:, LANES:]
    lam = _lambda_value(wl_ref[...], lam_init)
    d = o[:tq] - lam * o[tq:]
    o_ref[...] = _sub_norm(d, sg_ref[...], lam_init).astype(o_ref.dtype)


def _da_prompt(q, kt, v, bias, wl, sg, batch, seq, lam_init):
    n = q.shape[0]
    tq = kt.shape[2]
    nq = seq // tq
    kern = functools.partial(_da_prompt_kernel, tq=tq, lam_init=lam_init)
    return pl.pallas_call(
        kern,
        grid=(batch, N_HEADS_A, nq),
        in_specs=[pl.BlockSpec((tq, LANES), lambda b, h, i: (b * nq + i, h)),
                  pl.BlockSpec((nq, LANES, tq), lambda b, h, i: (b, h, 0)),
                  pl.BlockSpec((seq, LANES), lambda b, h, i: (b, h)),
                  pl.BlockSpec((1, 2, tq, tq), lambda b, h, i: (h, 0, 0, 0)),
                  _const_spec(wl.shape), _const_spec(sg.shape)],
        out_specs=pl.BlockSpec((tq, LANES), lambda b, h, i: (b * nq + i, h)),
        out_shape=jax.ShapeDtypeStruct((n, WIDTH_A), BF16),
        scratch_shapes=[pltpu.VMEM((2 * tq, LANES), F32), pltpu.VMEM((2 * tq, 2 * LANES), F32)],
        compiler_params=_cparams(("arbitrary", "arbitrary", "arbitrary")),
        name="da_prompt",
    )(q, kt, v, bias, wl, sg)


def _sb_prompt_kernel(q_ref, kt_ref, v_ref, u_ref, o_ref, carry_sc, acc_sc, *, tq):
    qi = pl.program_id(2)
    qq = _split_lanes(q_ref[...])
    carry_sc[...] = jnp.zeros(carry_sc.shape, F32)
    acc_sc[...] = jnp.zeros(acc_sc.shape, F32)

    def block(ki, valid):
        vb = v_ref[pl.ds(pl.multiple_of(ki * tq, tq), tq), :]
        z = jnp.dot(qq, kt_ref[ki], preferred_element_type=F32)
        a = _stick_weights(z, u_ref[...], carry_sc, valid)
        acc_sc[...] = acc_sc[...] + jnp.dot(a, vb, preferred_element_type=F32)
        return jnp.max(carry_sc[...])

    row = lax.broadcasted_iota(jnp.int32, (2, tq, tq), 1).reshape(2 * tq, tq)
    col = lax.broadcasted_iota(jnp.int32, (2 * tq, tq), 1)
    top = block(qi, col < row)

    def more(c):
        return jnp.logical_and(c[0] < qi, c[1] > STICK_UNDERFLOW)

    def left(c):
        return c[0] + 1, block(qi - 1 - c[0], None)

    lax.while_loop(more, left, (jnp.int32(0), top))

    acc = acc_sc[...]
    lane = lax.broadcasted_iota(jnp.int32, (tq, LANES), 1)
    o_ref[...] = jnp.where(lane < LANES // 2, acc[:tq], acc[tq:]).astype(o_ref.dtype)


def _sb_prompt(q, kt, v, u_hi_lo, batch, seq):
    n = q.shape[0]
    tq = kt.shape[2]
    nq = seq // tq
    pairs = WIDTH_B // LANES
    kern = functools.partial(_sb_prompt_kernel, tq=tq)
    return pl.pallas_call(
        kern,
        grid=(batch, pairs, nq),
        in_specs=[pl.BlockSpec((tq, LANES), lambda b, h, i: (b * nq + i, h)),
                  pl.BlockSpec((nq, LANES, tq), lambda b, h, i: (b, h, 0)),
                  pl.BlockSpec((seq, LANES), lambda b, h, i: (b, h)),
                  _const_spec(u_hi_lo.shape)],
        out_specs=pl.BlockSpec((tq, LANES), lambda b, h, i: (b * nq + i, h)),
        out_shape=jax.ShapeDtypeStruct((n, WIDTH_B), BF16),
        scratch_shapes=[pltpu.VMEM((2 * tq, LANES), F32), pltpu.VMEM((2 * tq, LANES), F32)],
        compiler_params=_cparams(("arbitrary", "arbitrary", "arbitrary")),
        name="sb_prompt",
    )(q, kt, v, u_hi_lo)


def _block_diag_queries(q, groups, width):
    t = q.shape[0]
    rep = jnp.concatenate([q] * groups, axis=0)
    r = lax.broadcasted_iota(jnp.int32, rep.shape, 0)
    c = lax.broadcasted_iota(jnp.int32, rep.shape, 1)
    keep = (c // width) == (r // t)
    return jnp.where(keep, rep, 0.0).astype(BF16)


def _pad_rows(x, rows):
    return jnp.concatenate([x, jnp.zeros((rows - x.shape[0], x.shape[1]), x.dtype)], axis=0)


def _da_sample_kernel(pt_ref, q_ref, knew_ref, vnew_ref, blast_ref, bnew_ref, wl_ref, sg_ref, *rest,
                      pages, page, lam_init):
    del pt_ref
    kt_refs, v_refs = rest[:pages], rest[pages:2 * pages]
    o_ref, qbd_sc, m_sc, l_sc, acc_sc = rest[2 * pages:]
    g = pl.program_id(1)
    last = pl.num_programs(1) - 1
    t = q_ref.shape[0]
    rows_h = 2 * t

    @pl.when(g == 0)
    def _():
        qbd_sc[...] = _block_diag_queries(q_ref[...], 2 * N_HEADS_A, HEAD_DIM_A)
        m_sc[...] = jnp.full(m_sc.shape, NEG_BIG, F32)
        l_sc[...] = jnp.zeros(l_sc.shape, F32)
        acc_sc[...] = jnp.zeros(acc_sc.shape, F32)

    qbd = qbd_sc[...]
    ktcat = jnp.concatenate([r[0, 0].astype(BF16) for r in kt_refs], axis=1)
    s = jnp.dot(qbd, ktcat, preferred_element_type=F32)
    s = s + blast_ref[...] * (g == last).astype(F32)
    alpha, p = _softmax_probs(s, m_sc, l_sc)
    for h in range(N_HEADS_A):
        vh = jnp.concatenate([r[0, 0, pl.ds(h, page, stride=N_HEADS_A), :] for r in v_refs], axis=0)
        rs = slice(h * rows_h, (h + 1) * rows_h)
        acc_sc[rs, :] = alpha[rs] * acc_sc[rs, :] + jnp.dot(p[rs], vh.astype(BF16), preferred_element_type=F32)

    @pl.when(g == last)
    def _():
        kn = _pad_rows(knew_ref[...], page).astype(BF16)
        vn = _pad_rows(vnew_ref[...], page).astype(BF16)
        s2 = lax.dot_general(qbd, kn, _NT, preferred_element_type=F32) + bnew_ref[...]
        alpha2, p2 = _softmax_probs(s2, m_sc, l_sc)
        o_all = jnp.dot(p2, vn, preferred_element_type=F32)
        inv_l = 1.0 / l_sc[...]
        lam = _lambda_value(wl_ref[...], lam_init)
        outs = []
        for h in range(N_HEADS_A):
            rs = slice(h * rows_h, (h + 1) * rows_h)
            cols = slice(h * 2 * HEAD_DIM_A, (h + 1) * 2 * HEAD_DIM_A)
            o = (alpha2[rs] * acc_sc[rs, :] + o_all[rs, cols]) * inv_l[rs]
            outs.append(_sub_norm(o[:t] - lam * o[t:], sg_ref[...], lam_init))
        o_ref[...] = jnp.concatenate(outs, axis=1)


def _page_specs(layer, n_pages, pages, block, reverse):
    specs = []
    for j in range(pages):
        if reverse:
            idx = lambda b, g, pt, j=j: (layer, pt[b * n_pages + n_pages - (g + 1) * pages + j], 0, 0)
        else:
            idx = lambda b, g, pt, j=j: (layer, pt[b * n_pages + g * pages + j], 0, 0)
        specs.append(pl.BlockSpec((1, 1) + block, idx))
    return specs


def _da_sample(pt, q, knew, vnew, blast, bnew, wl, sg, cache_kt, cache_vi, layer, dec_b, dec_t, lam_init):
    n_pages = pt.shape[0] // dec_b
    page = cache_kt.shape[3]
    pages = min(PAGES_PER_STEP, n_pages)
    rows = 2 * N_HEADS_A * dec_t
    tok = lambda w: pl.BlockSpec((dec_t, w), lambda b, g, pt: (b, 0))
    const = lambda a: pl.BlockSpec(a.shape, lambda b, g, pt: (0,) * a.ndim)
    kern = functools.partial(_da_sample_kernel, pages=pages, page=page, lam_init=lam_init)
    grid_spec = pltpu.PrefetchScalarGridSpec(
        num_scalar_prefetch=1,
        grid=(dec_b, n_pages // pages),
        in_specs=[tok(WIDTH_A), tok(WIDTH_A), tok(WIDTH_A), const(blast), const(bnew), const(wl), const(sg)]
        + _page_specs(layer, n_pages, pages, (WIDTH_A, page), False)
        + _page_specs(layer, n_pages, pages, (page * N_HEADS_A, 2 * HEAD_DIM_A), False),
        out_specs=tok(WIDTH_A),
        scratch_shapes=[pltpu.VMEM((rows, WIDTH_A), BF16), pltpu.VMEM((rows, 1), F32),
                        pltpu.VMEM((rows, 1), F32), pltpu.VMEM((rows, 2 * HEAD_DIM_A), F32)],
    )
    return pl.pallas_call(
        kern,
        grid_spec=grid_spec,
        out_shape=jax.ShapeDtypeStruct((dec_b * dec_t, WIDTH_A), F32),
        compiler_params=_cparams(("arbitrary", "arbitrary")),
        name="da_sample",
    )(pt, q, knew, vnew, blast, bnew, wl, sg, *([cache_kt] * pages), *([cache_vi] * pages))


def _sb_sweep_pages(qbd, kt_refs, vt_refs, u_ref, carry_ref, acc_ref):
    blk = u_ref.shape[1]
    ktcat = jnp.concatenate([r[0, 0].astype(BF16) for r in kt_refs], axis=1)
    vtcat = jnp.concatenate([r[0, 0].astype(BF16) for r in vt_refs], axis=1)
    z = jnp.dot(qbd, ktcat, preferred_element_type=F32)
    for b in reversed(range(ktcat.shape[1] // blk)):
        sl = slice(b * blk, (b + 1) * blk)
        a = _stick_weights(z[:, sl], u_ref[...], carry_ref, None)
        acc_ref[...] = acc_ref[...] + lax.dot_general(a, vtcat[:, sl], _NT, preferred_element_type=F32)


def _sb_near_kernel(pt_ref, q_ref, knew_ref, vnew_ref, u_ref, *rest, pages, page):
    del pt_ref
    kt_refs, vt_refs = rest[:pages], rest[pages:2 * pages]
    acc_ref, carry_ref = rest[2 * pages:]
    t = q_ref.shape[0]
    qbd = _block_diag_queries(q_ref[...], N_HEADS_B, HEAD_DIM_B)
    carry_ref[...] = jnp.zeros(carry_ref.shape, F32)
    kn = _pad_rows(knew_ref[...], page).astype(BF16)
    vn = _pad_rows(vnew_ref[...], page).astype(BF16)
    z = lax.dot_general(qbd, kn, _NT, preferred_element_type=F32)
    row_t = lax.broadcasted_iota(jnp.int32, z.shape, 0) % t
    col = lax.broadcasted_iota(jnp.int32, z.shape, 1)
    u_page = jnp.concatenate([u_ref[0:page, 0:page], u_ref[0:page, 0:page]], axis=0)
    a = _stick_weights(z, u_page, carry_ref, col < row_t)
    acc_ref[...] = jnp.dot(a, vn, preferred_element_type=F32)
    _sb_sweep_pages(qbd, kt_refs, vt_refs, u_ref, carry_ref, acc_ref)


def _sb_far_kernel(pt_ref, need_ref, q_ref, acc_in_ref, carry_in_ref, u_ref, *rest, pages):
    del pt_ref
    kt_refs, vt_refs = rest[:pages], rest[pages:2 * pages]
    o_ref, qbd_sc, carry_sc, acc_sc = rest[2 * pages:]
    b = pl.program_id(0)
    g = pl.program_id(1)
    last = pl.num_programs(1) - 1
    t = q_ref.shape[0]

    @pl.when(g == 0)
    def _():
        qbd_sc[...] = _block_diag_queries(q_ref[...], N_HEADS_B, HEAD_DIM_B)
        carry_sc[...] = carry_in_ref[...]
        acc_sc[...] = acc_in_ref[...]

    @pl.when(need_ref[b] > 0)
    def _():
        _sb_sweep_pages(qbd_sc[...], kt_refs, vt_refs, u_ref, carry_sc, acc_sc)

    @pl.when(g == last)
    def _():
        acc = acc_sc[...]
        col_h = lax.broadcasted_iota(jnp.int32, (t, WIDTH_B), 1) // HEAD_DIM_B
        out = jnp.zeros((t, WIDTH_B), F32)
        for h in range(N_HEADS_B):
            out = out + jnp.where(col_h == h, acc[h * t:(h + 1) * t, :], 0.0)
        o_ref[...] = out


def _sb_sample(pt, q, knew, vnew, u_hi_lo, cache_kt, cache_vt, layer, dec_b, dec_t):
    n_pages = pt.shape[0] // dec_b
    page = cache_kt.shape[3]
    near = min(SB_NEAR_PAGES, n_pages)
    n_far = n_pages - near
    assert n_far > 0, "the far sweep also does the final head extraction"
    far = max(p for p in range(1, SB_FAR_PAGES + 1) if n_far % p == 0)
    rows = N_HEADS_B * dec_t
    blk = (WIDTH_B, page)

    tok1 = lambda w: pl.BlockSpec((dec_t, w), lambda b, pt: (b, 0))
    state1 = lambda w: pl.BlockSpec((rows, w), lambda b, pt: (b, 0))
    near_specs = [pl.BlockSpec((1, 1) + blk, lambda b, pt, j=j: (layer, pt[b * n_pages + n_far + j], 0, 0))
                  for j in range(near)]
    acc, carry = pl.pallas_call(
        functools.partial(_sb_near_kernel, pages=near, page=page),
        grid_spec=pltpu.PrefetchScalarGridSpec(
            num_scalar_prefetch=1,
            grid=(dec_b,),
            in_specs=[tok1(WIDTH_B), tok1(WIDTH_B), tok1(WIDTH_B),
                      pl.BlockSpec(u_hi_lo.shape, lambda b, pt: (0, 0))] + near_specs + near_specs,
            out_specs=[state1(WIDTH_B), state1(LANES)],
        ),
        out_shape=[jax.ShapeDtypeStruct((dec_b * rows, WIDTH_B), F32),
                   jax.ShapeDtypeStruct((dec_b * rows, LANES), F32)],
        compiler_params=_cparams(("arbitrary",)),
        name="sb_sample_near",
    )(pt, q, knew, vnew, u_hi_lo, *([cache_kt] * near), *([cache_vt] * near))

    need = (jnp.max(carry.reshape(dec_b, rows * LANES), axis=1) > STICK_UNDERFLOW).astype(jnp.int32)
    tok2 = lambda w: pl.BlockSpec((dec_t, w), lambda b, g, pt, nd: (b, 0))
    state2 = lambda w: pl.BlockSpec((rows, w), lambda b, g, pt, nd: (b, 0))

    def far_spec(j):
        def idx(b, g, pt, nd):
            return (layer, jnp.where(nd[b] > 0, pt[b * n_pages + n_far - (g + 1) * far + j], 0), 0, 0)
        return pl.BlockSpec((1, 1) + blk, idx)

    far_specs = [far_spec(j) for j in range(far)]
    return pl.pallas_call(
        functools.partial(_sb_far_kernel, pages=far),
        grid_spec=pltpu.PrefetchScalarGridSpec(
            num_scalar_prefetch=2,
            grid=(dec_b, n_far // far),
            in_specs=[tok2(WIDTH_B), state2(WIDTH_B), state2(LANES),
                      pl.BlockSpec(u_hi_lo.shape, lambda b, g, pt, nd: (0, 0))] + far_specs + far_specs,
            out_specs=tok2(WIDTH_B),
            scratch_shapes=[pltpu.VMEM((rows, WIDTH_B), BF16), pltpu.VMEM((rows, LANES), F32),
                            pltpu.VMEM((rows, WIDTH_B), F32)],
        ),
        out_shape=jax.ShapeDtypeStruct((dec_b * dec_t, WIDTH_B), F32),
        compiler_params=_cparams(("arbitrary", "arbitrary")),
        name="sb_sample_far",
    )(pt, need, q, acc, carry, u_hi_lo, *([cache_kt] * far), *([cache_vt] * far))


def _merge_out_kernel(x_ref, oa_ref, ob_ref, gates_ref, wpa_ref, wpb_ref, wo_ref, o_ref):
    d = x_ref.shape[1]
    ga = jax.nn.sigmoid(gates_ref[:, :d])
    gb = jax.nn.sigmoid(gates_ref[:, d:])
    pa = jnp.dot(oa_ref[...].astype(BF16), wpa_ref[...], preferred_element_type=F32)
    pb = jnp.dot(ob_ref[...].astype(BF16), wpb_ref[...], preferred_element_type=F32)
    merged = ga * pa + gb * pb
    o_ref[...] = x_ref[...] + jnp.dot(merged.astype(BF16), wo_ref[...], preferred_element_type=F32)


def _merge_out(x, oa, ob, gates, wpa, wpb, wo):
    n, d = x.shape
    tm = min(TOKEN_TILE, n)
    row = lambda w: pl.BlockSpec((tm, w), lambda i: (i, 0))
    return pl.pallas_call(
        _merge_out_kernel,
        grid=(n // tm,),
        in_specs=[row(d), row(WIDTH_A), row(WIDTH_B), row(2 * d),
                  _const_spec(wpa.shape), _const_spec(wpb.shape), _const_spec(wo.shape)],
        out_specs=row(d),
        out_shape=jax.ShapeDtypeStruct((n, d), F32),
        compiler_params=_cparams(("arbitrary",)),
        name="merge_out",
    )(x, oa, ob, gates, wpa, wpb, wo)


def _gelu_exact(x):
    return 0.5 * x * (1.0 + lax.erf(x * np.float32(math.sqrt(0.5))))


def _ffn_up(x_ref, g_ref, wup_ref, ff):
    x = x_ref[...]
    n2 = _rms_rows(x, g_ref[...]).astype(BF16)
    a = jnp.dot(n2, wup_ref[:, :ff], preferred_element_type=F32)
    b = jnp.dot(n2, wup_ref[:, ff:], preferred_element_type=F32)
    return x, a, b


def _ffn_down(x, a_m1, a_m2, a, b, cw_ref, cb_ref, wdn_ref):
    conv = cb_ref[...] + a_m2 * cw_ref[0:1, :] + a_m1 * cw_ref[1:2, :] + a * cw_ref[2:3, :]
    h = (_gelu_exact(conv) * b).astype(BF16)
    return x + jnp.dot(h, wdn_ref[...], preferred_element_type=F32)


def _ffn_prompt_kernel(x_ref, g_ref, wup_ref, cw_ref, cb_ref, wdn_ref, y_ref, tail_ref, carry_sc,
                       *, tiles_per_seq):
    ff = wdn_ref.shape[0]
    tm = x_ref.shape[0]
    x, a, b = _ffn_up(x_ref, g_ref, wup_ref, ff)
    first = (pl.program_id(0) % tiles_per_seq) == 0
    prev = jnp.where(first, 0.0, carry_sc[...])
    p6 = prev[SUBLANES - 2:SUBLANES - 1, :]
    p7 = prev[SUBLANES - 1:SUBLANES, :]
    row = lax.broadcasted_iota(jnp.int32, a.shape, 0)
    a_m1 = jnp.where(row == 0, p7, pltpu.roll(a, 1, 0))
    a_m2 = jnp.where(row == 0, p6, jnp.where(row == 1, p7, pltpu.roll(a, 2, 0)))
    last_rows = a[tm - SUBLANES:, :]
    carry_sc[...] = last_rows
    tail_ref[0] = last_rows
    y_ref[...] = _ffn_down(x, a_m1, a_m2, a, b, cw_ref, cb_ref, wdn_ref)


def _ffn_sample_kernel(x_ref, g_ref, wup_ref, cw_ref, cb_ref, wdn_ref, s1_ref, s2_ref, y_ref, a_ref,
                       *, dec_t):
    ff = wdn_ref.shape[0]
    x, a, b = _ffn_up(x_ref, g_ref, wup_ref, ff)
    t = lax.broadcasted_iota(jnp.int32, a.shape, 0) % dec_t
    a_m1 = jnp.where(t >= 1, pltpu.roll(a, 1, 0), s1_ref[...])
    a_m2 = jnp.where(t >= 2, pltpu.roll(a, 2, 0), s2_ref[...])
    a_ref[...] = a
    y_ref[...] = _ffn_down(x, a_m1, a_m2, a, b, cw_ref, cb_ref, wdn_ref)


def _ffn_prompt(x, g, wup, cw, cb, wdn, batch, seq):
    n, d = x.shape
    ff = wdn.shape[0]
    tm = min(TOKEN_TILE, seq)
    tiles_per_seq = seq // tm
    row = lambda w: pl.BlockSpec((tm, w), lambda i: (i, 0))
    kern = functools.partial(_ffn_prompt_kernel, tiles_per_seq=tiles_per_seq)
    return pl.pallas_call(
        kern,
        grid=(n // tm,),
        in_specs=[row(d), _const_spec(g.shape), _const_spec(wup.shape), _const_spec(cw.shape),
                  _const_spec(cb.shape), _const_spec(wdn.shape)],
        out_specs=[row(d), pl.BlockSpec((1, SUBLANES, ff), lambda i: (i // tiles_per_seq, 0, 0))],
        out_shape=[jax.ShapeDtypeStruct((n, d), F32), jax.ShapeDtypeStruct((batch, SUBLANES, ff), F32)],
        scratch_shapes=[pltpu.VMEM((SUBLANES, ff), F32)],
        compiler_params=_cparams(("arbitrary",)),
        name="ffn_prompt",
    )(x, g, wup, cw, cb, wdn)


def _ffn_sample(x, g, wup, cw, cb, wdn, s1, s2, dec_t):
    n, d = x.shape
    ff = wdn.shape[0]
    tm = min(TOKEN_TILE, n)
    row = lambda w: pl.BlockSpec((tm, w), lambda i: (i, 0))
    kern = functools.partial(_ffn_sample_kernel, dec_t=dec_t)
    return pl.pallas_call(
        kern,
        grid=(n // tm,),
        in_specs=[row(d), _const_spec(g.shape), _const_spec(wup.shape), _const_spec(cw.shape),
                  _const_spec(cb.shape), _const_spec(wdn.shape), row(ff), row(ff)],
        out_specs=[row(d), row(ff)],
        out_shape=[jax.ShapeDtypeStruct((n, d), F32), jax.ShapeDtypeStruct((n, ff), F32)],
        compiler_params=_cparams(("arbitrary",)),
        name="ffn_sample",
    )(x, g, wup, cw, cb, wdn, s1, s2)


def _bucket_table(max_dist):
    n = np.arange(max_dist + 1, dtype=np.int32)
    max_exact = N_BUCKETS // 2
    nf = np.maximum(n, 1).astype(np.float32)
    large = max_exact + (np.log(nf / np.float32(max_exact)) / np.float32(math.log(MAX_DISTANCE / max_exact))
                         * np.float32(N_BUCKETS - max_exact)).astype(np.int32)
    large = np.minimum(large, N_BUCKETS - 1)
    return np.where(n < max_exact, n, large).astype(np.int32)


def _bias_kernel(rs_ref, prompt_ref, last_ref, new_ref, *, upper, tq, dec_t, page):
    h = pl.program_id(0)

    def lookup(dist):
        val = jnp.zeros(dist.shape, F32) + rs_ref[N_BUCKETS - 1, h]
        for b in range(N_BUCKETS - 2, -1, -1):
            val = jnp.where(dist <= upper[b], rs_ref[b, h], val)
        return jnp.where(dist >= 0, val, NEG_BIG)

    ql = lax.broadcasted_iota(jnp.int32, (tq, tq), 0)
    kl = lax.broadcasted_iota(jnp.int32, (tq, tq), 1)
    prompt_ref[0, 0] = lookup(ql - kl + tq)
    prompt_ref[0, 1] = lookup(ql - kl)
    t = lax.broadcasted_iota(jnp.int32, (2 * dec_t, page), 0) % dec_t
    j = lax.broadcasted_iota(jnp.int32, (2 * dec_t, page), 1)
    last_ref[...] = lookup(page + t - j)
    new_ref[...] = lookup(t - j)


def _bias_tiles(rel_shift, tq, dec_t, page):
    table = _bucket_table(2 * MAX_DISTANCE)
    upper = tuple(int(np.nonzero(table <= b)[0].max()) for b in range(N_BUCKETS - 1))
    heads = rel_shift.shape[1]
    kern = functools.partial(_bias_kernel, upper=upper, tq=tq, dec_t=dec_t, page=page)
    return pl.pallas_call(
        kern,
        grid=(heads,),
        in_specs=[pl.BlockSpec(memory_space=pltpu.SMEM)],
        out_specs=[pl.BlockSpec((1, 2, tq, tq), lambda h: (h, 0, 0, 0)),
                   pl.BlockSpec((2 * dec_t, page), lambda h: (h, 0)),
                   pl.BlockSpec((2 * dec_t, page), lambda h: (h, 0))],
        out_shape=[jax.ShapeDtypeStruct((heads, 2, tq, tq), F32),
                   jax.ShapeDtypeStruct((heads * 2 * dec_t, page), F32),
                   jax.ShapeDtypeStruct((heads * 2 * dec_t, page), F32)],
        compiler_params=_cparams(("arbitrary",)),
        name="bias_tiles",
    )(rel_shift)


def _lower_triangles(w):
    j = np.arange(w)[:, None]
    s = np.arange(w)[None, :]
    u = (j > s).astype(np.float32)
    return jnp.asarray(np.concatenate([u, u], axis=0), dtype=BF16)


def _group_mean_matrix():
    i = np.arange(MXU_EDGE)
    m = (i[:, None] // HEAD_DIM_A == i[None, :] // HEAD_DIM_A).astype(np.float32) / HEAD_DIM_A
    return jnp.asarray(m, dtype=BF16)


def kernel(x_prompt, x_sample, cache_da_k, cache_da_v, cache_sb_k, cache_sb_v, state_conv, page_table,
           rel_bias, attn_norm_g, w_in, q_norm_g, k_norm_g, w_lambda, subln_g, w_proj_a, w_proj_b, w_out,
           ffn_norm_g, w_up, conv_w, conv_b, w_down):
    batch, seq, d = x_prompt.shape
    dec_b, dec_t, _ = x_sample.shape
    depth = w_in.shape[0]
    n_pool, page = cache_da_k.shape[1], cache_da_k.shape[2]
    n_pages = page_table.shape[1]
    past_len = n_pages * page
    ff = w_down.shape[1]

    pt = page_table.reshape(-1).astype(jnp.int32)
    cda_kt = jnp.swapaxes(cache_da_k.reshape(depth, n_pool, page, WIDTH_A), 2, 3)
    cda_vi = cache_da_v.reshape(depth, n_pool, page * N_HEADS_A, 2 * HEAD_DIM_A)
    csb_kt = jnp.swapaxes(cache_sb_k.reshape(depth, n_pool, page, WIDTH_B), 2, 3)
    csb_vt = jnp.swapaxes(cache_sb_v.reshape(depth, n_pool, page, WIDTH_B), 2, 3)

    rel_shift = ((rel_bias - rel_bias[N_BUCKETS - 1]) * LOG2E).astype(F32)
    pages = min(PAGES_PER_STEP, n_pages)
    bias_prompt, blast, bnew = _bias_tiles(rel_shift, min(DA_TILE, seq), dec_t, page)
    blast = jnp.pad(blast, ((0, 0), ((pages - 1) * page, 0)))

    u_hi_lo = _lower_triangles(min(SB_TILE, seq))
    gm = _group_mean_matrix()

    c_qa, c_ka, c_va = 0, WIDTH_A, 2 * WIDTH_A
    c_qb, c_kb, c_vb, c_g = 3 * WIDTH_A, 3 * WIDTH_A + WIDTH_B, 3 * WIDTH_A + 2 * WIDTH_B, 3 * WIDTH_A + 3 * WIDTH_B

    xp = x_prompt.reshape(batch * seq, d)
    xs = x_sample.reshape(dec_b * dec_t, d)
    kv_prompt, tails_p = None, []
    rows_s = [[] for _ in range(5)]
    for l in range(depth):
        lam_init = LAMBDA_BASE - LAMBDA_SCALE * math.exp(-LAMBDA_RATE * l)
        g_attn = attn_norm_g[l].reshape(1, d)
        wl_in = w_in[l].astype(BF16)
        w_std = jnp.concatenate([wl_in[:, c_qa:c_ka], wl_in[:, c_va:c_qb], wl_in[:, c_qb:c_kb],
                                 wl_in[:, c_vb:c_g], wl_in[:, c_g:]], axis=1)
        w_keys = jnp.concatenate([wl_in[:, c_ka:c_va], wl_in[:, c_kb:c_vb]], axis=1)
        w_t = jnp.concatenate([w_keys, wl_in[:, c_vb:c_g]], axis=1).T
        qg_t = jnp.tile(q_norm_g[l], WIDTH_A // HEAD_DIM_A).reshape(1, WIDTH_A)
        kg_t = jnp.tile(k_norm_g[l], WIDTH_A // HEAD_DIM_A).reshape(1, WIDTH_A)
        kg_col = k_norm_g[l].reshape(HEAD_DIM_A, 1)
        wl = w_lambda[l]
        sg = subln_g[l].reshape(1, 2 * HEAD_DIM_A)
        wpa, wpb, wo = w_proj_a[l].astype(BF16), w_proj_b[l].astype(BF16), w_out[l].astype(BF16)
        g_ffn = ffn_norm_g[l].reshape(1, d)
        wup, wdn = w_up[l].astype(BF16), w_down[l].astype(BF16)
        cw, cb = conv_w[l], conv_b[l].reshape(1, ff)

        (qa, kat, katb, vai, vab, qb, kbt, kbtb, vbt, vbb, gates) = _in_proj_prompt(
            xp, g_attn, w_std, w_t, qg_t, kg_col, gm, batch, seq, kv_prompt)
        kv_prompt = (kat, vai, kbt, vbt)
        oa = _da_prompt(qa, katb, vab, bias_prompt, wl, sg, batch, seq, lam_init)
        ob = _sb_prompt(qb, kbtb, vbb, u_hi_lo, batch, seq)
        xp = _merge_out(xp, oa, ob, gates, wpa, wpb, wo)
        xp, tail = _ffn_prompt(xp, g_ffn, wup, cw, cb, wdn, batch, seq)
        tails_p.append(tail[:, SUBLANES - (CONV_WIDTH - 1):, :])

        qa, ka, va, qb, kb, vb, gates = _in_proj_sample(xs, g_attn, w_std, w_keys, qg_t, kg_t, gm)
        oa = _da_sample(pt, qa, ka, va, blast, bnew, wl, sg, cda_kt, cda_vi, l, dec_b, dec_t, lam_init)
        ob = _sb_sample(pt, qb, kb, vb, u_hi_lo, csb_kt, csb_vt, l, dec_b, dec_t)
        xs = _merge_out(xs, oa, ob, gates, wpa, wpb, wo)
        st = state_conv[l]
        zeros = jnp.zeros((dec_b, dec_t - 1, ff), F32)
        s1 = jnp.concatenate([st[:, 1:2], zeros], axis=1).reshape(dec_b * dec_t, ff)
        s2 = jnp.concatenate([st, zeros[:, 1:]], axis=1).reshape(dec_b * dec_t, ff)
        xs, a_s = _ffn_sample(xs, g_ffn, wup, cw, cb, wdn, s1, s2, dec_t)
        new_s = (ka.reshape(dec_b, dec_t, N_HEADS_A, 2, HEAD_DIM_A),
                 va.reshape(dec_b, dec_t, N_HEADS_A, 2 * HEAD_DIM_A),
                 kb.reshape(dec_b, dec_t, N_HEADS_B, HEAD_DIM_B),
                 vb.reshape(dec_b, dec_t, N_HEADS_B, HEAD_DIM_B),
                 a_s.reshape(dec_b, dec_t, ff)[:, dec_t - (CONV_WIDTH - 1):, :])
        for lst, r in zip(rows_s, new_s):
            lst.append(r)

    kat, vai, kbt, vbt = kv_prompt
    out_p = [jnp.transpose(kat.reshape(depth, batch, N_HEADS_A, 2, HEAD_DIM_A, seq), (0, 1, 5, 2, 3, 4)),
             vai.reshape(depth, batch, seq, N_HEADS_A, 2 * HEAD_DIM_A),
             jnp.transpose(kbt.reshape(depth, batch, N_HEADS_B, HEAD_DIM_B, seq), (0, 1, 4, 2, 3)),
             jnp.transpose(vbt.reshape(depth, batch, N_HEADS_B, HEAD_DIM_B, seq), (0, 1, 4, 2, 3)),
             jnp.stack(tails_p)]
    out_s = [jnp.stack(r) for r in rows_s]
    return (xp.reshape(batch, seq, d), xs.reshape(dec_b, dec_t, d), *out_p, *out_s)
```

```python
import functools
import math

import numpy as np
import jax
import jax.numpy as jnp
from jax import lax
from jax.experimental import pallas as pl
from jax.experimental.pallas import tpu as pltpu

F32 = jnp.float32
BF16 = jnp.bfloat16

N_HEADS_A = 8
HEAD_DIM_A = 64
WIDTH_A = N_HEADS_A * 2 * HEAD_DIM_A
N_HEADS_B = 8
HEAD_DIM_B = 64
WIDTH_B = N_HEADS_B * HEAD_DIM_B
N_BUCKETS = 32
MAX_DISTANCE = 128
LAMBDA_BASE = 0.8
LAMBDA_SCALE = 0.6
LAMBDA_RATE = 0.3
NORM_EPS = 1e-6
NEG_BIG = -1e30
CONV_WIDTH = 3

LANES = 128
SUBLANES = 8
MXU_EDGE = 256
VMEM_LIMIT_BYTES = 56 * 1024 * 1024

TOKEN_TILE = 256
DA_TILE = 512
SB_TILE = MXU_EDGE
PAGES_PER_STEP = 16
SB_NEAR_PAGES = 8
SB_FAR_PAGES = 14

LOG2E = math.log2(math.e)
STICK_UNDERFLOW = -104.0


def _cparams(sem):
    return pltpu.CompilerParams(dimension_semantics=sem, vmem_limit_bytes=VMEM_LIMIT_BYTES)


def _const_spec(shape):
    nd = len(shape)
    return pl.BlockSpec(shape, lambda *_: (0,) * nd)


_NT = (((1,), (1,)), ((), ()))


def _rms_rows(x, g):
    return x * lax.rsqrt(jnp.mean(x * x, axis=-1, keepdims=True) + NORM_EPS) * g


def _head_norm(y, gain, gm):
    sq = (y * y).astype(BF16)
    outs = []
    for j in range(WIDTH_A // MXU_EDGE):
        sl = slice(j * MXU_EDGE, (j + 1) * MXU_EDGE)
        ms = jnp.dot(sq[:, sl], gm, preferred_element_type=F32)
        outs.append(y[:, sl] * lax.rsqrt(ms + NORM_EPS))
    return jnp.concatenate(outs, axis=1) * gain


def _in_proj_prompt_kernel(x_ref, g_ref, ws_ref, wt_ref, qg_ref, kgc_ref, gm_ref, *refs, n_prev):
    prev_refs, refs = refs[:4 if n_prev else 0], refs[4 if n_prev else 0:]
    qa_ref, kat_ref, katb_ref, vai_ref, vab_ref, qb_ref, kbt_ref, kbtb_ref, vbt_ref, vbb_ref, gates_ref = refs
    for dst, src in zip((kat_ref, vai_ref, kbt_ref, vbt_ref), prev_refs):
        dst[0:n_prev] = src[...]
    tm = x_ref.shape[0]
    nb = _rms_rows(x_ref[...], g_ref[...]).astype(BF16)

    def seg(lo, width):
        return jnp.dot(nb, ws_ref[:, lo:lo + width], preferred_element_type=F32)

    def seg_t(lo, width):
        return lax.dot_general(wt_ref[lo:lo + width, :], nb, _NT, preferred_element_type=F32)

    qa = _head_norm(seg(0, WIDTH_A), qg_ref[...], gm_ref[...]) * (HEAD_DIM_A ** -0.5 * LOG2E)
    qa_ref[...] = qa.astype(BF16)
    va = seg(WIDTH_A, WIDTH_A)
    vab_ref[...] = va.astype(BF16)
    for h in range(N_HEADS_A):
        vai_ref[n_prev, pl.ds(h, tm, stride=N_HEADS_A), :] = va[:, h * LANES:(h + 1) * LANES]
    off = 2 * WIDTH_A
    qb_ref[...] = (seg(off, WIDTH_B) * (HEAD_DIM_B ** -0.5)).astype(BF16)
    vbb_ref[...] = seg(off + WIDTH_B, WIDTH_B).astype(BF16)
    gates_ref[...] = seg(off + 2 * WIDTH_B, gates_ref.shape[1])

    kat = seg_t(0, WIDTH_A).reshape(WIDTH_A // HEAD_DIM_A, HEAD_DIM_A, tm)
    ms = jnp.mean(kat * kat, axis=1, keepdims=True)
    kat = (kat * lax.rsqrt(ms + NORM_EPS) * kgc_ref[...].reshape(1, HEAD_DIM_A, 1)).reshape(WIDTH_A, tm)
    kat_ref[n_prev, 0] = kat
    katb_ref[0] = kat.astype(BF16)
    kbt = seg_t(WIDTH_A, WIDTH_B)
    kbt_ref[n_prev, 0] = kbt
    kbtb_ref[0] = kbt.astype(BF16)
    vbt_ref[n_prev, 0] = seg_t(WIDTH_A + WIDTH_B, WIDTH_B)


def _in_proj_prompt(x, g, ws, wt, qg_t, kg_col, gm, batch, seq, prev):
    n, d = x.shape
    tm = min(TOKEN_TILE, seq)
    tps = seq // tm
    n_prev = 0 if prev is None else prev[0].shape[0]
    gate_w = ws.shape[1] - 2 * WIDTH_A - 2 * WIDTH_B
    row = lambda w: pl.BlockSpec((tm, w), lambda i: (i, 0))
    t_blk = lambda layers, w: pl.BlockSpec((layers, 1, w, tm), lambda i: (0, i // tps, 0, i % tps))
    vi_blk = lambda layers: pl.BlockSpec((layers, tm * N_HEADS_A, LANES), lambda i: (0, i, 0))
    t_out = lambda w: t_blk(n_prev + 1, w)
    prev_specs = [] if prev is None else [t_blk(n_prev, WIDTH_A), vi_blk(n_prev),
                                          t_blk(n_prev, WIDTH_B), t_blk(n_prev, WIDTH_B)]
    da_tile, sb_tile = min(DA_TILE, seq), min(SB_TILE, seq)
    t_tile = lambda w, kt: pl.BlockSpec((1, w, tm), lambda i: (i // (kt // tm), 0, i % (kt // tm)))
    out_specs = [row(WIDTH_A), t_out(WIDTH_A), t_tile(WIDTH_A, da_tile),
                 vi_blk(n_prev + 1), row(WIDTH_A),
                 row(WIDTH_B), t_out(WIDTH_B), t_tile(WIDTH_B, sb_tile), t_out(WIDTH_B), row(WIDTH_B), row(gate_w)]
    layers = n_prev + 1
    out_shape = [jax.ShapeDtypeStruct((n, WIDTH_A), BF16),
                 jax.ShapeDtypeStruct((layers, batch, WIDTH_A, seq), F32),
                 jax.ShapeDtypeStruct((n // da_tile, WIDTH_A, da_tile), BF16),
                 jax.ShapeDtypeStruct((layers, n * N_HEADS_A, LANES), F32),
                 jax.ShapeDtypeStruct((n, WIDTH_A), BF16),
                 jax.ShapeDtypeStruct((n, WIDTH_B), BF16),
                 jax.ShapeDtypeStruct((layers, batch, WIDTH_B, seq), F32),
                 jax.ShapeDtypeStruct((n // sb_tile, WIDTH_B, sb_tile), BF16),
                 jax.ShapeDtypeStruct((layers, batch, WIDTH_B, seq), F32),
                 jax.ShapeDtypeStruct((n, WIDTH_B), BF16),
                 jax.ShapeDtypeStruct((n, gate_w), F32)]
    return pl.pallas_call(
        functools.partial(_in_proj_prompt_kernel, n_prev=n_prev),
        grid=(n // tm,),
        in_specs=[row(d), _const_spec(g.shape), _const_spec(ws.shape), _const_spec(wt.shape),
                  _const_spec(qg_t.shape), _const_spec(kg_col.shape), _const_spec(gm.shape)] + prev_specs,
        out_specs=out_specs,
        out_shape=out_shape,
        compiler_params=_cparams(("arbitrary",)),
        name="in_proj_prompt",
    )(x, g, ws, wt, qg_t, kg_col, gm, *(prev or ()))


def _in_proj_sample_kernel(x_ref, g_ref, ws_ref, wk_ref, qg_ref, kg_ref, gm_ref,
                           qa_ref, ka_ref, va_ref, qb_ref, kb_ref, vb_ref, gates_ref):
    nb = _rms_rows(x_ref[...], g_ref[...]).astype(BF16)

    def seg(w_ref, lo, width):
        return jnp.dot(nb, w_ref[:, lo:lo + width], preferred_element_type=F32)

    qa_ref[...] = _head_norm(seg(ws_ref, 0, WIDTH_A), qg_ref[...], gm_ref[...]) * (HEAD_DIM_A ** -0.5 * LOG2E)
    va_ref[...] = seg(ws_ref, WIDTH_A, WIDTH_A)
    off = 2 * WIDTH_A
    qb_ref[...] = seg(ws_ref, off, WIDTH_B) * (HEAD_DIM_B ** -0.5)
    vb_ref[...] = seg(ws_ref, off + WIDTH_B, WIDTH_B)
    gates_ref[...] = seg(ws_ref, off + 2 * WIDTH_B, gates_ref.shape[1])
    ka_ref[...] = _head_norm(seg(wk_ref, 0, WIDTH_A), kg_ref[...], gm_ref[...])
    kb_ref[...] = seg(wk_ref, WIDTH_A, WIDTH_B)


def _in_proj_sample(x, g, ws, wk, qg_t, kg_t, gm):
    n, d = x.shape
    tm = min(TOKEN_TILE, n)
    gate_w = ws.shape[1] - 2 * WIDTH_A - 2 * WIDTH_B
    row = lambda w: pl.BlockSpec((tm, w), lambda i: (i, 0))
    widths = [WIDTH_A, WIDTH_A, WIDTH_A, WIDTH_B, WIDTH_B, WIDTH_B, gate_w]
    return pl.pallas_call(
        _in_proj_sample_kernel,
        grid=(n // tm,),
        in_specs=[row(d), _const_spec(g.shape), _const_spec(ws.shape), _const_spec(wk.shape),
                  _const_spec(qg_t.shape), _const_spec(kg_t.shape), _const_spec(gm.shape)],
        out_specs=[row(w) for w in widths],
        out_shape=[jax.ShapeDtypeStruct((n, w), F32) for w in widths],
        compiler_params=_cparams(("arbitrary",)),
        name="in_proj_sample",
    )(x, g, ws, wk, qg_t, kg_t, gm)


def _lambda_value(wl, lam_init):
    a = jnp.sum(wl[0:1] * wl[1:2], axis=1, keepdims=True)
    b = jnp.sum(wl[2:3] * wl[3:4], axis=1, keepdims=True)
    return jnp.exp(a) - jnp.exp(b) + lam_init


def _sub_norm(d, gain, lam_init):
    y = d * lax.rsqrt(jnp.mean(d * d, axis=-1, keepdims=True) + NORM_EPS)
    return y * gain * (1.0 - lam_init)


def _softmax_probs(s, m_sc, l_sc):
    m_prev = m_sc[...]
    m_new = jnp.maximum(m_prev, jnp.max(s, axis=-1, keepdims=True))
    alpha = jnp.exp2(m_prev - m_new)
    p = jnp.exp2(s - m_new)
    l_sc[...] = alpha * l_sc[...] + jnp.sum(p, axis=-1, keepdims=True)
    m_sc[...] = m_new
    return alpha, p.astype(BF16)


def _lane_tile(x, width):
    reps = width // x.shape[1]
    return x if reps == 1 else jnp.concatenate([x] * reps, axis=1)


def _stick_weights(z, u_hi_lo, carry_sc, valid):
    width = z.shape[1]
    w = jnp.maximum(z, 0.0) + jnp.log(1.0 + jnp.exp(-jnp.abs(z)))
    if valid is not None:
        w = jnp.where(valid, w, 0.0)
    hi = w.astype(BF16)
    lo = (w - hi.astype(F32)).astype(BF16)
    later = jnp.dot(jnp.concatenate([hi, lo], axis=1), u_hi_lo, preferred_element_type=F32)
    carry = carry_sc[...]
    a = jnp.exp(z - w + later + _lane_tile(carry, width))
    if valid is not None:
        a = jnp.where(valid, a, 0.0)
    carry_sc[...] = carry - jnp.sum(w, axis=-1, keepdims=True)
    return a.astype(BF16)


def _split_lanes(q):
    lane = lax.broadcasted_iota(jnp.int32, q.shape, 1)
    zero = jnp.zeros_like(q)
    half = LANES // 2
    return jnp.concatenate([jnp.where(lane < half, q, zero), jnp.where(lane >= half, q, zero)], axis=0)


def _da_prompt_kernel(q_ref, kt_ref, v_ref, bias_ref, wl_ref, sg_ref, o_ref, m_sc, acc_sc,
                      *, tq, lam_init):
    qi = pl.program_id(2)
    qq = _split_lanes(q_ref[...])
    m_sc[...] = jnp.full(m_sc.shape, NEG_BIG, F32)
    acc_sc[...] = jnp.zeros(acc_sc.shape, F32)
    ones = jnp.ones((tq, LANES), BF16)

    def block(ki, bias):
        v_ext = jnp.concatenate([v_ref[pl.ds(pl.multiple_of(ki * tq, tq), tq), :], ones], axis=1)
        s = jnp.dot(qq, kt_ref[ki], preferred_element_type=F32)
        if bias is not None:
            s = (s.reshape(2, tq, tq) + bias[None]).reshape(2 * tq, tq)
        m_prev = m_sc[...]
        m_new = jnp.maximum(m_prev, jnp.max(s, axis=-1, keepdims=True))
        alpha = jnp.exp2(m_prev - m_new)
        p = jnp.concatenate([jnp.exp2(s[:, j * LANES:(j + 1) * LANES] - m_new).astype(BF16)
                             for j in range(tq // LANES)], axis=1)
        acc_sc[...] = _lane_tile(alpha, 2 * LANES) * acc_sc[...] + jnp.dot(p, v_ext, preferred_element_type=F32)
        m_sc[...] = m_new

    def far(ki, c):
        block(ki, None)
        return c

    lax.fori_loop(0, qi - 1, far, 0)

    @pl.when(qi >= 1)
    def _():
        block(qi - 1, bias_ref[0, 0])

    block(qi, bias_ref[0, 1])

    acc = acc_sc[...]
    o = acc[:, :LANES] / acc[:, LANES:]
    lam = _lambda_value(wl_ref[...], lam_init)
    d = o[:tq] - lam * o[tq:]
    o_ref[...] = _sub_norm(d, sg_ref[...], lam_init).astype(o_ref.dtype)


def _da_prompt(q, kt, v, bias, wl, sg, batch, seq, lam_init):
    n = q.shape[0]
    tq = kt.shape[2]
    nq = seq // tq
    kern = functools.partial(_da_prompt_kernel, tq=tq, lam_init=lam_init)
    return pl.pallas_call(
        kern,
        grid=(batch, N_HEADS_A, nq),
        in_specs=[pl.BlockSpec((tq, LANES), lambda b, h, i: (b * nq + i, h)),
                  pl.BlockSpec((nq, LANES, tq), lambda b, h, i: (b, h, 0)),
                  pl.BlockSpec((seq, LANES), lambda b, h, i: (b, h)),
                  pl.BlockSpec((1, 2, tq, tq), lambda b, h, i: (h, 0, 0, 0)),
                  _const_spec(wl.shape), _const_spec(sg.shape)],
        out_specs=pl.BlockSpec((tq, LANES), lambda b, h, i: (b * nq + i, h)),
        out_shape=jax.ShapeDtypeStruct((n, WIDTH_A), BF16),
        scratch_shapes=[pltpu.VMEM((2 * tq, LANES), F32), pltpu.VMEM((2 * tq, 2 * LANES), F32)],
        compiler_params=_cparams(("arbitrary", "arbitrary", "arbitrary")),
        name="da_prompt",
    )(q, kt, v, bias, wl, sg)


def _sb_prompt_kernel(q_ref, kt_ref, v_ref, u_ref, o_ref, carry_sc, acc_sc, *, tq):
    qi = pl.program_id(2)
    qq = _split_lanes(q_ref[...])
    carry_sc[...] = jnp.zeros(carry_sc.shape, F32)
    acc_sc[...] = jnp.zeros(acc_sc.shape, F32)

    def block(ki, valid):
        vb = v_ref[pl.ds(pl.multiple_of(ki * tq, tq), tq), :]
        z = jnp.dot(qq, kt_ref[ki], preferred_element_type=F32)
        a = _stick_weights(z, u_ref[...], carry_sc, valid)
        acc_sc[...] = acc_sc[...] + jnp.dot(a, vb, preferred_element_type=F32)
        return jnp.max(carry_sc[...])

    row = lax.broadcasted_iota(jnp.int32, (2, tq, tq), 1).reshape(2 * tq, tq)
    col = lax.broadcasted_iota(jnp.int32, (2 * tq, tq), 1)
    top = block(qi, col < row)

    def more(c):
        return jnp.logical_and(c[0] < qi, c[1] > STICK_UNDERFLOW)

    def left(c):
        return c[0] + 1, block(qi - 1 - c[0], None)

    lax.while_loop(more, left, (jnp.int32(0), top))

    acc = acc_sc[...]
    lane = lax.broadcasted_iota(jnp.int32, (tq, LANES), 1)
    o_ref[...] = jnp.where(lane < LANES // 2, acc[:tq], acc[tq:]).astype(o_ref.dtype)


def _sb_prompt(q, kt, v, u_hi_lo, batch, seq):
    n = q.shape[0]
    tq = kt.shape[2]
    nq = seq // tq
    pairs = WIDTH_B // LANES
    kern = functools.partial(_sb_prompt_kernel, tq=tq)
    return pl.pallas_call(
        kern,
        grid=(batch, pairs, nq),
        in_specs=[pl.BlockSpec((tq, LANES), lambda b, h, i: (b * nq + i, h)),
                  pl.BlockSpec((nq, LANES, tq), lambda b, h, i: (b, h, 0)),
                  pl.BlockSpec((seq, LANES), lambda b, h, i: (b, h)),
                  _const_spec(u_hi_lo.shape)],
        out_specs=pl.BlockSpec((tq, LANES), lambda b, h, i: (b * nq + i, h)),
        out_shape=jax.ShapeDtypeStruct((n, WIDTH_B), BF16),
        scratch_shapes=[pltpu.VMEM((2 * tq, LANES), F32), pltpu.VMEM((2 * tq, LANES), F32)],
        compiler_params=_cparams(("arbitrary", "arbitrary", "arbitrary")),
        name="sb_prompt",
    )(q, kt, v, u_hi_lo)


def _block_diag_queries(q, groups, width):
    t = q.shape[0]
    rep = jnp.concatenate([q] * groups, axis=0)
    r = lax.broadcasted_iota(jnp.int32, rep.shape, 0)
    c = lax.broadcasted_iota(jnp.int32, rep.shape, 1)
    keep = (c // width) == (r // t)
    return jnp.where(keep, rep, 0.0).astype(BF16)


def _pad_rows(x, rows):
    return jnp.concatenate([x, jnp.zeros((rows - x.shape[0], x.shape[1]), x.dtype)], axis=0)


def _da_sample_kernel(pt_ref, q_ref, knew_ref, vnew_ref, blast_ref, bnew_ref, wl_ref, sg_ref, *rest,
                      pages, page, lam_init):
    del pt_ref
    kt_refs, v_refs = rest[:pages], rest[pages:2 * pages]
    o_ref, qbd_sc, m_sc, l_sc, acc_sc = rest[2 * pages:]
    g = pl.program_id(1)
    last = pl.num_programs(1) - 1
    t = q_ref.shape[0]
    rows_h = 2 * t

    @pl.when(g == 0)
    def _():
        qbd_sc[...] = _block_diag_queries(q_ref[...], 2 * N_HEADS_A, HEAD_DIM_A)
        m_sc[...] = jnp.full(m_sc.shape, NEG_BIG, F32)
        l_sc[...] = jnp.zeros(l_sc.shape, F32)
        acc_sc[...] = jnp.zeros(acc_sc.shape, F32)

    qbd = qbd_sc[...]
    ktcat = jnp.concatenate([r[0, 0].astype(BF16) for r in kt_refs], axis=1)
    s = jnp.dot(qbd, ktcat, preferred_element_type=F32)
    s = s + blast_ref[...] * (g == last).astype(F32)
    alpha, p = _softmax_probs(s, m_sc, l_sc)
    for h in range(N_HEADS_A):
        vh = jnp.concatenate([r[0, 0, pl.ds(h, page, stride=N_HEADS_A), :] for r in v_refs], axis=0)
        rs = slice(h * rows_h, (h + 1) * rows_h)
        acc_sc[rs, :] = alpha[rs] * acc_sc[rs, :] + jnp.dot(p[rs], vh.astype(BF16), preferred_element_type=F32)

    @pl.when(g == last)
    def _():
        kn = _pad_rows(knew_ref[...], page).astype(BF16)
        vn = _pad_rows(vnew_ref[...], page).astype(BF16)
        s2 = lax.dot_general(qbd, kn, _NT, preferred_element_type=F32) + bnew_ref[...]
        alpha2, p2 = _softmax_probs(s2, m_sc, l_sc)
        o_all = jnp.dot(p2, vn, preferred_element_type=F32)
        inv_l = 1.0 / l_sc[...]
        lam = _lambda_value(wl_ref[...], lam_init)
        outs = []
        for h in range(N_HEADS_A):
            rs = slice(h * rows_h, (h + 1) * rows_h)
            cols = slice(h * 2 * HEAD_DIM_A, (h + 1) * 2 * HEAD_DIM_A)
            o = (alpha2[rs] * acc_sc[rs, :] + o_all[rs, cols]) * inv_l[rs]
            outs.append(_sub_norm(o[:t] - lam * o[t:], sg_ref[...], lam_init))
        o_ref[...] = jnp.concatenate(outs, axis=1)


def _page_specs(layer, n_pages, pages, block, reverse):
    specs = []
    for j in range(pages):
        if reverse:
            idx = lambda b, g, pt, j=j: (layer, pt[b * n_pages + n_pages - (g + 1) * pages + j], 0, 0)
        else:
            idx = lambda b, g, pt, j=j: (layer, pt[b * n_pages + g * pages + j], 0, 0)
        specs.append(pl.BlockSpec((1, 1) + block, idx))
    return specs


def _da_sample(pt, q, knew, vnew, blast, bnew, wl, sg, cache_kt, cache_vi, layer, dec_b, dec_t, lam_init):
    n_pages = pt.shape[0] // dec_b
    page = cache_kt.shape[3]
    pages = min(PAGES_PER_STEP, n_pages)
    rows = 2 * N_HEADS_A * dec_t
    tok = lambda w: pl.BlockSpec((dec_t, w), lambda b, g, pt: (b, 0))
    const = lambda a: pl.BlockSpec(a.shape, lambda b, g, pt: (0,) * a.ndim)
    kern = functools.partial(_da_sample_kernel, pages=pages, page=page, lam_init=lam_init)
    grid_spec = pltpu.PrefetchScalarGridSpec(
        num_scalar_prefetch=1,
        grid=(dec_b, n_pages // pages),
        in_specs=[tok(WIDTH_A), tok(WIDTH_A), tok(WIDTH_A), const(blast), const(bnew), const(wl), const(sg)]
        + _page_specs(layer, n_pages, pages, (WIDTH_A, page), False)
        + _page_specs(layer, n_pages, pages, (page * N_HEADS_A, 2 * HEAD_DIM_A), False),
        out_specs=tok(WIDTH_A),
        scratch_shapes=[pltpu.VMEM((rows, WIDTH_A), BF16), pltpu.VMEM((rows, 1), F32),
                        pltpu.VMEM((rows, 1), F32), pltpu.VMEM((rows, 2 * HEAD_DIM_A), F32)],
    )
    return pl.pallas_call(
        kern,
        grid_spec=grid_spec,
        out_shape=jax.ShapeDtypeStruct((dec_b * dec_t, WIDTH_A), F32),
        compiler_params=_cparams(("arbitrary", "arbitrary")),
        name="da_sample",
    )(pt, q, knew, vnew, blast, bnew, wl, sg, *([cache_kt] * pages), *([cache_vi] * pages))


def _sb_sweep_pages(qbd, kt_refs, vt_refs, u_ref, carry_ref, acc_ref):
    blk = u_ref.shape[1]
    ktcat = jnp.concatenate([r[0, 0].astype(BF16) for r in kt_refs], axis=1)
    vtcat = jnp.concatenate([r[0, 0].astype(BF16) for r in vt_refs], axis=1)
    z = jnp.dot(qbd, ktcat, preferred_element_type=F32)
    for b in reversed(range(ktcat.shape[1] // blk)):
        sl = slice(b * blk, (b + 1) * blk)
        a = _stick_weights(z[:, sl], u_ref[...], carry_ref, None)
        acc_ref[...] = acc_ref[...] + lax.dot_general(a, vtcat[:, sl], _NT, preferred_element_type=F32)


def _sb_near_kernel(pt_ref, q_ref, knew_ref, vnew_ref, u_ref, *rest, pages, page):
    del pt_ref
    kt_refs, vt_refs = rest[:pages], rest[pages:2 * pages]
    acc_ref, carry_ref, o_ref = rest[2 * pages:]
    t = q_ref.shape[0]
    qbd = _block_diag_queries(q_ref[...], N_HEADS_B, HEAD_DIM_B)
    carry_ref[...] = jnp.zeros(carry_ref.shape, F32)
    kn = _pad_rows(knew_ref[...], page).astype(BF16)
    vn = _pad_rows(vnew_ref[...], page).astype(BF16)
    z = lax.dot_general(qbd, kn, _NT, preferred_element_type=F32)
    row_t = lax.broadcasted_iota(jnp.int32, z.shape, 0) % t
    col = lax.broadcasted_iota(jnp.int32, z.shape, 1)
    u_page = jnp.concatenate([u_ref[0:page, 0:page], u_ref[0:page, 0:page]], axis=0)
    a = _stick_weights(z, u_page, carry_ref, col < row_t)
    acc_ref[...] = jnp.dot(a, vn, preferred_element_type=F32)
    _sb_sweep_pages(qbd, kt_refs, vt_refs, u_ref, carry_ref, acc_ref)
    o_ref[...] = _own_head_columns(acc_ref[...], t)


def _own_head_columns(acc, t):
    col_h = lax.broadcasted_iota(jnp.int32, (t, WIDTH_B), 1) // HEAD_DIM_B
    out = jnp.zeros((t, WIDTH_B), F32)
    for h in range(N_HEADS_B):
        out = out + jnp.where(col_h == h, acc[h * t:(h + 1) * t, :], 0.0)
    return out


def _sb_far_kernel(pt_ref, need_ref, q_ref, acc_in_ref, carry_in_ref, u_ref, *rest, pages):
    del pt_ref
    kt_refs, vt_refs = rest[:pages], rest[pages:2 * pages]
    o_ref, qbd_sc, carry_sc, acc_sc = rest[2 * pages:]
    b = pl.program_id(0)
    g = pl.program_id(1)
    last = pl.num_programs(1) - 1
    t = q_ref.shape[0]

    @pl.when(g == 0)
    def _():
        qbd_sc[...] = _block_diag_queries(q_ref[...], N_HEADS_B, HEAD_DIM_B)
        carry_sc[...] = carry_in_ref[...]
        acc_sc[...] = acc_in_ref[...]

    @pl.when(need_ref[b] > 0)
    def _():
        _sb_sweep_pages(qbd_sc[...], kt_refs, vt_refs, u_ref, carry_sc, acc_sc)

    @pl.when(g == last)
    def _():
        o_ref[...] = _own_head_columns(acc_sc[...], t)


def _sb_sample(pt, q, knew, vnew, u_hi_lo, cache_kt, cache_vt, layer, dec_b, dec_t):
    n_pages = pt.shape[0] // dec_b
    page = cache_kt.shape[3]
    near = min(SB_NEAR_PAGES, n_pages)
    n_far = n_pages - near
    assert n_far > 0, "the far sweep also does the final head extraction"
    far = max(p for p in range(1, SB_FAR_PAGES + 1) if n_far % p == 0)
    rows = N_HEADS_B * dec_t
    blk = (WIDTH_B, page)

    tok1 = lambda w: pl.BlockSpec((dec_t, w), lambda b, pt: (b, 0))
    state1 = lambda w: pl.BlockSpec((rows, w), lambda b, pt: (b, 0))
    near_specs = [pl.BlockSpec((1, 1) + blk, lambda b, pt, j=j: (layer, pt[b * n_pages + n_far + j], 0, 0))
                  for j in range(near)]
    acc, carry, o_near = pl.pallas_call(
        functools.partial(_sb_near_kernel, pages=near, page=page),
        grid_spec=pltpu.PrefetchScalarGridSpec(
            num_scalar_prefetch=1,
            grid=(dec_b,),
            in_specs=[tok1(WIDTH_B), tok1(WIDTH_B), tok1(WIDTH_B),
                      pl.BlockSpec(u_hi_lo.shape, lambda b, pt: (0, 0))] + near_specs + near_specs,
            out_specs=[state1(WIDTH_B), state1(LANES), tok1(WIDTH_B)],
        ),
        out_shape=[jax.ShapeDtypeStruct((dec_b * rows, WIDTH_B), F32),
                   jax.ShapeDtypeStruct((dec_b * rows, LANES), F32),
                   jax.ShapeDtypeStruct((dec_b * dec_t, WIDTH_B), F32)],
        compiler_params=_cparams(("arbitrary",)),
        name="sb_sample_near",
    )(pt, q, knew, vnew, u_hi_lo, *([cache_kt] * near), *([cache_vt] * near))

    need = (jnp.max(carry.reshape(dec_b, rows * LANES), axis=1) > STICK_UNDERFLOW).astype(jnp.int32)
    tok2 = lambda w: pl.BlockSpec((dec_t, w), lambda b, g, pt, nd: (b, 0))
    state2 = lambda w: pl.BlockSpec((rows, w), lambda b, g, pt, nd: (b, 0))

    def far_spec(j):
        def idx(b, g, pt, nd):
            return (layer, jnp.where(nd[b] > 0, pt[b * n_pages + n_far - (g + 1) * far + j], 0), 0, 0)
        return pl.BlockSpec((1, 1) + blk, idx)

    far_specs = [far_spec(j) for j in range(far)]

    def sweep_far():
        return pl.pallas_call(
            functools.partial(_sb_far_kernel, pages=far),
            grid_spec=pltpu.PrefetchScalarGridSpec(
                num_scalar_prefetch=2,
                grid=(dec_b, n_far // far),
                in_specs=[tok2(WIDTH_B), state2(WIDTH_B), state2(LANES),
                          pl.BlockSpec(u_hi_lo.shape, lambda b, g, pt, nd: (0, 0))] + far_specs + far_specs,
                out_specs=tok2(WIDTH_B),
                scratch_shapes=[pltpu.VMEM((rows, WIDTH_B), BF16), pltpu.VMEM((rows, LANES), F32),
                                pltpu.VMEM((rows, WIDTH_B), F32)],
            ),
            out_shape=jax.ShapeDtypeStruct((dec_b * dec_t, WIDTH_B), F32),
            compiler_params=_cparams(("arbitrary", "arbitrary")),
            name="sb_sample_far",
        )(pt, need, q, acc, carry, u_hi_lo, *([cache_kt] * far), *([cache_vt] * far))

    return lax.cond(jnp.any(need > 0), sweep_far, lambda: o_near)


def _merge_out_kernel(x_ref, oa_ref, ob_ref, gates_ref, wpa_ref, wpb_ref, wo_ref, o_ref):
    d = x_ref.shape[1]
    ga = jax.nn.sigmoid(gates_ref[:, :d])
    gb = jax.nn.sigmoid(gates_ref[:, d:])
    pa = jnp.dot(oa_ref[...].astype(BF16), wpa_ref[...], preferred_element_type=F32)
    pb = jnp.dot(ob_ref[...].astype(BF16), wpb_ref[...], preferred_element_type=F32)
    merged = ga * pa + gb * pb
    o_ref[...] = x_ref[...] + jnp.dot(merged.astype(BF16), wo_ref[...], preferred_element_type=F32)


def _merge_out(x, oa, ob, gates, wpa, wpb, wo):
    n, d = x.shape
    tm = min(TOKEN_TILE, n)
    row = lambda w: pl.BlockSpec((tm, w), lambda i: (i, 0))
    return pl.pallas_call(
        _merge_out_kernel,
        grid=(n // tm,),
        in_specs=[row(d), row(WIDTH_A), row(WIDTH_B), row(2 * d),
                  _const_spec(wpa.shape), _const_spec(wpb.shape), _const_spec(wo.shape)],
        out_specs=row(d),
        out_shape=jax.ShapeDtypeStruct((n, d), F32),
        compiler_params=_cparams(("arbitrary",)),
        name="merge_out",
    )(x, oa, ob, gates, wpa, wpb, wo)


def _gelu_exact(x):
    return 0.5 * x * (1.0 + lax.erf(x * np.float32(math.sqrt(0.5))))


def _ffn_up(x_ref, g_ref, wup_ref, ff):
    x = x_ref[...]
    n2 = _rms_rows(x, g_ref[...]).astype(BF16)
    a = jnp.dot(n2, wup_ref[:, :ff], preferred_element_type=F32)
    b = jnp.dot(n2, wup_ref[:, ff:], preferred_element_type=F32)
    return x, a, b


def _ffn_down(x, a_m1, a_m2, a, b, cw_ref, cb_ref, wdn_ref):
    conv = cb_ref[...] + a_m2 * cw_ref[0:1, :] + a_m1 * cw_ref[1:2, :] + a * cw_ref[2:3, :]
    h = (_gelu_exact(conv) * b).astype(BF16)
    return x + jnp.dot(h, wdn_ref[...], preferred_element_type=F32)


def _ffn_prompt_kernel(x_ref, g_ref, wup_ref, cw_ref, cb_ref, wdn_ref, y_ref, tail_ref, carry_sc,
                       *, tiles_per_seq):
    ff = wdn_ref.shape[0]
    tm = x_ref.shape[0]
    x, a, b = _ffn_up(x_ref, g_ref, wup_ref, ff)
    first = (pl.program_id(0) % tiles_per_seq) == 0
    prev = jnp.where(first, 0.0, carry_sc[...])
    p6 = prev[SUBLANES - 2:SUBLANES - 1, :]
    p7 = prev[SUBLANES - 1:SUBLANES, :]
    row = lax.broadcasted_iota(jnp.int32, a.shape, 0)
    a_m1 = jnp.where(row == 0, p7, pltpu.roll(a, 1, 0))
    a_m2 = jnp.where(row == 0, p6, jnp.where(row == 1, p7, pltpu.roll(a, 2, 0)))
    last_rows = a[tm - SUBLANES:, :]
    carry_sc[...] = last_rows
    tail_ref[0] = last_rows
    y_ref[...] = _ffn_down(x, a_m1, a_m2, a, b, cw_ref, cb_ref, wdn_ref)


def _ffn_sample_kernel(x_ref, g_ref, wup_ref, cw_ref, cb_ref, wdn_ref, s1_ref, s2_ref, y_ref, a_ref,
                       *, dec_t):
    ff = wdn_ref.shape[0]
    x, a, b = _ffn_up(x_ref, g_ref, wup_ref, ff)
    t = lax.broadcasted_iota(jnp.int32, a.shape, 0) % dec_t
    a_m1 = jnp.where(t >= 1, pltpu.roll(a, 1, 0), s1_ref[...])
    a_m2 = jnp.where(t >= 2, pltpu.roll(a, 2, 0), s2_ref[...])
    a_ref[...] = a
    y_ref[...] = _ffn_down(x, a_m1, a_m2, a, b, cw_ref, cb_ref, wdn_ref)


def _ffn_prompt(x, g, wup, cw, cb, wdn, batch, seq):
    n, d = x.shape
    ff = wdn.shape[0]
    tm = min(TOKEN_TILE, seq)
    tiles_per_seq = seq // tm
    row = lambda w: pl.BlockSpec((tm, w), lambda i: (i, 0))
    kern = functools.partial(_ffn_prompt_kernel, tiles_per_seq=tiles_per_seq)
    return pl.pallas_call(
        kern,
        grid=(n // tm,),
        in_specs=[row(d), _const_spec(g.shape), _const_spec(wup.shape), _const_spec(cw.shape),
                  _const_spec(cb.shape), _const_spec(wdn.shape)],
        out_specs=[row(d), pl.BlockSpec((1, SUBLANES, ff), lambda i: (i // tiles_per_seq, 0, 0))],
        out_shape=[jax.ShapeDtypeStruct((n, d), F32), jax.ShapeDtypeStruct((batch, SUBLANES, ff), F32)],
        scratch_shapes=[pltpu.VMEM((SUBLANES, ff), F32)],
        compiler_params=_cparams(("arbitrary",)),
        name="ffn_prompt",
    )(x, g, wup, cw, cb, wdn)


def _ffn_sample(x, g, wup, cw, cb, wdn, s1, s2, dec_t):
    n, d = x.shape
    ff = wdn.shape[0]
    tm = min(TOKEN_TILE, n)
    row = lambda w: pl.BlockSpec((tm, w), lambda i: (i, 0))
    kern = functools.partial(_ffn_sample_kernel, dec_t=dec_t)
    return pl.pallas_call(
        kern,
        grid=(n // tm,),
        in_specs=[row(d), _const_spec(g.shape), _const_spec(wup.shape), _const_spec(cw.shape),
                  _const_spec(cb.shape), _const_spec(wdn.shape), row(ff), row(ff)],
        out_specs=[row(d), row(ff)],
        out_shape=[jax.ShapeDtypeStruct((n, d), F32), jax.ShapeDtypeStruct((n, ff), F32)],
        compiler_params=_cparams(("arbitrary",)),
        name="ffn_sample",
    )(x, g, wup, cw, cb, wdn, s1, s2)


def _bucket_table(max_dist):
    n = np.arange(max_dist + 1, dtype=np.int32)
    max_exact = N_BUCKETS // 2
    nf = np.maximum(n, 1).astype(np.float32)
    large = max_exact + (np.log(nf / np.float32(max_exact)) / np.float32(math.log(MAX_DISTANCE / max_exact))
                         * np.float32(N_BUCKETS - max_exact)).astype(np.int32)
    large = np.minimum(large, N_BUCKETS - 1)
    return np.where(n < max_exact, n, large).astype(np.int32)


def _bias_kernel(rs_ref, prompt_ref, last_ref, new_ref, *, upper, tq, dec_t, page):
    h = pl.program_id(0)

    def lookup(dist):
        val = jnp.zeros(dist.shape, F32) + rs_ref[N_BUCKETS - 1, h]
        for b in range(N_BUCKETS - 2, -1, -1):
            val = jnp.where(dist <= upper[b], rs_ref[b, h], val)
        return jnp.where(dist >= 0, val, NEG_BIG)

    ql = lax.broadcasted_iota(jnp.int32, (tq, tq), 0)
    kl = lax.broadcasted_iota(jnp.int32, (tq, tq), 1)
    prompt_ref[0, 0] = lookup(ql - kl + tq)
    prompt_ref[0, 1] = lookup(ql - kl)
    t = lax.broadcasted_iota(jnp.int32, (2 * dec_t, page), 0) % dec_t
    j = lax.broadcasted_iota(jnp.int32, (2 * dec_t, page), 1)
    last_ref[...] = lookup(page + t - j)
    new_ref[...] = lookup(t - j)


def _bias_tiles(rel_shift, tq, dec_t, page):
    table = _bucket_table(2 * MAX_DISTANCE)
    upper = tuple(int(np.nonzero(table <= b)[0].max()) for b in range(N_BUCKETS - 1))
    heads = rel_shift.shape[1]
    kern = functools.partial(_bias_kernel, upper=upper, tq=tq, dec_t=dec_t, page=page)
    return pl.pallas_call(
        kern,
        grid=(heads,),
        in_specs=[pl.BlockSpec(memory_space=pltpu.SMEM)],
        out_specs=[pl.BlockSpec((1, 2, tq, tq), lambda h: (h, 0, 0, 0)),
                   pl.BlockSpec((2 * dec_t, page), lambda h: (h, 0)),
                   pl.BlockSpec((2 * dec_t, page), lambda h: (h, 0))],
        out_shape=[jax.ShapeDtypeStruct((heads, 2, tq, tq), F32),
                   jax.ShapeDtypeStruct((heads * 2 * dec_t, page), F32),
                   jax.ShapeDtypeStruct((heads * 2 * dec_t, page), F32)],
        compiler_params=_cparams(("arbitrary",)),
        name="bias_tiles",
    )(rel_shift)


def _lower_triangles(w):
    j = np.arange(w)[:, None]
    s = np.arange(w)[None, :]
    u = -(j > s).astype(np.float32)
    return jnp.asarray(np.concatenate([u, u], axis=0), dtype=BF16)


def _group_mean_matrix():
    i = np.arange(MXU_EDGE)
    m = (i[:, None] // HEAD_DIM_A == i[None, :] // HEAD_DIM_A).astype(np.float32) / HEAD_DIM_A
    return jnp.asarray(m, dtype=BF16)


def kernel(x_prompt, x_sample, cache_da_k, cache_da_v, cache_sb_k, cache_sb_v, state_conv, page_table,
           rel_bias, attn_norm_g, w_in, q_norm_g, k_norm_g, w_lambda, subln_g, w_proj_a, w_proj_b, w_out,
           ffn_norm_g, w_up, conv_w, conv_b, w_down):
    batch, seq, d = x_prompt.shape
    dec_b, dec_t, _ = x_sample.shape
    depth = w_in.shape[0]
    n_pool, page = cache_da_k.shape[1], cache_da_k.shape[2]
    n_pages = page_table.shape[1]
    past_len = n_pages * page
    ff = w_down.shape[1]

    pt = page_table.reshape(-1).astype(jnp.int32)
    cda_kt = jnp.swapaxes(cache_da_k.reshape(depth, n_pool, page, WIDTH_A), 2, 3)
    cda_vi = cache_da_v.reshape(depth, n_pool, page * N_HEADS_A, 2 * HEAD_DIM_A)
    csb_kt = jnp.swapaxes(cache_sb_k.reshape(depth, n_pool, page, WIDTH_B), 2, 3)
    csb_vt = jnp.swapaxes(cache_sb_v.reshape(depth, n_pool, page, WIDTH_B), 2, 3)

    rel_shift = ((rel_bias - rel_bias[N_BUCKETS - 1]) * LOG2E).astype(F32)
    pages = min(PAGES_PER_STEP, n_pages)
    bias_prompt, blast, bnew = _bias_tiles(rel_shift, min(DA_TILE, seq), dec_t, page)
    blast = jnp.pad(blast, ((0, 0), ((pages - 1) * page, 0)))

    u_hi_lo = _lower_triangles(min(SB_TILE, seq))
    gm = _group_mean_matrix()

    c_qa, c_ka, c_va = 0, WIDTH_A, 2 * WIDTH_A
    c_qb, c_kb, c_vb, c_g = 3 * WIDTH_A, 3 * WIDTH_A + WIDTH_B, 3 * WIDTH_A + 2 * WIDTH_B, 3 * WIDTH_A + 3 * WIDTH_B

    xp = x_prompt.reshape(batch * seq, d)
    xs = x_sample.reshape(dec_b * dec_t, d)
    kv_prompt, tails_p = None, []
    rows_s = [[] for _ in range(5)]
    for l in range(depth):
        lam_init = LAMBDA_BASE - LAMBDA_SCALE * math.exp(-LAMBDA_RATE * l)
        g_attn = attn_norm_g[l].reshape(1, d)
        wl_in = w_in[l].astype(BF16)
        w_std = jnp.concatenate([wl_in[:, c_qa:c_ka], wl_in[:, c_va:c_qb], wl_in[:, c_qb:c_kb],
                                 wl_in[:, c_vb:c_g], wl_in[:, c_g:]], axis=1)
        w_keys = jnp.concatenate([wl_in[:, c_ka:c_va], wl_in[:, c_kb:c_vb]], axis=1)
        w_t = jnp.concatenate([w_keys, wl_in[:, c_vb:c_g]], axis=1).T
        qg_t = jnp.tile(q_norm_g[l], WIDTH_A // HEAD_DIM_A).reshape(1, WIDTH_A)
        kg_t = jnp.tile(k_norm_g[l], WIDTH_A // HEAD_DIM_A).reshape(1, WIDTH_A)
        kg_col = k_norm_g[l].reshape(HEAD_DIM_A, 1)
        wl = w_lambda[l]
        sg = subln_g[l].reshape(1, 2 * HEAD_DIM_A)
        wpa, wpb, wo = w_proj_a[l].astype(BF16), w_proj_b[l].astype(BF16), w_out[l].astype(BF16)
        g_ffn = ffn_norm_g[l].reshape(1, d)
        wup, wdn = w_up[l].astype(BF16), w_down[l].astype(BF16)
        cw, cb = conv_w[l], conv_b[l].reshape(1, ff)

        (qa, kat, katb, vai, vab, qb, kbt, kbtb, vbt, vbb, gates) = _in_proj_prompt(
            xp, g_attn, w_std, w_t, qg_t, kg_col, gm, batch, seq, kv_prompt)
        kv_prompt = (kat, vai, kbt, vbt)
        oa = _da_prompt(qa, katb, vab, bias_prompt, wl, sg, batch, seq, lam_init)
        ob = _sb_prompt(qb, kbtb, vbb, u_hi_lo, batch, seq)
        xp = _merge_out(xp, oa, ob, gates, wpa, wpb, wo)
        xp, tail = _ffn_prompt(xp, g_ffn, wup, cw, cb, wdn, batch, seq)
        tails_p.append(tail[:, SUBLANES - (CONV_WIDTH - 1):, :])

        qa, ka, va, qb, kb, vb, gates = _in_proj_sample(xs, g_attn, w_std, w_keys, qg_t, kg_t, gm)
        oa = _da_sample(pt, qa, ka, va, blast, bnew, wl, sg, cda_kt, cda_vi, l, dec_b, dec_t, lam_init)
        ob = _sb_sample(pt, qb, kb, vb, u_hi_lo, csb_kt, csb_vt, l, dec_b, dec_t)
        xs = _merge_out(xs, oa, ob, gates, wpa, wpb, wo)
        st = state_conv[l]
        zeros = jnp.zeros((dec_b, dec_t - 1, ff), F32)
        s1 = jnp.concatenate([st[:, 1:2], zeros], axis=1).reshape(dec_b * dec_t, ff)
        s2 = jnp.concatenate([st, zeros[:, 1:]], axis=1).reshape(dec_b * dec_t, ff)
        xs, a_s = _ffn_sample(xs, g_ffn, wup, cw, cb, wdn, s1, s2, dec_t)
        new_s = (ka.reshape(dec_b, dec_t, N_HEADS_A, 2, HEAD_DIM_A),
                 va.reshape(dec_b, dec_t, N_HEADS_A, 2 * HEAD_DIM_A),
                 kb.reshape(dec_b, dec_t, N_HEADS_B, HEAD_DIM_B),
                 vb.reshape(dec_b, dec_t, N_HEADS_B, HEAD_DIM_B),
                 a_s.reshape(dec_b, dec_t, ff)[:, dec_t - (CONV_WIDTH - 1):, :])
        for lst, r in zip(rows_s, new_s):
            lst.append(r)

    kat, vai, kbt, vbt = kv_prompt
    out_p = [jnp.transpose(kat.reshape(depth, batch, N_HEADS_A, 2, HEAD_DIM_A, seq), (0, 1, 5, 2, 3, 4)),
             vai.reshape(depth, batch, seq, N_HEADS_A, 2 * HEAD_DIM_A),
             jnp.transpose(kbt.reshape(depth, batch, N_HEADS_B, HEAD_DIM_B, seq), (0, 1, 4, 2, 3)),
             jnp.transpose(vbt.reshape(depth, batch, N_HEADS_B, HEAD_DIM_B, seq), (0, 1, 4, 2, 3)),
             jnp.stack(tails_p)]
    out_s = [jnp.stack(r) for r in rows_s]
    return (xp.reshape(batch, seq, d), xs.reshape(dec_b, dec_t, d), *out_p, *out_s)
```

```python
import functools
import math

import numpy as np
import jax
import jax.numpy as jnp
from jax import lax
from jax.experimental import pallas as pl
from jax.experimental.pallas import tpu as pltpu

F32 = jnp.float32
BF16 = jnp.bfloat16

N_HEADS_A = 8
HEAD_DIM_A = 64
WIDTH_A = N_HEADS_A * 2 * HEAD_DIM_A
N_HEADS_B = 8
HEAD_DIM_B = 64
WIDTH_B = N_HEADS_B * HEAD_DIM_B
N_BUCKETS = 32
MAX_DISTANCE = 128
LAMBDA_BASE = 0.8
LAMBDA_SCALE = 0.6
LAMBDA_RATE = 0.3
NORM_EPS = 1e-6
NEG_BIG = -1e30
CONV_WIDTH = 3

LANES = 128
SUBLANES = 8
MXU_EDGE = 256
VMEM_LIMIT_BYTES = 56 * 1024 * 1024

TOKEN_TILE = 256
WIDE_TOKEN_TILE = 512
DA_TILE = 512
SB_TILE = MXU_EDGE
PAGES_PER_STEP = 16
SB_PAIRS_PER_STEP = 4
SB_NEAR_PAGES = 8
SB_FAR_PAGES = 14

LOG2E = math.log2(math.e)
STICK_UNDERFLOW = -104.0


def _cparams(sem):
    return pltpu.CompilerParams(dimension_semantics=sem, vmem_limit_bytes=VMEM_LIMIT_BYTES)


def _const_spec(shape):
    nd = len(shape)
    return pl.BlockSpec(shape, lambda *_: (0,) * nd)


_NT = (((1,), (1,)), ((), ()))


def _rms_rows(x, g):
    return x * lax.rsqrt(jnp.mean(x * x, axis=-1, keepdims=True) + NORM_EPS) * g


def _head_norm(y, gain, gm):
    sq = (y * y).astype(BF16)
    outs = []
    for j in range(WIDTH_A // MXU_EDGE):
        sl = slice(j * MXU_EDGE, (j + 1) * MXU_EDGE)
        ms = jnp.dot(sq[:, sl], gm, preferred_element_type=F32)
        outs.append(y[:, sl] * lax.rsqrt(ms + NORM_EPS))
    return jnp.concatenate(outs, axis=1) * gain


def _in_proj_prompt_kernel(x_ref, g_ref, ws_ref, wt_ref, qg_ref, kgc_ref, gm_ref, *refs, n_prev):
    prev_refs, refs = refs[:4 if n_prev else 0], refs[4 if n_prev else 0:]
    qa_ref, kat_ref, katb_ref, vai_ref, vab_ref, qb_ref, kbt_ref, kbtb_ref, vbt_ref, vbb_ref, gates_ref = refs
    for dst, src in zip((kat_ref, vai_ref, kbt_ref, vbt_ref), prev_refs):
        dst[0:n_prev] = src[...]
    tm = x_ref.shape[0]
    nb = _rms_rows(x_ref[...], g_ref[...]).astype(BF16)

    def seg(lo, width):
        return jnp.dot(nb, ws_ref[:, lo:lo + width], preferred_element_type=F32)

    def seg_t(lo, width):
        return lax.dot_general(wt_ref[lo:lo + width, :], nb, _NT, preferred_element_type=F32)

    qa = _head_norm(seg(0, WIDTH_A), qg_ref[...], gm_ref[...]) * (HEAD_DIM_A ** -0.5 * LOG2E)
    qa_ref[...] = qa.astype(BF16)
    va = seg(WIDTH_A, WIDTH_A)
    vab_ref[...] = va.astype(BF16)
    for h in range(N_HEADS_A):
        vai_ref[n_prev, pl.ds(h, tm, stride=N_HEADS_A), :] = va[:, h * LANES:(h + 1) * LANES]
    off = 2 * WIDTH_A
    qb_ref[...] = (seg(off, WIDTH_B) * (HEAD_DIM_B ** -0.5)).astype(BF16)
    vbb_ref[...] = seg(off + WIDTH_B, WIDTH_B).astype(BF16)
    gates_ref[...] = seg(off + 2 * WIDTH_B, gates_ref.shape[1])

    kat = seg_t(0, WIDTH_A).reshape(WIDTH_A // HEAD_DIM_A, HEAD_DIM_A, tm)
    ms = jnp.mean(kat * kat, axis=1, keepdims=True)
    kat = (kat * lax.rsqrt(ms + NORM_EPS) * kgc_ref[...].reshape(1, HEAD_DIM_A, 1)).reshape(WIDTH_A, tm)
    kat_ref[n_prev, 0] = kat
    katb_ref[0] = kat.astype(BF16)
    kbt = seg_t(WIDTH_A, WIDTH_B)
    kbt_ref[n_prev, 0] = kbt
    kbtb_ref[0] = kbt.astype(BF16)
    vbt_ref[n_prev, 0] = seg_t(WIDTH_A + WIDTH_B, WIDTH_B)


def _in_proj_prompt(x, g, ws, wt, qg_t, kg_col, gm, batch, seq, prev):
    n, d = x.shape
    tm = min(TOKEN_TILE, seq)
    tps = seq // tm
    n_prev = 0 if prev is None else prev[0].shape[0]
    gate_w = ws.shape[1] - 2 * WIDTH_A - 2 * WIDTH_B
    row = lambda w: pl.BlockSpec((tm, w), lambda i: (i, 0))
    t_blk = lambda layers, w: pl.BlockSpec((layers, 1, w, tm), lambda i: (0, i // tps, 0, i % tps))
    vi_blk = lambda layers: pl.BlockSpec((layers, tm * N_HEADS_A, LANES), lambda i: (0, i, 0))
    t_out = lambda w: t_blk(n_prev + 1, w)
    prev_specs = [] if prev is None else [t_blk(n_prev, WIDTH_A), vi_blk(n_prev),
                                          t_blk(n_prev, WIDTH_B), t_blk(n_prev, WIDTH_B)]
    da_tile, sb_tile = min(DA_TILE, seq), min(SB_TILE, seq)
    t_tile = lambda w, kt: pl.BlockSpec((1, w, tm), lambda i: (i // (kt // tm), 0, i % (kt // tm)))
    out_specs = [row(WIDTH_A), t_out(WIDTH_A), t_tile(WIDTH_A, da_tile),
                 vi_blk(n_prev + 1), row(WIDTH_A),
                 row(WIDTH_B), t_out(WIDTH_B), t_tile(WIDTH_B, sb_tile), t_out(WIDTH_B), row(WIDTH_B), row(gate_w)]
    layers = n_prev + 1
    out_shape = [jax.ShapeDtypeStruct((n, WIDTH_A), BF16),
                 jax.ShapeDtypeStruct((layers, batch, WIDTH_A, seq), F32),
                 jax.ShapeDtypeStruct((n // da_tile, WIDTH_A, da_tile), BF16),
                 jax.ShapeDtypeStruct((layers, n * N_HEADS_A, LANES), F32),
                 jax.ShapeDtypeStruct((n, WIDTH_A), BF16),
                 jax.ShapeDtypeStruct((n, WIDTH_B), BF16),
                 jax.ShapeDtypeStruct((layers, batch, WIDTH_B, seq), F32),
                 jax.ShapeDtypeStruct((n // sb_tile, WIDTH_B, sb_tile), BF16),
                 jax.ShapeDtypeStruct((layers, batch, WIDTH_B, seq), F32),
                 jax.ShapeDtypeStruct((n, WIDTH_B), BF16),
                 jax.ShapeDtypeStruct((n, gate_w), F32)]
    return pl.pallas_call(
        functools.partial(_in_proj_prompt_kernel, n_prev=n_prev),
        grid=(n // tm,),
        in_specs=[row(d), _const_spec(g.shape), _const_spec(ws.shape), _const_spec(wt.shape),
                  _const_spec(qg_t.shape), _const_spec(kg_col.shape), _const_spec(gm.shape)] + prev_specs,
        out_specs=out_specs,
        out_shape=out_shape,
        compiler_params=_cparams(("arbitrary",)),
        name="in_proj_prompt",
    )(x, g, ws, wt, qg_t, kg_col, gm, *(prev or ()))


def _in_proj_sample_kernel(x_ref, g_ref, ws_ref, wk_ref, qg_ref, kg_ref, gm_ref,
                           qa_ref, ka_ref, va_ref, qb_ref, kb_ref, vb_ref, gates_ref):
    nb = _rms_rows(x_ref[...], g_ref[...]).astype(BF16)

    def seg(w_ref, lo, width):
        return jnp.dot(nb, w_ref[:, lo:lo + width], preferred_element_type=F32)

    qa_ref[...] = _head_norm(seg(ws_ref, 0, WIDTH_A), qg_ref[...], gm_ref[...]) * (HEAD_DIM_A ** -0.5 * LOG2E)
    va_ref[...] = seg(ws_ref, WIDTH_A, WIDTH_A)
    off = 2 * WIDTH_A
    qb_ref[...] = seg(ws_ref, off, WIDTH_B) * (HEAD_DIM_B ** -0.5)
    vb_ref[...] = seg(ws_ref, off + WIDTH_B, WIDTH_B)
    gates_ref[...] = seg(ws_ref, off + 2 * WIDTH_B, gates_ref.shape[1])
    ka_ref[...] = _head_norm(seg(wk_ref, 0, WIDTH_A), kg_ref[...], gm_ref[...])
    kb_ref[...] = seg(wk_ref, WIDTH_A, WIDTH_B)


def _in_proj_sample(x, g, ws, wk, qg_t, kg_t, gm):
    n, d = x.shape
    tm = min(TOKEN_TILE, n)
    gate_w = ws.shape[1] - 2 * WIDTH_A - 2 * WIDTH_B
    row = lambda w: pl.BlockSpec((tm, w), lambda i: (i, 0))
    widths = [WIDTH_A, WIDTH_A, WIDTH_A, WIDTH_B, WIDTH_B, WIDTH_B, gate_w]
    return pl.pallas_call(
        _in_proj_sample_kernel,
        grid=(n // tm,),
        in_specs=[row(d), _const_spec(g.shape), _const_spec(ws.shape), _const_spec(wk.shape),
                  _const_spec(qg_t.shape), _const_spec(kg_t.shape), _const_spec(gm.shape)],
        out_specs=[row(w) for w in widths],
        out_shape=[jax.ShapeDtypeStruct((n, w), F32) for w in widths],
        compiler_params=_cparams(("arbitrary",)),
        name="in_proj_sample",
    )(x, g, ws, wk, qg_t, kg_t, gm)


def _lambda_value(wl, lam_init):
    a = jnp.sum(wl[0:1] * wl[1:2], axis=1, keepdims=True)
    b = jnp.sum(wl[2:3] * wl[3:4], axis=1, keepdims=True)
    return jnp.exp(a) - jnp.exp(b) + lam_init


def _sub_norm(d, gain, lam_init):
    y = d * lax.rsqrt(jnp.mean(d * d, axis=-1, keepdims=True) + NORM_EPS)
    return y * gain * (1.0 - lam_init)


def _softmax_probs(s, m_sc, l_sc):
    m_prev = m_sc[...]
    m_new = jnp.maximum(m_prev, jnp.max(s, axis=-1, keepdims=True))
    alpha = jnp.exp2(m_prev - m_new)
    p = jnp.exp2(s - m_new)
    l_sc[...] = alpha * l_sc[...] + jnp.sum(p, axis=-1, keepdims=True)
    m_sc[...] = m_new
    return alpha, p.astype(BF16)


def _lane_tile(x, width):
    reps = width // x.shape[1]
    return x if reps == 1 else jnp.concatenate([x] * reps, axis=1)


def _stick_weights(z, u_hi_lo, carry_sc, valid):
    width = z.shape[1]
    w = jnp.maximum(z, 0.0) + jnp.log(1.0 + jnp.exp(-jnp.abs(z)))
    if valid is not None:
        w = jnp.where(valid, w, 0.0)
    hi = w.astype(BF16)
    lo = (w - hi.astype(F32)).astype(BF16)
    later = jnp.dot(jnp.concatenate([hi, lo], axis=1), u_hi_lo, preferred_element_type=F32)
    carry = carry_sc[...]
    a = jnp.exp(z - w + later + _lane_tile(carry, width))
    if valid is not None:
        a = jnp.where(valid, a, 0.0)
    carry_sc[...] = carry - jnp.sum(w, axis=-1, keepdims=True)
    return a.astype(BF16)


def _split_lanes(q):
    lane = lax.broadcasted_iota(jnp.int32, q.shape, 1)
    zero = jnp.zeros_like(q)
    half = LANES // 2
    return jnp.concatenate([jnp.where(lane < half, q, zero), jnp.where(lane >= half, q, zero)], axis=0)


def _da_prompt_kernel(q_ref, kt_ref, v_ref, bias_ref, wl_ref, sg_ref, o_ref, m_sc, acc_sc,
                      *, tq, lam_init):
    qi = pl.program_id(2)
    qq = _split_lanes(q_ref[...])
    m_sc[...] = jnp.full(m_sc.shape, NEG_BIG, F32)
    acc_sc[...] = jnp.zeros(acc_sc.shape, F32)
    ones = jnp.ones((tq, LANES), BF16)

    def block(ki, bias):
        v_ext = jnp.concatenate([v_ref[pl.ds(pl.multiple_of(ki * tq, tq), tq), :], ones], axis=1)
        s = jnp.dot(qq, kt_ref[ki], preferred_element_type=F32)
        if bias is not None:
            s = (s.reshape(2, tq, tq) + bias[None]).reshape(2 * tq, tq)
        m_prev = m_sc[...]
        m_new = jnp.maximum(m_prev, jnp.max(s, axis=-1, keepdims=True))
        alpha = jnp.exp2(m_prev - m_new)
        p = jnp.concatenate([jnp.exp2(s[:, j * LANES:(j + 1) * LANES] - m_new).astype(BF16)
                             for j in range(tq // LANES)], axis=1)
        acc_sc[...] = _lane_tile(alpha, 2 * LANES) * acc_sc[...] + jnp.dot(p, v_ext, preferred_element_type=F32)
        m_sc[...] = m_new

    def far(ki, c):
        block(ki, None)
        return c

    lax.fori_loop(0, qi - 1, far, 0)

    @pl.when(qi >= 1)
    def _():
        block(qi - 1, bias_ref[0, 0])

    block(qi, bias_ref[0, 1])

    acc = acc_sc[...]
    o = acc[:, :LANES] / acc[:, LANES:]
    lam = _lambda_value(wl_ref[...], lam_init)
    d = o[:tq] - lam * o[tq:]
    o_ref[...] = _sub_norm(d, sg_ref[...], lam_init).astype(o_ref.dtype)


def _da_prompt(q, kt, v, bias, wl, sg, batch, seq, lam_init):
    n = q.shape[0]
    tq = kt.shape[2]
    nq = seq // tq
    kern = functools.partial(_da_prompt_kernel, tq=tq, lam_init=lam_init)
    return pl.pallas_call(
        kern,
        grid=(batch, N_HEADS_A, nq),
        in_specs=[pl.BlockSpec((tq, LANES), lambda b, h, i: (b * nq + i, h)),
                  pl.BlockSpec((nq, LANES, tq), lambda b, h, i: (b, h, 0)),
                  pl.BlockSpec((seq, LANES), lambda b, h, i: (b, h)),
                  pl.BlockSpec((1, 2, tq, tq), lambda b, h, i: (h, 0, 0, 0)),
                  _const_spec(wl.shape), _const_spec(sg.shape)],
        out_specs=pl.BlockSpec((tq, LANES), lambda b, h, i: (b * nq + i, h)),
        out_shape=jax.ShapeDtypeStruct((n, WIDTH_A), BF16),
        scratch_shapes=[pltpu.VMEM((2 * tq, LANES), F32), pltpu.VMEM((2 * tq, 2 * LANES), F32)],
        compiler_params=_cparams(("arbitrary", "arbitrary", "arbitrary")),
        name="da_prompt",
    )(q, kt, v, bias, wl, sg)


def _sb_prompt_kernel(q_ref, kt_ref, v_ref, u_ref, o_ref, carry_sc, acc_sc, *, tq, pp):
    qi = pl.program_id(2)
    q = q_ref[...]
    qqs = [_split_lanes(q[:, p * LANES:(p + 1) * LANES]) for p in range(pp)]
    carry_sc[...] = jnp.zeros(carry_sc.shape, F32)
    acc_sc[...] = jnp.zeros(acc_sc.shape, F32)

    def block(ki, valid):
        k0 = pl.multiple_of(ki * tq, tq)
        kt = kt_ref[ki]
        top = None
        for p in range(pp):
            lanes = slice(p * LANES, (p + 1) * LANES)
            z = jnp.dot(qqs[p], kt[lanes, :], preferred_element_type=F32)
            a = _stick_weights(z, u_ref[...], carry_sc.at[p], valid)
            acc_sc[p] = acc_sc[p] + jnp.dot(a, v_ref[pl.ds(k0, tq), lanes], preferred_element_type=F32)
            c = jnp.max(carry_sc[p])
            top = c if top is None else jnp.maximum(top, c)
        return top

    row = lax.broadcasted_iota(jnp.int32, (2, tq, tq), 1).reshape(2 * tq, tq)
    col = lax.broadcasted_iota(jnp.int32, (2 * tq, tq), 1)
    top = block(qi, col < row)

    def more(c):
        return jnp.logical_and(c[0] < qi, c[1] > STICK_UNDERFLOW)

    def left(c):
        return c[0] + 1, block(qi - 1 - c[0], None)

    lax.while_loop(more, left, (jnp.int32(0), top))

    lane = lax.broadcasted_iota(jnp.int32, (tq, LANES), 1)
    for p in range(pp):
        acc = acc_sc[p]
        o_ref[:, p * LANES:(p + 1) * LANES] = jnp.where(lane < LANES // 2, acc[:tq], acc[tq:]).astype(o_ref.dtype)


def _sb_prompt(q, kt, v, u_hi_lo, batch, seq):
    n = q.shape[0]
    tq = kt.shape[2]
    nq = seq // tq
    pp = SB_PAIRS_PER_STEP
    width = pp * LANES
    kern = functools.partial(_sb_prompt_kernel, tq=tq, pp=pp)
    return pl.pallas_call(
        kern,
        grid=(batch, WIDTH_B // width, nq),
        in_specs=[pl.BlockSpec((tq, width), lambda b, h, i: (b * nq + i, h)),
                  pl.BlockSpec((nq, width, tq), lambda b, h, i: (b, h, 0)),
                  pl.BlockSpec((seq, width), lambda b, h, i: (b, h)),
                  _const_spec(u_hi_lo.shape)],
        out_specs=pl.BlockSpec((tq, width), lambda b, h, i: (b * nq + i, h)),
        out_shape=jax.ShapeDtypeStruct((n, WIDTH_B), BF16),
        scratch_shapes=[pltpu.VMEM((pp, 2 * tq, LANES), F32), pltpu.VMEM((pp, 2 * tq, LANES), F32)],
        compiler_params=_cparams(("arbitrary", "arbitrary", "arbitrary")),
        name="sb_prompt",
    )(q, kt, v, u_hi_lo)


def _block_diag_queries(q, groups, width):
    t = q.shape[0]
    rep = jnp.concatenate([q] * groups, axis=0)
    r = lax.broadcasted_iota(jnp.int32, rep.shape, 0)
    c = lax.broadcasted_iota(jnp.int32, rep.shape, 1)
    keep = (c // width) == (r // t)
    return jnp.where(keep, rep, 0.0).astype(BF16)


def _pad_rows(x, rows):
    return jnp.concatenate([x, jnp.zeros((rows - x.shape[0], x.shape[1]), x.dtype)], axis=0)


def _da_sample_kernel(pt_ref, q_ref, knew_ref, vnew_ref, blast_ref, bnew_ref, wl_ref, sg_ref, *rest,
                      pages, page, lam_init):
    del pt_ref
    kt_refs, v_refs = rest[:pages], rest[pages:2 * pages]
    o_ref, qbd_sc, m_sc, l_sc, acc_sc = rest[2 * pages:]
    g = pl.program_id(1)
    last = pl.num_programs(1) - 1
    t = q_ref.shape[0]
    rows_h = 2 * t

    @pl.when(g == 0)
    def _():
        qbd_sc[...] = _block_diag_queries(q_ref[...], 2 * N_HEADS_A, HEAD_DIM_A)
        m_sc[...] = jnp.full(m_sc.shape, NEG_BIG, F32)
        l_sc[...] = jnp.zeros(l_sc.shape, F32)
        acc_sc[...] = jnp.zeros(acc_sc.shape, F32)

    qbd = qbd_sc[...]
    ktcat = jnp.concatenate([r[0, 0].astype(BF16) for r in kt_refs], axis=1)
    s = jnp.dot(qbd, ktcat, preferred_element_type=F32)
    s = s + blast_ref[...] * (g == last).astype(F32)
    alpha, p = _softmax_probs(s, m_sc, l_sc)
    for h in range(N_HEADS_A):
        vh = jnp.concatenate([r[0, 0, pl.ds(h, page, stride=N_HEADS_A), :] for r in v_refs], axis=0)
        rs = slice(h * rows_h, (h + 1) * rows_h)
        acc_sc[rs, :] = alpha[rs] * acc_sc[rs, :] + jnp.dot(p[rs], vh.astype(BF16), preferred_element_type=F32)

    @pl.when(g == last)
    def _():
        kn = _pad_rows(knew_ref[...], page).astype(BF16)
        vn = _pad_rows(vnew_ref[...], page).astype(BF16)
        s2 = lax.dot_general(qbd, kn, _NT, preferred_element_type=F32) + bnew_ref[...]
        alpha2, p2 = _softmax_probs(s2, m_sc, l_sc)
        o_all = jnp.dot(p2, vn, preferred_element_type=F32)
        inv_l = 1.0 / l_sc[...]
        lam = _lambda_value(wl_ref[...], lam_init)
        outs = []
        for h in range(N_HEADS_A):
            rs = slice(h * rows_h, (h + 1) * rows_h)
            cols = slice(h * 2 * HEAD_DIM_A, (h + 1) * 2 * HEAD_DIM_A)
            o = (alpha2[rs] * acc_sc[rs, :] + o_all[rs, cols]) * inv_l[rs]
            outs.append(_sub_norm(o[:t] - lam * o[t:], sg_ref[...], lam_init))
        o_ref[...] = jnp.concatenate(outs, axis=1)


def _page_specs(layer, n_pages, pages, block, reverse):
    specs = []
    for j in range(pages):
        if reverse:
            idx = lambda b, g, pt, j=j: (layer, pt[b * n_pages + n_pages - (g + 1) * pages + j], 0, 0)
        else:
            idx = lambda b, g, pt, j=j: (layer, pt[b * n_pages + g * pages + j], 0, 0)
        specs.append(pl.BlockSpec((1, 1) + block, idx))
    return specs


def _da_sample(pt, q, knew, vnew, blast, bnew, wl, sg, cache_kt, cache_vi, layer, dec_b, dec_t, lam_init):
    n_pages = pt.shape[0] // dec_b
    page = cache_kt.shape[3]
    pages = min(PAGES_PER_STEP, n_pages)
    rows = 2 * N_HEADS_A * dec_t
    tok = lambda w: pl.BlockSpec((dec_t, w), lambda b, g, pt: (b, 0))
    const = lambda a: pl.BlockSpec(a.shape, lambda b, g, pt: (0,) * a.ndim)
    kern = functools.partial(_da_sample_kernel, pages=pages, page=page, lam_init=lam_init)
    grid_spec = pltpu.PrefetchScalarGridSpec(
        num_scalar_prefetch=1,
        grid=(dec_b, n_pages // pages),
        in_specs=[tok(WIDTH_A), tok(WIDTH_A), tok(WIDTH_A), const(blast), const(bnew), const(wl), const(sg)]
        + _page_specs(layer, n_pages, pages, (WIDTH_A, page), False)
        + _page_specs(layer, n_pages, pages, (page * N_HEADS_A, 2 * HEAD_DIM_A), False),
        out_specs=tok(WIDTH_A),
        scratch_shapes=[pltpu.VMEM((rows, WIDTH_A), BF16), pltpu.VMEM((rows, 1), F32),
                        pltpu.VMEM((rows, 1), F32), pltpu.VMEM((rows, 2 * HEAD_DIM_A), F32)],
    )
    return pl.pallas_call(
        kern,
        grid_spec=grid_spec,
        out_shape=jax.ShapeDtypeStruct((dec_b * dec_t, WIDTH_A), F32),
        compiler_params=_cparams(("arbitrary", "arbitrary")),
        name="da_sample",
    )(pt, q, knew, vnew, blast, bnew, wl, sg, *([cache_kt] * pages), *([cache_vi] * pages))


def _sb_sweep_pages(qbd, kt_refs, vt_refs, u_ref, carry_ref, acc_ref):
    blk = u_ref.shape[1]
    ktcat = jnp.concatenate([r[0, 0].astype(BF16) for r in kt_refs], axis=1)
    vtcat = jnp.concatenate([r[0, 0].astype(BF16) for r in vt_refs], axis=1)
    z = jnp.dot(qbd, ktcat, preferred_element_type=F32)
    for b in reversed(range(ktcat.shape[1] // blk)):
        sl = slice(b * blk, (b + 1) * blk)
        a = _stick_weights(z[:, sl], u_ref[...], carry_ref, None)
        acc_ref[...] = acc_ref[...] + lax.dot_general(a, vtcat[:, sl], _NT, preferred_element_type=F32)


def _sb_near_kernel(pt_ref, q_ref, knew_ref, vnew_ref, u_ref, *rest, pages, page):
    del pt_ref
    kt_refs, vt_refs = rest[:pages], rest[pages:2 * pages]
    acc_ref, carry_ref, o_ref = rest[2 * pages:]
    t = q_ref.shape[0]
    qbd = _block_diag_queries(q_ref[...], N_HEADS_B, HEAD_DIM_B)
    carry_ref[...] = jnp.zeros(carry_ref.shape, F32)
    kn = _pad_rows(knew_ref[...], page).astype(BF16)
    vn = _pad_rows(vnew_ref[...], page).astype(BF16)
    z = lax.dot_general(qbd, kn, _NT, preferred_element_type=F32)
    row_t = lax.broadcasted_iota(jnp.int32, z.shape, 0) % t
    col = lax.broadcasted_iota(jnp.int32, z.shape, 1)
    u_page = jnp.concatenate([u_ref[0:page, 0:page], u_ref[0:page, 0:page]], axis=0)
    a = _stick_weights(z, u_page, carry_ref, col < row_t)
    acc_ref[...] = jnp.dot(a, vn, preferred_element_type=F32)
    _sb_sweep_pages(qbd, kt_refs, vt_refs, u_ref, carry_ref, acc_ref)
    o_ref[...] = _own_head_columns(acc_ref[...], t)


def _own_head_columns(acc, t):
    col_h = lax.broadcasted_iota(jnp.int32, (t, WIDTH_B), 1) // HEAD_DIM_B
    out = jnp.zeros((t, WIDTH_B), F32)
    for h in range(N_HEADS_B):
        out = out + jnp.where(col_h == h, acc[h * t:(h + 1) * t, :], 0.0)
    return out


def _sb_far_kernel(pt_ref, need_ref, q_ref, acc_in_ref, carry_in_ref, u_ref, *rest, pages):
    del pt_ref
    kt_refs, vt_refs = rest[:pages], rest[pages:2 * pages]
    o_ref, qbd_sc, carry_sc, acc_sc = rest[2 * pages:]
    b = pl.program_id(0)
    g = pl.program_id(1)
    last = pl.num_programs(1) - 1
    t = q_ref.shape[0]

    @pl.when(g == 0)
    def _():
        qbd_sc[...] = _block_diag_queries(q_ref[...], N_HEADS_B, HEAD_DIM_B)
        carry_sc[...] = carry_in_ref[...]
        acc_sc[...] = acc_in_ref[...]

    @pl.when(need_ref[b] > 0)
    def _():
        _sb_sweep_pages(qbd_sc[...], kt_refs, vt_refs, u_ref, carry_sc, acc_sc)

    @pl.when(g == last)
    def _():
        o_ref[...] = _own_head_columns(acc_sc[...], t)


def _sb_sample(pt, q, knew, vnew, u_hi_lo, cache_kt, cache_vt, layer, dec_b, dec_t):
    n_pages = pt.shape[0] // dec_b
    page = cache_kt.shape[3]
    near = min(SB_NEAR_PAGES, n_pages)
    n_far = n_pages - near
    assert n_far > 0, "the far sweep also does the final head extraction"
    far = max(p for p in range(1, SB_FAR_PAGES + 1) if n_far % p == 0)
    rows = N_HEADS_B * dec_t
    blk = (WIDTH_B, page)

    tok1 = lambda w: pl.BlockSpec((dec_t, w), lambda b, pt: (b, 0))
    state1 = lambda w: pl.BlockSpec((rows, w), lambda b, pt: (b, 0))
    near_specs = [pl.BlockSpec((1, 1) + blk, lambda b, pt, j=j: (layer, pt[b * n_pages + n_far + j], 0, 0))
                  for j in range(near)]
    acc, carry, o_near = pl.pallas_call(
        functools.partial(_sb_near_kernel, pages=near, page=page),
        grid_spec=pltpu.PrefetchScalarGridSpec(
            num_scalar_prefetch=1,
            grid=(dec_b,),
            in_specs=[tok1(WIDTH_B), tok1(WIDTH_B), tok1(WIDTH_B),
                      pl.BlockSpec(u_hi_lo.shape, lambda b, pt: (0, 0))] + near_specs + near_specs,
            out_specs=[state1(WIDTH_B), state1(LANES), tok1(WIDTH_B)],
        ),
        out_shape=[jax.ShapeDtypeStruct((dec_b * rows, WIDTH_B), F32),
                   jax.ShapeDtypeStruct((dec_b * rows, LANES), F32),
                   jax.ShapeDtypeStruct((dec_b * dec_t, WIDTH_B), F32)],
        compiler_params=_cparams(("arbitrary",)),
        name="sb_sample_near",
    )(pt, q, knew, vnew, u_hi_lo, *([cache_kt] * near), *([cache_vt] * near))

    need = (jnp.max(carry.reshape(dec_b, rows * LANES), axis=1) > STICK_UNDERFLOW).astype(jnp.int32)
    tok2 = lambda w: pl.BlockSpec((dec_t, w), lambda b, g, pt, nd: (b, 0))
    state2 = lambda w: pl.BlockSpec((rows, w), lambda b, g, pt, nd: (b, 0))

    def far_spec(j):
        def idx(b, g, pt, nd):
            return (layer, jnp.where(nd[b] > 0, pt[b * n_pages + n_far - (g + 1) * far + j], 0), 0, 0)
        return pl.BlockSpec((1, 1) + blk, idx)

    far_specs = [far_spec(j) for j in range(far)]

    def sweep_far():
        return pl.pallas_call(
            functools.partial(_sb_far_kernel, pages=far),
            grid_spec=pltpu.PrefetchScalarGridSpec(
                num_scalar_prefetch=2,
                grid=(dec_b, n_far // far),
                in_specs=[tok2(WIDTH_B), state2(WIDTH_B), state2(LANES),
                          pl.BlockSpec(u_hi_lo.shape, lambda b, g, pt, nd: (0, 0))] + far_specs + far_specs,
                out_specs=tok2(WIDTH_B),
                scratch_shapes=[pltpu.VMEM((rows, WIDTH_B), BF16), pltpu.VMEM((rows, LANES), F32),
                                pltpu.VMEM((rows, WIDTH_B), F32)],
            ),
            out_shape=jax.ShapeDtypeStruct((dec_b * dec_t, WIDTH_B), F32),
            compiler_params=_cparams(("arbitrary", "arbitrary")),
            name="sb_sample_far",
        )(pt, need, q, acc, carry, u_hi_lo, *([cache_kt] * far), *([cache_vt] * far))

    return lax.cond(jnp.any(need > 0), sweep_far, lambda: o_near)


def _merge_out_kernel(x_ref, oa_ref, ob_ref, gates_ref, wpa_ref, wpb_ref, wo_ref, o_ref):
    d = x_ref.shape[1]
    ga = jax.nn.sigmoid(gates_ref[:, :d])
    gb = jax.nn.sigmoid(gates_ref[:, d:])
    pa = jnp.dot(oa_ref[...].astype(BF16), wpa_ref[...], preferred_element_type=F32)
    pb = jnp.dot(ob_ref[...].astype(BF16), wpb_ref[...], preferred_element_type=F32)
    merged = ga * pa + gb * pb
    o_ref[...] = x_ref[...] + jnp.dot(merged.astype(BF16), wo_ref[...], preferred_element_type=F32)


def _merge_out(x, oa, ob, gates, wpa, wpb, wo):
    n, d = x.shape
    tm = min(WIDE_TOKEN_TILE, n)
    row = lambda w: pl.BlockSpec((tm, w), lambda i: (i, 0))
    return pl.pallas_call(
        _merge_out_kernel,
        grid=(n // tm,),
        in_specs=[row(d), row(WIDTH_A), row(WIDTH_B), row(2 * d),
                  _const_spec(wpa.shape), _const_spec(wpb.shape), _const_spec(wo.shape)],
        out_specs=row(d),
        out_shape=jax.ShapeDtypeStruct((n, d), F32),
        compiler_params=_cparams(("arbitrary",)),
        name="merge_out",
    )(x, oa, ob, gates, wpa, wpb, wo)


def _gelu_exact(x):
    return 0.5 * x * (1.0 + lax.erf(x * np.float32(math.sqrt(0.5))))


def _ffn_up(x_ref, g_ref, wup_ref, ff):
    x = x_ref[...]
    n2 = _rms_rows(x, g_ref[...]).astype(BF16)
    a = jnp.dot(n2, wup_ref[:, :ff], preferred_element_type=F32)
    b = jnp.dot(n2, wup_ref[:, ff:], preferred_element_type=F32)
    return x, a, b


def _ffn_down(x, a_m1, a_m2, a, b, cw_ref, cb_ref, wdn_ref):
    conv = cb_ref[...] + a_m2 * cw_ref[0:1, :] + a_m1 * cw_ref[1:2, :] + a * cw_ref[2:3, :]
    h = (_gelu_exact(conv) * b).astype(BF16)
    return x + jnp.dot(h, wdn_ref[...], preferred_element_type=F32)


def _ffn_prompt_kernel(x_ref, g_ref, wup_ref, cw_ref, cb_ref, wdn_ref, y_ref, tail_ref, carry_sc,
                       *, tiles_per_seq):
    ff = wdn_ref.shape[0]
    tm = x_ref.shape[0]
    x, a, b = _ffn_up(x_ref, g_ref, wup_ref, ff)
    first = (pl.program_id(0) % tiles_per_seq) == 0
    prev = jnp.where(first, 0.0, carry_sc[...])
    p6 = prev[SUBLANES - 2:SUBLANES - 1, :]
    p7 = prev[SUBLANES - 1:SUBLANES, :]
    row = lax.broadcasted_iota(jnp.int32, a.shape, 0)
    a_m1 = jnp.where(row == 0, p7, pltpu.roll(a, 1, 0))
    a_m2 = jnp.where(row == 0, p6, jnp.where(row == 1, p7, pltpu.roll(a, 2, 0)))
    last_rows = a[tm - SUBLANES:, :]
    carry_sc[...] = last_rows
    tail_ref[0] = last_rows
    y_ref[...] = _ffn_down(x, a_m1, a_m2, a, b, cw_ref, cb_ref, wdn_ref)


def _ffn_sample_kernel(x_ref, g_ref, wup_ref, cw_ref, cb_ref, wdn_ref, s1_ref, s2_ref, y_ref, a_ref,
                       *, dec_t):
    ff = wdn_ref.shape[0]
    x, a, b = _ffn_up(x_ref, g_ref, wup_ref, ff)
    t = lax.broadcasted_iota(jnp.int32, a.shape, 0) % dec_t
    a_m1 = jnp.where(t >= 1, pltpu.roll(a, 1, 0), s1_ref[...])
    a_m2 = jnp.where(t >= 2, pltpu.roll(a, 2, 0), s2_ref[...])
    a_ref[...] = a
    y_ref[...] = _ffn_down(x, a_m1, a_m2, a, b, cw_ref, cb_ref, wdn_ref)


def _ffn_prompt(x, g, wup, cw, cb, wdn, batch, seq):
    n, d = x.shape
    ff = wdn.shape[0]
    tm = min(WIDE_TOKEN_TILE, seq)
    tiles_per_seq = seq // tm
    row = lambda w: pl.BlockSpec((tm, w), lambda i: (i, 0))
    kern = functools.partial(_ffn_prompt_kernel, tiles_per_seq=tiles_per_seq)
    return pl.pallas_call(
        kern,
        grid=(n // tm,),
        in_specs=[row(d), _const_spec(g.shape), _const_spec(wup.shape), _const_spec(cw.shape),
                  _const_spec(cb.shape), _const_spec(wdn.shape)],
        out_specs=[row(d), pl.BlockSpec((1, SUBLANES, ff), lambda i: (i // tiles_per_seq, 0, 0))],
        out_shape=[jax.ShapeDtypeStruct((n, d), F32), jax.ShapeDtypeStruct((batch, SUBLANES, ff), F32)],
        scratch_shapes=[pltpu.VMEM((SUBLANES, ff), F32)],
        compiler_params=_cparams(("arbitrary",)),
        name="ffn_prompt",
    )(x, g, wup, cw, cb, wdn)


def _ffn_sample(x, g, wup, cw, cb, wdn, s1, s2, dec_t):
    n, d = x.shape
    ff = wdn.shape[0]
    tm = min(TOKEN_TILE, n)
    row = lambda w: pl.BlockSpec((tm, w), lambda i: (i, 0))
    kern = functools.partial(_ffn_sample_kernel, dec_t=dec_t)
    return pl.pallas_call(
        kern,
        grid=(n // tm,),
        in_specs=[row(d), _const_spec(g.shape), _const_spec(wup.shape), _const_spec(cw.shape),
                  _const_spec(cb.shape), _const_spec(wdn.shape), row(ff), row(ff)],
        out_specs=[row(d), row(ff)],
        out_shape=[jax.ShapeDtypeStruct((n, d), F32), jax.ShapeDtypeStruct((n, ff), F32)],
        compiler_params=_cparams(("arbitrary",)),
        name="ffn_sample",
    )(x, g, wup, cw, cb, wdn, s1, s2)


def _bucket_table(max_dist):
    n = np.arange(max_dist + 1, dtype=np.int32)
    max_exact = N_BUCKETS // 2
    nf = np.maximum(n, 1).astype(np.float32)
    large = max_exact + (np.log(nf / np.float32(max_exact)) / np.float32(math.log(MAX_DISTANCE / max_exact))
                         * np.float32(N_BUCKETS - max_exact)).astype(np.int32)
    large = np.minimum(large, N_BUCKETS - 1)
    return np.where(n < max_exact, n, large).astype(np.int32)


def _bias_kernel(rs_ref, prompt_ref, last_ref, new_ref, *, upper, tq, dec_t, page):
    h = pl.program_id(0)

    def lookup(dist):
        val = jnp.zeros(dist.shape, F32) + rs_ref[N_BUCKETS - 1, h]
        for b in range(N_BUCKETS - 2, -1, -1):
            val = jnp.where(dist <= upper[b], rs_ref[b, h], val)
        return jnp.where(dist >= 0, val, NEG_BIG)

    ql = lax.broadcasted_iota(jnp.int32, (tq, tq), 0)
    kl = lax.broadcasted_iota(jnp.int32, (tq, tq), 1)
    prompt_ref[0, 0] = lookup(ql - kl + tq)
    prompt_ref[0, 1] = lookup(ql - kl)
    t = lax.broadcasted_iota(jnp.int32, (2 * dec_t, page), 0) % dec_t
    j = lax.broadcasted_iota(jnp.int32, (2 * dec_t, page), 1)
    last_ref[...] = lookup(page + t - j)
    new_ref[...] = lookup(t - j)


def _bias_tiles(rel_shift, tq, dec_t, page):
    table = _bucket_table(2 * MAX_DISTANCE)
    upper = tuple(int(np.nonzero(table <= b)[0].max()) for b in range(N_BUCKETS - 1))
    heads = rel_shift.shape[1]
    kern = functools.partial(_bias_kernel, upper=upper, tq=tq, dec_t=dec_t, page=page)
    return pl.pallas_call(
        kern,
        grid=(heads,),
        in_specs=[pl.BlockSpec(memory_space=pltpu.SMEM)],
        out_specs=[pl.BlockSpec((1, 2, tq, tq), lambda h: (h, 0, 0, 0)),
                   pl.BlockSpec((2 * dec_t, page), lambda h: (h, 0)),
                   pl.BlockSpec((2 * dec_t, page), lambda h: (h, 0))],
        out_shape=[jax.ShapeDtypeStruct((heads, 2, tq, tq), F32),
                   jax.ShapeDtypeStruct((heads * 2 * dec_t, page), F32),
                   jax.ShapeDtypeStruct((heads * 2 * dec_t, page), F32)],
        compiler_params=_cparams(("arbitrary",)),
        name="bias_tiles",
    )(rel_shift)


def _lower_triangles(w):
    j = np.arange(w)[:, None]
    s = np.arange(w)[None, :]
    u = -(j > s).astype(np.float32)
    return jnp.asarray(np.concatenate([u, u], axis=0), dtype=BF16)


def _group_mean_matrix():
    i = np.arange(MXU_EDGE)
    m = (i[:, None] // HEAD_DIM_A == i[None, :] // HEAD_DIM_A).astype(np.float32) / HEAD_DIM_A
    return jnp.asarray(m, dtype=BF16)


def kernel(x_prompt, x_sample, cache_da_k, cache_da_v, cache_sb_k, cache_sb_v, state_conv, page_table,
           rel_bias, attn_norm_g, w_in, q_norm_g, k_norm_g, w_lambda, subln_g, w_proj_a, w_proj_b, w_out,
           ffn_norm_g, w_up, conv_w, conv_b, w_down):
    batch, seq, d = x_prompt.shape
    dec_b, dec_t, _ = x_sample.shape
    depth = w_in.shape[0]
    n_pool, page = cache_da_k.shape[1], cache_da_k.shape[2]
    n_pages = page_table.shape[1]
    past_len = n_pages * page
    ff = w_down.shape[1]

    pt = page_table.reshape(-1).astype(jnp.int32)
    cda_kt = jnp.swapaxes(cache_da_k.reshape(depth, n_pool, page, WIDTH_A), 2, 3)
    cda_vi = cache_da_v.reshape(depth, n_pool, page * N_HEADS_A, 2 * HEAD_DIM_A)
    csb_kt = jnp.swapaxes(cache_sb_k.reshape(depth, n_pool, page, WIDTH_B), 2, 3)
    csb_vt = jnp.swapaxes(cache_sb_v.reshape(depth, n_pool, page, WIDTH_B), 2, 3)

    rel_shift = ((rel_bias - rel_bias[N_BUCKETS - 1]) * LOG2E).astype(F32)
    pages = min(PAGES_PER_STEP, n_pages)
    bias_prompt, blast, bnew = _bias_tiles(rel_shift, min(DA_TILE, seq), dec_t, page)
    blast = jnp.pad(blast, ((0, 0), ((pages - 1) * page, 0)))

    u_hi_lo = _lower_triangles(min(SB_TILE, seq))
    gm = _group_mean_matrix()

    c_qa, c_ka, c_va = 0, WIDTH_A, 2 * WIDTH_A
    c_qb, c_kb, c_vb, c_g = 3 * WIDTH_A, 3 * WIDTH_A + WIDTH_B, 3 * WIDTH_A + 2 * WIDTH_B, 3 * WIDTH_A + 3 * WIDTH_B

    xp = x_prompt.reshape(batch * seq, d)
    xs = x_sample.reshape(dec_b * dec_t, d)
    kv_prompt, tails_p = None, []
    rows_s = [[] for _ in range(5)]
    for l in range(depth):
        lam_init = LAMBDA_BASE - LAMBDA_SCALE * math.exp(-LAMBDA_RATE * l)
        g_attn = attn_norm_g[l].reshape(1, d)
        wl_in = w_in[l].astype(BF16)
        w_std = jnp.concatenate([wl_in[:, c_qa:c_ka], wl_in[:, c_va:c_qb], wl_in[:, c_qb:c_kb],
                                 wl_in[:, c_vb:c_g], wl_in[:, c_g:]], axis=1)
        w_keys = jnp.concatenate([wl_in[:, c_ka:c_va], wl_in[:, c_kb:c_vb]], axis=1)
        w_t = jnp.concatenate([w_keys, wl_in[:, c_vb:c_g]], axis=1).T
        qg_t = jnp.tile(q_norm_g[l], WIDTH_A // HEAD_DIM_A).reshape(1, WIDTH_A)
        kg_t = jnp.tile(k_norm_g[l], WIDTH_A // HEAD_DIM_A).reshape(1, WIDTH_A)
        kg_col = k_norm_g[l].reshape(HEAD_DIM_A, 1)
        wl = w_lambda[l]
        sg = subln_g[l].reshape(1, 2 * HEAD_DIM_A)
        wpa, wpb, wo = w_proj_a[l].astype(BF16), w_proj_b[l].astype(BF16), w_out[l].astype(BF16)
        g_ffn = ffn_norm_g[l].reshape(1, d)
        wup, wdn = w_up[l].astype(BF16), w_down[l].astype(BF16)
        cw, cb = conv_w[l], conv_b[l].reshape(1, ff)

        (qa, kat, katb, vai, vab, qb, kbt, kbtb, vbt, vbb, gates) = _in_proj_prompt(
            xp, g_attn, w_std, w_t, qg_t, kg_col, gm, batch, seq, kv_prompt)
        kv_prompt = (kat, vai, kbt, vbt)
        oa = _da_prompt(qa, katb, vab, bias_prompt, wl, sg, batch, seq, lam_init)
        ob = _sb_prompt(qb, kbtb, vbb, u_hi_lo, batch, seq)
        xp = _merge_out(xp, oa, ob, gates, wpa, wpb, wo)
        xp, tail = _ffn_prompt(xp, g_ffn, wup, cw, cb, wdn, batch, seq)
        tails_p.append(tail[:, SUBLANES - (CONV_WIDTH - 1):, :])

        qa, ka, va, qb, kb, vb, gates = _in_proj_sample(xs, g_attn, w_std, w_keys, qg_t, kg_t, gm)
        oa = _da_sample(pt, qa, ka, va, blast, bnew, wl, sg, cda_kt, cda_vi, l, dec_b, dec_t, lam_init)
        ob = _sb_sample(pt, qb, kb, vb, u_hi_lo, csb_kt, csb_vt, l, dec_b, dec_t)
        xs = _merge_out(xs, oa, ob, gates, wpa, wpb, wo)
        st = state_conv[l]
        zeros = jnp.zeros((dec_b, dec_t - 1, ff), F32)
        s1 = jnp.concatenate([st[:, 1:2], zeros], axis=1).reshape(dec_b * dec_t, ff)
        s2 = jnp.concatenate([st, zeros[:, 1:]], axis=1).reshape(dec_b * dec_t, ff)
        xs, a_s = _ffn_sample(xs, g_ffn, wup, cw, cb, wdn, s1, s2, dec_t)
        new_s = (ka.reshape(dec_b, dec_t, N_HEADS_A, 2, HEAD_DIM_A),
                 va.reshape(dec_b, dec_t, N_HEADS_A, 2 * HEAD_DIM_A),
                 kb.reshape(dec_b, dec_t, N_HEADS_B, HEAD_DIM_B),
                 vb.reshape(dec_b, dec_t, N_HEADS_B, HEAD_DIM_B),
                 a_s.reshape(dec_b, dec_t, ff)[:, dec_t - (CONV_WIDTH - 1):, :])
        for lst, r in zip(rows_s, new_s):
            lst.append(r)

    kat, vai, kbt, vbt = kv_prompt
    out_p = [jnp.transpose(kat.reshape(depth, batch, N_HEADS_A, 2, HEAD_DIM_A, seq), (0, 1, 5, 2, 3, 4)),
             vai.reshape(depth, batch, seq, N_HEADS_A, 2 * HEAD_DIM_A),
             jnp.transpose(kbt.reshape(depth, batch, N_HEADS_B, HEAD_DIM_B, seq), (0, 1, 4, 2, 3)),
             jnp.transpose(vbt.reshape(depth, batch, N_HEADS_B, HEAD_DIM_B, seq), (0, 1, 4, 2, 3)),
             jnp.stack(tails_p)]
    out_s = [jnp.stack(r) for r in rows_s]
    return (xp.reshape(batch, seq, d), xs.reshape(dec_b, dec_t, d), *out_p, *out_s)
```

```python
import functools
import math

import numpy as np
import jax
import jax.numpy as jnp
from jax import lax
from jax.experimental import pallas as pl
from jax.experimental.pallas import tpu as pltpu

F32 = jnp.float32
BF16 = jnp.bfloat16

N_HEADS_A = 8
HEAD_DIM_A = 64
WIDTH_A = N_HEADS_A * 2 * HEAD_DIM_A
N_HEADS_B = 8
HEAD_DIM_B = 64
WIDTH_B = N_HEADS_B * HEAD_DIM_B
N_BUCKETS = 32
MAX_DISTANCE = 128
LAMBDA_BASE = 0.8
LAMBDA_SCALE = 0.6
LAMBDA_RATE = 0.3
NORM_EPS = 1e-6
NEG_BIG = -1e30
CONV_WIDTH = 3

LANES = 128
SUBLANES = 8
MXU_EDGE = 256
VMEM_LIMIT_BYTES = 56 * 1024 * 1024

TOKEN_TILE = 256
WIDE_TOKEN_TILE = 512
DA_TILE = 512
SB_TILE = MXU_EDGE
PAGES_PER_STEP = 16
SB_PAIRS_PER_STEP = 4
SB_NEAR_PAGES = 4
SB_FAR_PAGES = 14

LOG2E = math.log2(math.e)
STICK_UNDERFLOW = -104.0


def _cparams(sem):
    return pltpu.CompilerParams(dimension_semantics=sem, vmem_limit_bytes=VMEM_LIMIT_BYTES)


def _const_spec(shape):
    nd = len(shape)
    return pl.BlockSpec(shape, lambda *_: (0,) * nd)


_NT = (((1,), (1,)), ((), ()))


def _rms_rows(x, g):
    return x * lax.rsqrt(jnp.mean(x * x, axis=-1, keepdims=True) + NORM_EPS) * g


def _head_norm(y, gain, gm):
    sq = (y * y).astype(BF16)
    outs = []
    for j in range(WIDTH_A // MXU_EDGE):
        sl = slice(j * MXU_EDGE, (j + 1) * MXU_EDGE)
        ms = jnp.dot(sq[:, sl], gm, preferred_element_type=F32)
        outs.append(y[:, sl] * lax.rsqrt(ms + NORM_EPS))
    return jnp.concatenate(outs, axis=1) * gain


def _in_proj_prompt_kernel(x_ref, g_ref, ws_ref, wt_ref, qg_ref, kgc_ref, gm_ref, *refs, n_prev):
    prev_refs, refs = refs[:4 if n_prev else 0], refs[4 if n_prev else 0:]
    qa_ref, kat_ref, katb_ref, vai_ref, vab_ref, qb_ref, kbt_ref, kbtb_ref, vbt_ref, vbb_ref, gates_ref = refs
    for dst, src in zip((kat_ref, vai_ref, kbt_ref, vbt_ref), prev_refs):
        dst[0:n_prev] = src[...]
    tm = x_ref.shape[0]
    nb = _rms_rows(x_ref[...], g_ref[...]).astype(BF16)

    def seg(lo, width):
        return jnp.dot(nb, ws_ref[:, lo:lo + width], preferred_element_type=F32)

    def seg_t(lo, width):
        return lax.dot_general(wt_ref[lo:lo + width, :], nb, _NT, preferred_element_type=F32)

    qa = _head_norm(seg(0, WIDTH_A), qg_ref[...], gm_ref[...]) * (HEAD_DIM_A ** -0.5 * LOG2E)
    qa_ref[...] = qa.astype(BF16)
    va = seg(WIDTH_A, WIDTH_A)
    vab_ref[...] = va.astype(BF16)
    for h in range(N_HEADS_A):
        vai_ref[n_prev, pl.ds(h, tm, stride=N_HEADS_A), :] = va[:, h * LANES:(h + 1) * LANES]
    off = 2 * WIDTH_A
    qb_ref[...] = (seg(off, WIDTH_B) * (HEAD_DIM_B ** -0.5)).astype(BF16)
    vbb_ref[...] = seg(off + WIDTH_B, WIDTH_B).astype(BF16)
    gates_ref[...] = seg(off + 2 * WIDTH_B, gates_ref.shape[1])

    kat = seg_t(0, WIDTH_A).reshape(WIDTH_A // HEAD_DIM_A, HEAD_DIM_A, tm)
    ms = jnp.mean(kat * kat, axis=1, keepdims=True)
    kat = (kat * lax.rsqrt(ms + NORM_EPS) * kgc_ref[...].reshape(1, HEAD_DIM_A, 1)).reshape(WIDTH_A, tm)
    kat_ref[n_prev, 0] = kat
    katb_ref[0] = kat.astype(BF16)
    kbt = seg_t(WIDTH_A, WIDTH_B)
    kbt_ref[n_prev, 0] = kbt
    kbtb_ref[0] = kbt.astype(BF16)
    vbt_ref[n_prev, 0] = seg_t(WIDTH_A + WIDTH_B, WIDTH_B)


def _in_proj_prompt(x, g, ws, wt, qg_t, kg_col, gm, batch, seq, prev):
    n, d = x.shape
    tm = min(TOKEN_TILE, seq)
    tps = seq // tm
    n_prev = 0 if prev is None else prev[0].shape[0]
    gate_w = ws.shape[1] - 2 * WIDTH_A - 2 * WIDTH_B
    row = lambda w: pl.BlockSpec((tm, w), lambda i: (i, 0))
    t_blk = lambda layers, w: pl.BlockSpec((layers, 1, w, tm), lambda i: (0, i // tps, 0, i % tps))
    vi_blk = lambda layers: pl.BlockSpec((layers, tm * N_HEADS_A, LANES), lambda i: (0, i, 0))
    t_out = lambda w: t_blk(n_prev + 1, w)
    prev_specs = [] if prev is None else [t_blk(n_prev, WIDTH_A), vi_blk(n_prev),
                                          t_blk(n_prev, WIDTH_B), t_blk(n_prev, WIDTH_B)]
    da_tile, sb_tile = min(DA_TILE, seq), min(SB_TILE, seq)
    t_tile = lambda w, kt: pl.BlockSpec((1, w, tm), lambda i: (i // (kt // tm), 0, i % (kt // tm)))
    out_specs = [row(WIDTH_A), t_out(WIDTH_A), t_tile(WIDTH_A, da_tile),
                 vi_blk(n_prev + 1), row(WIDTH_A),
                 row(WIDTH_B), t_out(WIDTH_B), t_tile(WIDTH_B, sb_tile), t_out(WIDTH_B), row(WIDTH_B), row(gate_w)]
    layers = n_prev + 1
    out_shape = [jax.ShapeDtypeStruct((n, WIDTH_A), BF16),
                 jax.ShapeDtypeStruct((layers, batch, WIDTH_A, seq), F32),
                 jax.ShapeDtypeStruct((n // da_tile, WIDTH_A, da_tile), BF16),
                 jax.ShapeDtypeStruct((layers, n * N_HEADS_A, LANES), F32),
                 jax.ShapeDtypeStruct((n, WIDTH_A), BF16),
                 jax.ShapeDtypeStruct((n, WIDTH_B), BF16),
                 jax.ShapeDtypeStruct((layers, batch, WIDTH_B, seq), F32),
                 jax.ShapeDtypeStruct((n // sb_tile, WIDTH_B, sb_tile), BF16),
                 jax.ShapeDtypeStruct((layers, batch, WIDTH_B, seq), F32),
                 jax.ShapeDtypeStruct((n, WIDTH_B), BF16),
                 jax.ShapeDtypeStruct((n, gate_w), F32)]
    return pl.pallas_call(
        functools.partial(_in_proj_prompt_kernel, n_prev=n_prev),
        grid=(n // tm,),
        in_specs=[row(d), _const_spec(g.shape), _const_spec(ws.shape), _const_spec(wt.shape),
                  _const_spec(qg_t.shape), _const_spec(kg_col.shape), _const_spec(gm.shape)] + prev_specs,
        out_specs=out_specs,
        out_shape=out_shape,
        compiler_params=_cparams(("arbitrary",)),
        name="in_proj_prompt",
    )(x, g, ws, wt, qg_t, kg_col, gm, *(prev or ()))


def _in_proj_sample_kernel(x_ref, g_ref, ws_ref, wk_ref, qg_ref, kg_ref, gm_ref,
                           qa_ref, ka_ref, va_ref, qb_ref, kb_ref, vb_ref, gates_ref):
    nb = _rms_rows(x_ref[...], g_ref[...]).astype(BF16)

    def seg(w_ref, lo, width):
        return jnp.dot(nb, w_ref[:, lo:lo + width], preferred_element_type=F32)

    qa_ref[...] = _head_norm(seg(ws_ref, 0, WIDTH_A), qg_ref[...], gm_ref[...]) * (HEAD_DIM_A ** -0.5 * LOG2E)
    va_ref[...] = seg(ws_ref, WIDTH_A, WIDTH_A)
    off = 2 * WIDTH_A
    qb_ref[...] = seg(ws_ref, off, WIDTH_B) * (HEAD_DIM_B ** -0.5)
    vb_ref[...] = seg(ws_ref, off + WIDTH_B, WIDTH_B)
    gates_ref[...] = seg(ws_ref, off + 2 * WIDTH_B, gates_ref.shape[1])
    ka_ref[...] = _head_norm(seg(wk_ref, 0, WIDTH_A), kg_ref[...], gm_ref[...])
    kb_ref[...] = seg(wk_ref, WIDTH_A, WIDTH_B)


def _in_proj_sample(x, g, ws, wk, qg_t, kg_t, gm):
    n, d = x.shape
    tm = min(TOKEN_TILE, n)
    gate_w = ws.shape[1] - 2 * WIDTH_A - 2 * WIDTH_B
    row = lambda w: pl.BlockSpec((tm, w), lambda i: (i, 0))
    widths = [WIDTH_A, WIDTH_A, WIDTH_A, WIDTH_B, WIDTH_B, WIDTH_B, gate_w]
    return pl.pallas_call(
        _in_proj_sample_kernel,
        grid=(n // tm,),
        in_specs=[row(d), _const_spec(g.shape), _const_spec(ws.shape), _const_spec(wk.shape),
                  _const_spec(qg_t.shape), _const_spec(kg_t.shape), _const_spec(gm.shape)],
        out_specs=[row(w) for w in widths],
        out_shape=[jax.ShapeDtypeStruct((n, w), F32) for w in widths],
        compiler_params=_cparams(("arbitrary",)),
        name="in_proj_sample",
    )(x, g, ws, wk, qg_t, kg_t, gm)


def _lambda_value(wl, lam_init):
    a = jnp.sum(wl[0:1] * wl[1:2], axis=1, keepdims=True)
    b = jnp.sum(wl[2:3] * wl[3:4], axis=1, keepdims=True)
    return jnp.exp(a) - jnp.exp(b) + lam_init


def _sub_norm(d, gain, lam_init):
    y = d * lax.rsqrt(jnp.mean(d * d, axis=-1, keepdims=True) + NORM_EPS)
    return y * gain * (1.0 - lam_init)


def _softmax_probs(s, m_sc, l_sc):
    m_prev = m_sc[...]
    m_new = jnp.maximum(m_prev, jnp.max(s, axis=-1, keepdims=True))
    alpha = jnp.exp2(m_prev - m_new)
    p = jnp.exp2(s - m_new)
    l_sc[...] = alpha * l_sc[...] + jnp.sum(p, axis=-1, keepdims=True)
    m_sc[...] = m_new
    return alpha, p.astype(BF16)


def _lane_tile(x, width):
    reps = width // x.shape[1]
    return x if reps == 1 else jnp.concatenate([x] * reps, axis=1)


def _stick_weights(z, u_hi_lo, carry_sc, valid):
    width = z.shape[1]
    w = jnp.maximum(z, 0.0) + jnp.log(1.0 + jnp.exp(-jnp.abs(z)))
    if valid is not None:
        w = jnp.where(valid, w, 0.0)
    hi = w.astype(BF16)
    lo = (w - hi.astype(F32)).astype(BF16)
    later = jnp.dot(jnp.concatenate([hi, lo], axis=1), u_hi_lo, preferred_element_type=F32)
    carry = carry_sc[...]
    a = jnp.exp(z - w + later + _lane_tile(carry, width))
    if valid is not None:
        a = jnp.where(valid, a, 0.0)
    carry_sc[...] = carry - jnp.sum(w, axis=-1, keepdims=True)
    return a.astype(BF16)


def _split_lanes(q):
    lane = lax.broadcasted_iota(jnp.int32, q.shape, 1)
    zero = jnp.zeros_like(q)
    half = LANES // 2
    return jnp.concatenate([jnp.where(lane < half, q, zero), jnp.where(lane >= half, q, zero)], axis=0)


def _da_prompt_kernel(q_ref, kt_ref, v_ref, bias_ref, wl_ref, sg_ref, o_ref, m_sc, acc_sc,
                      *, tq, lam_init):
    ones = jnp.ones((tq, LANES), BF16)
    lam = _lambda_value(wl_ref[...], lam_init)

    def query_tile(qi, carry):
        rows = pl.ds(pl.multiple_of(qi * tq, tq), tq)
        qq = _split_lanes(q_ref[rows, :])
        m_sc[...] = jnp.full(m_sc.shape, NEG_BIG, F32)
        acc_sc[...] = jnp.zeros(acc_sc.shape, F32)

        def block(ki, bias):
            v_ext = jnp.concatenate([v_ref[pl.ds(pl.multiple_of(ki * tq, tq), tq), :], ones], axis=1)
            s = jnp.dot(qq, kt_ref[ki], preferred_element_type=F32)
            if bias is not None:
                s = (s.reshape(2, tq, tq) + bias[None]).reshape(2 * tq, tq)
            m_prev = m_sc[...]
            m_new = jnp.maximum(m_prev, jnp.max(s, axis=-1, keepdims=True))
            alpha = jnp.exp2(m_prev - m_new)
            p = jnp.concatenate([jnp.exp2(s[:, j * LANES:(j + 1) * LANES] - m_new).astype(BF16)
                                 for j in range(tq // LANES)], axis=1)
            acc_sc[...] = (_lane_tile(alpha, 2 * LANES) * acc_sc[...]
                           + jnp.dot(p, v_ext, preferred_element_type=F32))
            m_sc[...] = m_new

        def far(ki, c):
            block(ki, None)
            return c

        lax.fori_loop(0, qi - 1, far, 0)

        @pl.when(qi >= 1)
        def _():
            block(qi - 1, bias_ref[0, 0])

        block(qi, bias_ref[0, 1])

        acc = acc_sc[...]
        o = acc[:, :LANES] / acc[:, LANES:]
        d = o[:tq] - lam * o[tq:]
        o_ref[rows, :] = _sub_norm(d, sg_ref[...], lam_init).astype(o_ref.dtype)
        return carry

    lax.fori_loop(0, q_ref.shape[0] // tq, query_tile, 0)


def _da_prompt(q, kt, v, bias, wl, sg, batch, seq, lam_init):
    n = q.shape[0]
    tq = kt.shape[2]
    nq = seq // tq
    kern = functools.partial(_da_prompt_kernel, tq=tq, lam_init=lam_init)
    head = pl.BlockSpec((seq, LANES), lambda b, h: (b, h))
    return pl.pallas_call(
        kern,
        grid=(batch, N_HEADS_A),
        in_specs=[head, pl.BlockSpec((nq, LANES, tq), lambda b, h: (b, h, 0)), head,
                  pl.BlockSpec((1, 2, tq, tq), lambda b, h: (h, 0, 0, 0)),
                  _const_spec(wl.shape), _const_spec(sg.shape)],
        out_specs=head,
        out_shape=jax.ShapeDtypeStruct((n, WIDTH_A), BF16),
        scratch_shapes=[pltpu.VMEM((2 * tq, LANES), F32), pltpu.VMEM((2 * tq, 2 * LANES), F32)],
        compiler_params=_cparams(("arbitrary", "arbitrary")),
        name="da_prompt",
    )(q, kt, v, bias, wl, sg)


def _sb_prompt_kernel(q_ref, kt_ref, v_ref, u_ref, o_ref, carry_sc, acc_sc, *, tq, pp):
    qi = pl.program_id(2)
    q = q_ref[...]
    qqs = [_split_lanes(q[:, p * LANES:(p + 1) * LANES]) for p in range(pp)]
    carry_sc[...] = jnp.zeros(carry_sc.shape, F32)
    acc_sc[...] = jnp.zeros(acc_sc.shape, F32)

    def block(ki, valid):
        k0 = pl.multiple_of(ki * tq, tq)
        kt = kt_ref[ki]
        top = None
        for p in range(pp):
            lanes = slice(p * LANES, (p + 1) * LANES)
            z = jnp.dot(qqs[p], kt[lanes, :], preferred_element_type=F32)
            a = _stick_weights(z, u_ref[...], carry_sc.at[p], valid)
            acc_sc[p] = acc_sc[p] + jnp.dot(a, v_ref[pl.ds(k0, tq), lanes], preferred_element_type=F32)
            c = jnp.max(carry_sc[p])
            top = c if top is None else jnp.maximum(top, c)
        return top

    row = lax.broadcasted_iota(jnp.int32, (2, tq, tq), 1).reshape(2 * tq, tq)
    col = lax.broadcasted_iota(jnp.int32, (2 * tq, tq), 1)
    top = block(qi, col < row)

    def more(c):
        return jnp.logical_and(c[0] < qi, c[1] > STICK_UNDERFLOW)

    def left(c):
        return c[0] + 1, block(qi - 1 - c[0], None)

    lax.while_loop(more, left, (jnp.int32(0), top))

    lane = lax.broadcasted_iota(jnp.int32, (tq, LANES), 1)
    for p in range(pp):
        acc = acc_sc[p]
        o_ref[:, p * LANES:(p + 1) * LANES] = jnp.where(lane < LANES // 2, acc[:tq], acc[tq:]).astype(o_ref.dtype)


def _sb_prompt(q, kt, v, u_hi_lo, batch, seq):
    n = q.shape[0]
    tq = kt.shape[2]
    nq = seq // tq
    pp = SB_PAIRS_PER_STEP
    width = pp * LANES
    kern = functools.partial(_sb_prompt_kernel, tq=tq, pp=pp)
    return pl.pallas_call(
        kern,
        grid=(batch, WIDTH_B // width, nq),
        in_specs=[pl.BlockSpec((tq, width), lambda b, h, i: (b * nq + i, h)),
                  pl.BlockSpec((nq, width, tq), lambda b, h, i: (b, h, 0)),
                  pl.BlockSpec((seq, width), lambda b, h, i: (b, h)),
                  _const_spec(u_hi_lo.shape)],
        out_specs=pl.BlockSpec((tq, width), lambda b, h, i: (b * nq + i, h)),
        out_shape=jax.ShapeDtypeStruct((n, WIDTH_B), BF16),
        scratch_shapes=[pltpu.VMEM((pp, 2 * tq, LANES), F32), pltpu.VMEM((pp, 2 * tq, LANES), F32)],
        compiler_params=_cparams(("arbitrary", "arbitrary", "arbitrary")),
        name="sb_prompt",
    )(q, kt, v, u_hi_lo)


def _block_diag_queries(q, groups, width):
    t = q.shape[0]
    rep = jnp.concatenate([q] * groups, axis=0)
    r = lax.broadcasted_iota(jnp.int32, rep.shape, 0)
    c = lax.broadcasted_iota(jnp.int32, rep.shape, 1)
    keep = (c // width) == (r // t)
    return jnp.where(keep, rep, 0.0).astype(BF16)


def _pad_rows(x, rows):
    return jnp.concatenate([x, jnp.zeros((rows - x.shape[0], x.shape[1]), x.dtype)], axis=0)


def _da_sample_kernel(pt_ref, q_ref, knew_ref, vnew_ref, blast_ref, bnew_ref, wl_ref, sg_ref, *rest,
                      pages, page, lam_init):
    del pt_ref
    kt_refs, v_refs = rest[:pages], rest[pages:2 * pages]
    o_ref, qbd_sc, m_sc, l_sc, acc_sc = rest[2 * pages:]
    g = pl.program_id(1)
    last = pl.num_programs(1) - 1
    t = q_ref.shape[0]
    rows_h = 2 * t

    @pl.when(g == 0)
    def _():
        qbd_sc[...] = _block_diag_queries(q_ref[...], 2 * N_HEADS_A, HEAD_DIM_A)
        m_sc[...] = jnp.full(m_sc.shape, NEG_BIG, F32)
        l_sc[...] = jnp.zeros(l_sc.shape, F32)
        acc_sc[...] = jnp.zeros(acc_sc.shape, F32)

    qbd = qbd_sc[...]
    ktcat = jnp.concatenate([r[0, 0].astype(BF16) for r in kt_refs], axis=1)
    s = jnp.dot(qbd, ktcat, preferred_element_type=F32)
    s = s + blast_ref[...] * (g == last).astype(F32)
    alpha, p = _softmax_probs(s, m_sc, l_sc)
    for h in range(N_HEADS_A):
        vh = jnp.concatenate([r[0, 0, pl.ds(h, page, stride=N_HEADS_A), :] for r in v_refs], axis=0)
        rs = slice(h * rows_h, (h + 1) * rows_h)
        acc_sc[rs, :] = alpha[rs] * acc_sc[rs, :] + jnp.dot(p[rs], vh.astype(BF16), preferred_element_type=F32)

    @pl.when(g == last)
    def _():
        kn = _pad_rows(knew_ref[...], page).astype(BF16)
        vn = _pad_rows(vnew_ref[...], page).astype(BF16)
        s2 = lax.dot_general(qbd, kn, _NT, preferred_element_type=F32) + bnew_ref[...]
        alpha2, p2 = _softmax_probs(s2, m_sc, l_sc)
        o_all = jnp.dot(p2, vn, preferred_element_type=F32)
        inv_l = 1.0 / l_sc[...]
        lam = _lambda_value(wl_ref[...], lam_init)
        outs = []
        for h in range(N_HEADS_A):
            rs = slice(h * rows_h, (h + 1) * rows_h)
            cols = slice(h * 2 * HEAD_DIM_A, (h + 1) * 2 * HEAD_DIM_A)
            o = (alpha2[rs] * acc_sc[rs, :] + o_all[rs, cols]) * inv_l[rs]
            outs.append(_sub_norm(o[:t] - lam * o[t:], sg_ref[...], lam_init))
        o_ref[...] = jnp.concatenate(outs, axis=1)


def _page_specs(layer, n_pages, pages, block, reverse):
    specs = []
    for j in range(pages):
        if reverse:
            idx = lambda b, g, pt, j=j: (layer, pt[b * n_pages + n_pages - (g + 1) * pages + j], 0, 0)
        else:
            idx = lambda b, g, pt, j=j: (layer, pt[b * n_pages + g * pages + j], 0, 0)
        specs.append(pl.BlockSpec((1, 1) + block, idx))
    return specs


def _da_sample(pt, q, knew, vnew, blast, bnew, wl, sg, cache_kt, cache_vi, layer, dec_b, dec_t, lam_init):
    n_pages = pt.shape[0] // dec_b
    page = cache_kt.shape[3]
    pages = min(PAGES_PER_STEP, n_pages)
    rows = 2 * N_HEADS_A * dec_t
    tok = lambda w: pl.BlockSpec((dec_t, w), lambda b, g, pt: (b, 0))
    const = lambda a: pl.BlockSpec(a.shape, lambda b, g, pt: (0,) * a.ndim)
    kern = functools.partial(_da_sample_kernel, pages=pages, page=page, lam_init=lam_init)
    grid_spec = pltpu.PrefetchScalarGridSpec(
        num_scalar_prefetch=1,
        grid=(dec_b, n_pages // pages),
        in_specs=[tok(WIDTH_A), tok(WIDTH_A), tok(WIDTH_A), const(blast), const(bnew), const(wl), const(sg)]
        + _page_specs(layer, n_pages, pages, (WIDTH_A, page), False)
        + _page_specs(layer, n_pages, pages, (page * N_HEADS_A, 2 * HEAD_DIM_A), False),
        out_specs=tok(WIDTH_A),
        scratch_shapes=[pltpu.VMEM((rows, WIDTH_A), BF16), pltpu.VMEM((rows, 1), F32),
                        pltpu.VMEM((rows, 1), F32), pltpu.VMEM((rows, 2 * HEAD_DIM_A), F32)],
    )
    return pl.pallas_call(
        kern,
        grid_spec=grid_spec,
        out_shape=jax.ShapeDtypeStruct((dec_b * dec_t, WIDTH_A), F32),
        compiler_params=_cparams(("arbitrary", "arbitrary")),
        name="da_sample",
    )(pt, q, knew, vnew, blast, bnew, wl, sg, *([cache_kt] * pages), *([cache_vi] * pages))


def _sb_sweep_pages(qbd, kt_refs, vt_refs, u_ref, carry_ref, acc_ref):
    blk = u_ref.shape[1]
    ktcat = jnp.concatenate([r[0, 0].astype(BF16) for r in kt_refs], axis=1)
    vtcat = jnp.concatenate([r[0, 0].astype(BF16) for r in vt_refs], axis=1)
    z = jnp.dot(qbd, ktcat, preferred_element_type=F32)
    for b in reversed(range(ktcat.shape[1] // blk)):
        sl = slice(b * blk, (b + 1) * blk)
        a = _stick_weights(z[:, sl], u_ref[...], carry_ref, None)
        acc_ref[...] = acc_ref[...] + lax.dot_general(a, vtcat[:, sl], _NT, preferred_element_type=F32)


def _sb_near_kernel(pt_ref, q_ref, knew_ref, vnew_ref, u_ref, *rest, pages, page):
    del pt_ref
    kt_refs, vt_refs = rest[:pages], rest[pages:2 * pages]
    acc_ref, carry_ref, o_ref = rest[2 * pages:]
    t = q_ref.shape[0]
    qbd = _block_diag_queries(q_ref[...], N_HEADS_B, HEAD_DIM_B)
    carry_ref[...] = jnp.zeros(carry_ref.shape, F32)
    kn = _pad_rows(knew_ref[...], page).astype(BF16)
    vn = _pad_rows(vnew_ref[...], page).astype(BF16)
    z = lax.dot_general(qbd, kn, _NT, preferred_element_type=F32)
    row_t = lax.broadcasted_iota(jnp.int32, z.shape, 0) % t
    col = lax.broadcasted_iota(jnp.int32, z.shape, 1)
    u_page = jnp.concatenate([u_ref[0:page, 0:page], u_ref[0:page, 0:page]], axis=0)
    a = _stick_weights(z, u_page, carry_ref, col < row_t)
    acc_ref[...] = jnp.dot(a, vn, preferred_element_type=F32)
    _sb_sweep_pages(qbd, kt_refs, vt_refs, u_ref, carry_ref, acc_ref)
    o_ref[...] = _own_head_columns(acc_ref[...], t)


def _own_head_columns(acc, t):
    col_h = lax.broadcasted_iota(jnp.int32, (t, WIDTH_B), 1) // HEAD_DIM_B
    out = jnp.zeros((t, WIDTH_B), F32)
    for h in range(N_HEADS_B):
        out = out + jnp.where(col_h == h, acc[h * t:(h + 1) * t, :], 0.0)
    return out


def _sb_far_kernel(pt_ref, need_ref, q_ref, acc_in_ref, carry_in_ref, u_ref, *rest, pages):
    del pt_ref
    kt_refs, vt_refs = rest[:pages], rest[pages:2 * pages]
    o_ref, qbd_sc, carry_sc, acc_sc = rest[2 * pages:]
    b = pl.program_id(0)
    g = pl.program_id(1)
    last = pl.num_programs(1) - 1
    t = q_ref.shape[0]

    @pl.when(g == 0)
    def _():
        qbd_sc[...] = _block_diag_queries(q_ref[...], N_HEADS_B, HEAD_DIM_B)
        carry_sc[...] = carry_in_ref[...]
        acc_sc[...] = acc_in_ref[...]

    @pl.when(need_ref[b] > 0)
    def _():
        _sb_sweep_pages(qbd_sc[...], kt_refs, vt_refs, u_ref, carry_sc, acc_sc)

    @pl.when(g == last)
    def _():
        o_ref[...] = _own_head_columns(acc_sc[...], t)


def _sb_sample(pt, q, knew, vnew, u_hi_lo, cache_kt, cache_vt, layer, dec_b, dec_t):
    n_pages = pt.shape[0] // dec_b
    page = cache_kt.shape[3]
    near = min(SB_NEAR_PAGES, n_pages)
    n_far = n_pages - near
    assert n_far > 0, "the far sweep also does the final head extraction"
    far = max(p for p in range(1, SB_FAR_PAGES + 1) if n_far % p == 0)
    rows = N_HEADS_B * dec_t
    blk = (WIDTH_B, page)

    tok1 = lambda w: pl.BlockSpec((dec_t, w), lambda b, pt: (b, 0))
    state1 = lambda w: pl.BlockSpec((rows, w), lambda b, pt: (b, 0))
    near_specs = [pl.BlockSpec((1, 1) + blk, lambda b, pt, j=j: (layer, pt[b * n_pages + n_far + j], 0, 0))
                  for j in range(near)]
    acc, carry, o_near = pl.pallas_call(
        functools.partial(_sb_near_kernel, pages=near, page=page),
        grid_spec=pltpu.PrefetchScalarGridSpec(
            num_scalar_prefetch=1,
            grid=(dec_b,),
            in_specs=[tok1(WIDTH_B), tok1(WIDTH_B), tok1(WIDTH_B),
                      pl.BlockSpec(u_hi_lo.shape, lambda b, pt: (0, 0))] + near_specs + near_specs,
            out_specs=[state1(WIDTH_B), state1(LANES), tok1(WIDTH_B)],
        ),
        out_shape=[jax.ShapeDtypeStruct((dec_b * rows, WIDTH_B), F32),
                   jax.ShapeDtypeStruct((dec_b * rows, LANES), F32),
                   jax.ShapeDtypeStruct((dec_b * dec_t, WIDTH_B), F32)],
        compiler_params=_cparams(("arbitrary",)),
        name="sb_sample_near",
    )(pt, q, knew, vnew, u_hi_lo, *([cache_kt] * near), *([cache_vt] * near))

    need = (jnp.max(carry.reshape(dec_b, rows * LANES), axis=1) > STICK_UNDERFLOW).astype(jnp.int32)
    tok2 = lambda w: pl.BlockSpec((dec_t, w), lambda b, g, pt, nd: (b, 0))
    state2 = lambda w: pl.BlockSpec((rows, w), lambda b, g, pt, nd: (b, 0))

    def far_spec(j):
        def idx(b, g, pt, nd):
            return (layer, jnp.where(nd[b] > 0, pt[b * n_pages + n_far - (g + 1) * far + j], 0), 0, 0)
        return pl.BlockSpec((1, 1) + blk, idx)

    far_specs = [far_spec(j) for j in range(far)]

    def sweep_far():
        return pl.pallas_call(
            functools.partial(_sb_far_kernel, pages=far),
            grid_spec=pltpu.PrefetchScalarGridSpec(
                num_scalar_prefetch=2,
                grid=(dec_b, n_far // far),
                in_specs=[tok2(WIDTH_B), state2(WIDTH_B), state2(LANES),
                          pl.BlockSpec(u_hi_lo.shape, lambda b, g, pt, nd: (0, 0))] + far_specs + far_specs,
                out_specs=tok2(WIDTH_B),
                scratch_shapes=[pltpu.VMEM((rows, WIDTH_B), BF16), pltpu.VMEM((rows, LANES), F32),
                                pltpu.VMEM((rows, WIDTH_B), F32)],
            ),
            out_shape=jax.ShapeDtypeStruct((dec_b * dec_t, WIDTH_B), F32),
            compiler_params=_cparams(("arbitrary", "arbitrary")),
            name="sb_sample_far",
        )(pt, need, q, acc, carry, u_hi_lo, *([cache_kt] * far), *([cache_vt] * far))

    return lax.cond(jnp.any(need > 0), sweep_far, lambda: o_near)


def _merge_out_kernel(x_ref, oa_ref, ob_ref, gates_ref, wpa_ref, wpb_ref, wo_ref, o_ref):
    d = x_ref.shape[1]
    ga = jax.nn.sigmoid(gates_ref[:, :d])
    gb = jax.nn.sigmoid(gates_ref[:, d:])
    pa = jnp.dot(oa_ref[...].astype(BF16), wpa_ref[...], preferred_element_type=F32)
    pb = jnp.dot(ob_ref[...].astype(BF16), wpb_ref[...], preferred_element_type=F32)
    merged = ga * pa + gb * pb
    o_ref[...] = x_ref[...] + jnp.dot(merged.astype(BF16), wo_ref[...], preferred_element_type=F32)


def _merge_out(x, oa, ob, gates, wpa, wpb, wo):
    n, d = x.shape
    tm = min(WIDE_TOKEN_TILE, n)
    row = lambda w: pl.BlockSpec((tm, w), lambda i: (i, 0))
    return pl.pallas_call(
        _merge_out_kernel,
        grid=(n // tm,),
        in_specs=[row(d), row(WIDTH_A), row(WIDTH_B), row(2 * d),
                  _const_spec(wpa.shape), _const_spec(wpb.shape), _const_spec(wo.shape)],
        out_specs=row(d),
        out_shape=jax.ShapeDtypeStruct((n, d), F32),
        compiler_params=_cparams(("arbitrary",)),
        name="merge_out",
    )(x, oa, ob, gates, wpa, wpb, wo)


def _gelu_exact(x):
    return 0.5 * x * (1.0 + lax.erf(x * np.float32(math.sqrt(0.5))))


def _ffn_up(x_ref, g_ref, wup_ref, ff):
    x = x_ref[...]
    n2 = _rms_rows(x, g_ref[...]).astype(BF16)
    a = jnp.dot(n2, wup_ref[:, :ff], preferred_element_type=F32)
    b = jnp.dot(n2, wup_ref[:, ff:], preferred_element_type=F32)
    return x, a, b


def _ffn_down(x, a_m1, a_m2, a, b, cw_ref, cb_ref, wdn_ref):
    conv = cb_ref[...] + a_m2 * cw_ref[0:1, :] + a_m1 * cw_ref[1:2, :] + a * cw_ref[2:3, :]
    h = (_gelu_exact(conv) * b).astype(BF16)
    return x + jnp.dot(h, wdn_ref[...], preferred_element_type=F32)


def _ffn_prompt_kernel(x_ref, g_ref, wup_ref, cw_ref, cb_ref, wdn_ref, y_ref, tail_ref, carry_sc,
                       *, tiles_per_seq):
    ff = wdn_ref.shape[0]
    tm = x_ref.shape[0]
    x, a, b = _ffn_up(x_ref, g_ref, wup_ref, ff)
    first = (pl.program_id(0) % tiles_per_seq) == 0
    prev = jnp.where(first, 0.0, carry_sc[...])
    p6 = prev[SUBLANES - 2:SUBLANES - 1, :]
    p7 = prev[SUBLANES - 1:SUBLANES, :]
    row = lax.broadcasted_iota(jnp.int32, a.shape, 0)
    a_m1 = jnp.where(row == 0, p7, pltpu.roll(a, 1, 0))
    a_m2 = jnp.where(row == 0, p6, jnp.where(row == 1, p7, pltpu.roll(a, 2, 0)))
    last_rows = a[tm - SUBLANES:, :]
    carry_sc[...] = last_rows
    tail_ref[0] = last_rows
    y_ref[...] = _ffn_down(x, a_m1, a_m2, a, b, cw_ref, cb_ref, wdn_ref)


def _ffn_sample_kernel(x_ref, g_ref, wup_ref, cw_ref, cb_ref, wdn_ref, s1_ref, s2_ref, y_ref, a_ref,
                       *, dec_t):
    ff = wdn_ref.shape[0]
    x, a, b = _ffn_up(x_ref, g_ref, wup_ref, ff)
    t = lax.broadcasted_iota(jnp.int32, a.shape, 0) % dec_t
    a_m1 = jnp.where(t >= 1, pltpu.roll(a, 1, 0), s1_ref[...])
    a_m2 = jnp.where(t >= 2, pltpu.roll(a, 2, 0), s2_ref[...])
    a_ref[...] = a
    y_ref[...] = _ffn_down(x, a_m1, a_m2, a, b, cw_ref, cb_ref, wdn_ref)


def _ffn_prompt(x, g, wup, cw, cb, wdn, batch, seq):
    n, d = x.shape
    ff = wdn.shape[0]
    tm = min(WIDE_TOKEN_TILE, seq)
    tiles_per_seq = seq // tm
    row = lambda w: pl.BlockSpec((tm, w), lambda i: (i, 0))
    kern = functools.partial(_ffn_prompt_kernel, tiles_per_seq=tiles_per_seq)
    return pl.pallas_call(
        kern,
        grid=(n // tm,),
        in_specs=[row(d), _const_spec(g.shape), _const_spec(wup.shape), _const_spec(cw.shape),
                  _const_spec(cb.shape), _const_spec(wdn.shape)],
        out_specs=[row(d), pl.BlockSpec((1, SUBLANES, ff), lambda i: (i // tiles_per_seq, 0, 0))],
        out_shape=[jax.ShapeDtypeStruct((n, d), F32), jax.ShapeDtypeStruct((batch, SUBLANES, ff), F32)],
        scratch_shapes=[pltpu.VMEM((SUBLANES, ff), F32)],
        compiler_params=_cparams(("arbitrary",)),
        name="ffn_prompt",
    )(x, g, wup, cw, cb, wdn)


def _ffn_sample(x, g, wup, cw, cb, wdn, s1, s2, dec_t):
    n, d = x.shape
    ff = wdn.shape[0]
    tm = min(TOKEN_TILE, n)
    row = lambda w: pl.BlockSpec((tm, w), lambda i: (i, 0))
    kern = functools.partial(_ffn_sample_kernel, dec_t=dec_t)
    return pl.pallas_call(
        kern,
        grid=(n // tm,),
        in_specs=[row(d), _const_spec(g.shape), _const_spec(wup.shape), _const_spec(cw.shape),
                  _const_spec(cb.shape), _const_spec(wdn.shape), row(ff), row(ff)],
        out_specs=[row(d), row(ff)],
        out_shape=[jax.ShapeDtypeStruct((n, d), F32), jax.ShapeDtypeStruct((n, ff), F32)],
        compiler_params=_cparams(("arbitrary",)),
        name="ffn_sample",
    )(x, g, wup, cw, cb, wdn, s1, s2)


def _bucket_table(max_dist):
    n = np.arange(max_dist + 1, dtype=np.int32)
    max_exact = N_BUCKETS // 2
    nf = np.maximum(n, 1).astype(np.float32)
    large = max_exact + (np.log(nf / np.float32(max_exact)) / np.float32(math.log(MAX_DISTANCE / max_exact))
                         * np.float32(N_BUCKETS - max_exact)).astype(np.int32)
    large = np.minimum(large, N_BUCKETS - 1)
    return np.where(n < max_exact, n, large).astype(np.int32)


def _bias_kernel(rs_ref, prompt_ref, last_ref, new_ref, *, upper, tq, dec_t, page):
    h = pl.program_id(0)

    def lookup(dist):
        val = jnp.zeros(dist.shape, F32) + rs_ref[N_BUCKETS - 1, h]
        for b in range(N_BUCKETS - 2, -1, -1):
            val = jnp.where(dist <= upper[b], rs_ref[b, h], val)
        return jnp.where(dist >= 0, val, NEG_BIG)

    ql = lax.broadcasted_iota(jnp.int32, (tq, tq), 0)
    kl = lax.broadcasted_iota(jnp.int32, (tq, tq), 1)
    prompt_ref[0, 0] = lookup(ql - kl + tq)
    prompt_ref[0, 1] = lookup(ql - kl)
    t = lax.broadcasted_iota(jnp.int32, (2 * dec_t, page), 0) % dec_t
    j = lax.broadcasted_iota(jnp.int32, (2 * dec_t, page), 1)
    last_ref[...] = lookup(page + t - j)
    new_ref[...] = lookup(t - j)


def _bias_tiles(rel_shift, tq, dec_t, page):
    table = _bucket_table(2 * MAX_DISTANCE)
    upper = tuple(int(np.nonzero(table <= b)[0].max()) for b in range(N_BUCKETS - 1))
    heads = rel_shift.shape[1]
    kern = functools.partial(_bias_kernel, upper=upper, tq=tq, dec_t=dec_t, page=page)
    return pl.pallas_call(
        kern,
        grid=(heads,),
        in_specs=[pl.BlockSpec(memory_space=pltpu.SMEM)],
        out_specs=[pl.BlockSpec((1, 2, tq, tq), lambda h: (h, 0, 0, 0)),
                   pl.BlockSpec((2 * dec_t, page), lambda h: (h, 0)),
                   pl.BlockSpec((2 * dec_t, page), lambda h: (h, 0))],
        out_shape=[jax.ShapeDtypeStruct((heads, 2, tq, tq), F32),
                   jax.ShapeDtypeStruct((heads * 2 * dec_t, page), F32),
                   jax.ShapeDtypeStruct((heads * 2 * dec_t, page), F32)],
        compiler_params=_cparams(("arbitrary",)),
        name="bias_tiles",
    )(rel_shift)


def _lower_triangles(w):
    j = np.arange(w)[:, None]
    s = np.arange(w)[None, :]
    u = -(j > s).astype(np.float32)
    return jnp.asarray(np.concatenate([u, u], axis=0), dtype=BF16)


def _group_mean_matrix():
    i = np.arange(MXU_EDGE)
    m = (i[:, None] // HEAD_DIM_A == i[None, :] // HEAD_DIM_A).astype(np.float32) / HEAD_DIM_A
    return jnp.asarray(m, dtype=BF16)


def kernel(x_prompt, x_sample, cache_da_k, cache_da_v, cache_sb_k, cache_sb_v, state_conv, page_table,
           rel_bias, attn_norm_g, w_in, q_norm_g, k_norm_g, w_lambda, subln_g, w_proj_a, w_proj_b, w_out,
           ffn_norm_g, w_up, conv_w, conv_b, w_down):
    batch, seq, d = x_prompt.shape
    dec_b, dec_t, _ = x_sample.shape
    depth = w_in.shape[0]
    n_pool, page = cache_da_k.shape[1], cache_da_k.shape[2]
    n_pages = page_table.shape[1]
    past_len = n_pages * page
    ff = w_down.shape[1]

    pt = page_table.reshape(-1).astype(jnp.int32)
    cda_kt = jnp.swapaxes(cache_da_k.reshape(depth, n_pool, page, WIDTH_A), 2, 3)
    cda_vi = cache_da_v.reshape(depth, n_pool, page * N_HEADS_A, 2 * HEAD_DIM_A)
    csb_kt = jnp.swapaxes(cache_sb_k.reshape(depth, n_pool, page, WIDTH_B), 2, 3)
    csb_vt = jnp.swapaxes(cache_sb_v.reshape(depth, n_pool, page, WIDTH_B), 2, 3)

    rel_shift = ((rel_bias - rel_bias[N_BUCKETS - 1]) * LOG2E).astype(F32)
    pages = min(PAGES_PER_STEP, n_pages)
    bias_prompt, blast, bnew = _bias_tiles(rel_shift, min(DA_TILE, seq), dec_t, page)
    blast = jnp.pad(blast, ((0, 0), ((pages - 1) * page, 0)))

    u_hi_lo = _lower_triangles(min(SB_TILE, seq))
    gm = _group_mean_matrix()

    c_qa, c_ka, c_va = 0, WIDTH_A, 2 * WIDTH_A
    c_qb, c_kb, c_vb, c_g = 3 * WIDTH_A, 3 * WIDTH_A + WIDTH_B, 3 * WIDTH_A + 2 * WIDTH_B, 3 * WIDTH_A + 3 * WIDTH_B

    xp = x_prompt.reshape(batch * seq, d)
    xs = x_sample.reshape(dec_b * dec_t, d)
    kv_prompt, tails_p = None, []
    rows_s = [[] for _ in range(5)]
    for l in range(depth):
        lam_init = LAMBDA_BASE - LAMBDA_SCALE * math.exp(-LAMBDA_RATE * l)
        g_attn = attn_norm_g[l].reshape(1, d)
        wl_in = w_in[l].astype(BF16)
        w_std = jnp.concatenate([wl_in[:, c_qa:c_ka], wl_in[:, c_va:c_qb], wl_in[:, c_qb:c_kb],
                                 wl_in[:, c_vb:c_g], wl_in[:, c_g:]], axis=1)
        w_keys = jnp.concatenate([wl_in[:, c_ka:c_va], wl_in[:, c_kb:c_vb]], axis=1)
        w_t = jnp.concatenate([w_keys, wl_in[:, c_vb:c_g]], axis=1).T
        qg_t = jnp.tile(q_norm_g[l], WIDTH_A // HEAD_DIM_A).reshape(1, WIDTH_A)
        kg_t = jnp.tile(k_norm_g[l], WIDTH_A // HEAD_DIM_A).reshape(1, WIDTH_A)
        kg_col = k_norm_g[l].reshape(HEAD_DIM_A, 1)
        wl = w_lambda[l]
        sg = subln_g[l].reshape(1, 2 * HEAD_DIM_A)
        wpa, wpb, wo = w_proj_a[l].astype(BF16), w_proj_b[l].astype(BF16), w_out[l].astype(BF16)
        g_ffn = ffn_norm_g[l].reshape(1, d)
        wup, wdn = w_up[l].astype(BF16), w_down[l].astype(BF16)
        cw, cb = conv_w[l], conv_b[l].reshape(1, ff)

        (qa, kat, katb, vai, vab, qb, kbt, kbtb, vbt, vbb, gates) = _in_proj_prompt(
            xp, g_attn, w_std, w_t, qg_t, kg_col, gm, batch, seq, kv_prompt)
        kv_prompt = (kat, vai, kbt, vbt)
        oa = _da_prompt(qa, katb, vab, bias_prompt, wl, sg, batch, seq, lam_init)
        ob = _sb_prompt(qb, kbtb, vbb, u_hi_lo, batch, seq)
        xp = _merge_out(xp, oa, ob, gates, wpa, wpb, wo)
        xp, tail = _ffn_prompt(xp, g_ffn, wup, cw, cb, wdn, batch, seq)
        tails_p.append(tail[:, SUBLANES - (CONV_WIDTH - 1):, :])

        qa, ka, va, qb, kb, vb, gates = _in_proj_sample(xs, g_attn, w_std, w_keys, qg_t, kg_t, gm)
        oa = _da_sample(pt, qa, ka, va, blast, bnew, wl, sg, cda_kt, cda_vi, l, dec_b, dec_t, lam_init)
        ob = _sb_sample(pt, qb, kb, vb, u_hi_lo, csb_kt, csb_vt, l, dec_b, dec_t)
        xs = _merge_out(xs, oa, ob, gates, wpa, wpb, wo)
        st = state_conv[l]
        zeros = jnp.zeros((dec_b, dec_t - 1, ff), F32)
        s1 = jnp.concatenate([st[:, 1:2], zeros], axis=1).reshape(dec_b * dec_t, ff)
        s2 = jnp.concatenate([st, zeros[:, 1:]], axis=1).reshape(dec_b * dec_t, ff)
        xs, a_s = _ffn_sample(xs, g_ffn, wup, cw, cb, wdn, s1, s2, dec_t)
        new_s = (ka.reshape(dec_b, dec_t, N_HEADS_A, 2, HEAD_DIM_A),
                 va.reshape(dec_b, dec_t, N_HEADS_A, 2 * HEAD_DIM_A),
                 kb.reshape(dec_b, dec_t, N_HEADS_B, HEAD_DIM_B),
                 vb.reshape(dec_b, dec_t, N_HEADS_B, HEAD_DIM_B),
                 a_s.reshape(dec_b, dec_t, ff)[:, dec_t - (CONV_WIDTH - 1):, :])
        for lst, r in zip(rows_s, new_s):
            lst.append(r)

    kat, vai, kbt, vbt = kv_prompt
    out_p = [jnp.transpose(kat.reshape(depth, batch, N_HEADS_A, 2, HEAD_DIM_A, seq), (0, 1, 5, 2, 3, 4)),
             vai.reshape(depth, batch, seq, N_HEADS_A, 2 * HEAD_DIM_A),
             jnp.transpose(kbt.reshape(depth, batch, N_HEADS_B, HEAD_DIM_B, seq), (0, 1, 4, 2, 3)),
             jnp.transpose(vbt.reshape(depth, batch, N_HEADS_B, HEAD_DIM_B, seq), (0, 1, 4, 2, 3)),
             jnp.stack(tails_p)]
    out_s = [jnp.stack(r) for r in rows_s]
    return (xp.reshape(batch, seq, d), xs.reshape(dec_b, dec_t, d), *out_p, *out_s)
```

```python
import functools
import math

import numpy as np
import jax
import jax.numpy as jnp
from jax import lax
from jax.experimental import pallas as pl
from jax.experimental.pallas import tpu as pltpu

F32 = jnp.float32
BF16 = jnp.bfloat16

N_HEADS_A = 8
HEAD_DIM_A = 64
WIDTH_A = N_HEADS_A * 2 * HEAD_DIM_A
N_HEADS_B = 8
HEAD_DIM_B = 64
WIDTH_B = N_HEADS_B * HEAD_DIM_B
N_BUCKETS = 32
MAX_DISTANCE = 128
LAMBDA_BASE = 0.8
LAMBDA_SCALE = 0.6
LAMBDA_RATE = 0.3
NORM_EPS = 1e-6
NEG_BIG = -1e30
CONV_WIDTH = 3

LANES = 128
SUBLANES = 8
MXU_EDGE = 256
VMEM_LIMIT_BYTES = 56 * 1024 * 1024

TOKEN_TILE = 256
WIDE_TOKEN_TILE = 512
DA_TILE = 512
SB_TILE = MXU_EDGE
PAGES_PER_STEP = 16
SB_PAIRS_PER_STEP = 4
SB_NEAR_PAGES = 4
SB_FAR_PAGES = 14

LOG2E = math.log2(math.e)
STICK_UNDERFLOW = -104.0


def _cparams(sem):
    return pltpu.CompilerParams(dimension_semantics=sem, vmem_limit_bytes=VMEM_LIMIT_BYTES)


def _const_spec(shape):
    nd = len(shape)
    return pl.BlockSpec(shape, lambda *_: (0,) * nd)


_NT = (((1,), (1,)), ((), ()))


def _rms_rows(x, g):
    return x * lax.rsqrt(jnp.mean(x * x, axis=-1, keepdims=True) + NORM_EPS) * g


def _head_norm(y, gain, gm):
    sq = (y * y).astype(BF16)
    outs = []
    for j in range(WIDTH_A // MXU_EDGE):
        sl = slice(j * MXU_EDGE, (j + 1) * MXU_EDGE)
        ms = jnp.dot(sq[:, sl], gm, preferred_element_type=F32)
        outs.append(y[:, sl] * lax.rsqrt(ms + NORM_EPS))
    return jnp.concatenate(outs, axis=1) * gain


def _in_proj_prompt_kernel(x_ref, g_ref, ws_ref, wt_ref, qg_ref, kgc_ref, gm_ref, *refs, n_prev):
    prev_refs, refs = refs[:4 if n_prev else 0], refs[4 if n_prev else 0:]
    qa_ref, kat_ref, katb_ref, vai_ref, vab_ref, qb_ref, kbt_ref, kbtb_ref, vbt_ref, vbb_ref, gates_ref = refs
    for dst, src in zip((kat_ref, vai_ref, kbt_ref, vbt_ref), prev_refs):
        dst[0:n_prev] = src[...]
    tm = x_ref.shape[0]
    nb = _rms_rows(x_ref[...], g_ref[...]).astype(BF16)

    def seg(lo, width):
        return jnp.dot(nb, ws_ref[:, lo:lo + width], preferred_element_type=F32)

    def seg_t(lo, width):
        return lax.dot_general(wt_ref[lo:lo + width, :], nb, _NT, preferred_element_type=F32)

    qa = _head_norm(seg(0, WIDTH_A), qg_ref[...], gm_ref[...]) * (HEAD_DIM_A ** -0.5 * LOG2E)
    qa_ref[...] = qa.astype(BF16)
    va = seg(WIDTH_A, WIDTH_A)
    vab_ref[...] = va.astype(BF16)
    for h in range(N_HEADS_A):
        vai_ref[n_prev, pl.ds(h, tm, stride=N_HEADS_A), :] = va[:, h * LANES:(h + 1) * LANES]
    off = 2 * WIDTH_A
    qb_ref[...] = (seg(off, WIDTH_B) * (HEAD_DIM_B ** -0.5)).astype(BF16)
    vbb_ref[...] = seg(off + WIDTH_B, WIDTH_B).astype(BF16)
    gates_ref[...] = seg(off + 2 * WIDTH_B, gates_ref.shape[1])

    kat = seg_t(0, WIDTH_A).reshape(WIDTH_A // HEAD_DIM_A, HEAD_DIM_A, tm)
    ms = jnp.mean(kat * kat, axis=1, keepdims=True)
    kat = (kat * lax.rsqrt(ms + NORM_EPS) * kgc_ref[...].reshape(1, HEAD_DIM_A, 1)).reshape(WIDTH_A, tm)
    kat_ref[n_prev, 0] = kat
    katb_ref[0] = kat.astype(BF16)
    kbt = seg_t(WIDTH_A, WIDTH_B)
    kbt_ref[n_prev, 0] = kbt
    kbtb_ref[0] = kbt.astype(BF16)
    vbt_ref[n_prev, 0] = seg_t(WIDTH_A + WIDTH_B, WIDTH_B)


def _in_proj_prompt(x, g, ws, wt, qg_t, kg_col, gm, batch, seq, prev):
    n, d = x.shape
    tm = min(TOKEN_TILE, seq)
    tps = seq // tm
    n_prev = 0 if prev is None else prev[0].shape[0]
    gate_w = ws.shape[1] - 2 * WIDTH_A - 2 * WIDTH_B
    row = lambda w: pl.BlockSpec((tm, w), lambda i: (i, 0))
    t_blk = lambda layers, w: pl.BlockSpec((layers, 1, w, tm), lambda i: (0, i // tps, 0, i % tps))
    vi_blk = lambda layers: pl.BlockSpec((layers, tm * N_HEADS_A, LANES), lambda i: (0, i, 0))
    t_out = lambda w: t_blk(n_prev + 1, w)
    prev_specs = [] if prev is None else [t_blk(n_prev, WIDTH_A), vi_blk(n_prev),
                                          t_blk(n_prev, WIDTH_B), t_blk(n_prev, WIDTH_B)]
    da_tile, sb_tile = min(DA_TILE, seq), min(SB_TILE, seq)
    t_tile = lambda w, kt: pl.BlockSpec((1, w, tm), lambda i: (i // (kt // tm), 0, i % (kt // tm)))
    out_specs = [row(WIDTH_A), t_out(WIDTH_A), t_tile(WIDTH_A, da_tile),
                 vi_blk(n_prev + 1), row(WIDTH_A),
                 row(WIDTH_B), t_out(WIDTH_B), t_tile(WIDTH_B, sb_tile), t_out(WIDTH_B), row(WIDTH_B), row(gate_w)]
    layers = n_prev + 1
    out_shape = [jax.ShapeDtypeStruct((n, WIDTH_A), BF16),
                 jax.ShapeDtypeStruct((layers, batch, WIDTH_A, seq), F32),
                 jax.ShapeDtypeStruct((n // da_tile, WIDTH_A, da_tile), BF16),
                 jax.ShapeDtypeStruct((layers, n * N_HEADS_A, LANES), F32),
                 jax.ShapeDtypeStruct((n, WIDTH_A), BF16),
                 jax.ShapeDtypeStruct((n, WIDTH_B), BF16),
                 jax.ShapeDtypeStruct((layers, batch, WIDTH_B, seq), F32),
                 jax.ShapeDtypeStruct((n // sb_tile, WIDTH_B, sb_tile), BF16),
                 jax.ShapeDtypeStruct((layers, batch, WIDTH_B, seq), F32),
                 jax.ShapeDtypeStruct((n, WIDTH_B), BF16),
                 jax.ShapeDtypeStruct((n, gate_w), F32)]
    return pl.pallas_call(
        functools.partial(_in_proj_prompt_kernel, n_prev=n_prev),
        grid=(n // tm,),
        in_specs=[row(d), _const_spec(g.shape), _const_spec(ws.shape), _const_spec(wt.shape),
                  _const_spec(qg_t.shape), _const_spec(kg_col.shape), _const_spec(gm.shape)] + prev_specs,
        out_specs=out_specs,
        out_shape=out_shape,
        compiler_params=_cparams(("arbitrary",)),
        name="in_proj_prompt",
    )(x, g, ws, wt, qg_t, kg_col, gm, *(prev or ()))


def _in_proj_sample_kernel(x_ref, g_ref, ws_ref, wk_ref, qg_ref, kg_ref, gm_ref,
                           qa_ref, ka_ref, va_ref, qb_ref, kb_ref, vb_ref, gates_ref):
    nb = _rms_rows(x_ref[...], g_ref[...]).astype(BF16)

    def seg(w_ref, lo, width):
        return jnp.dot(nb, w_ref[:, lo:lo + width], preferred_element_type=F32)

    qa_ref[...] = _head_norm(seg(ws_ref, 0, WIDTH_A), qg_ref[...], gm_ref[...]) * (HEAD_DIM_A ** -0.5 * LOG2E)
    va_ref[...] = seg(ws_ref, WIDTH_A, WIDTH_A)
    off = 2 * WIDTH_A
    qb_ref[...] = seg(ws_ref, off, WIDTH_B) * (HEAD_DIM_B ** -0.5)
    vb_ref[...] = seg(ws_ref, off + WIDTH_B, WIDTH_B)
    gates_ref[...] = seg(ws_ref, off + 2 * WIDTH_B, gates_ref.shape[1])
    ka_ref[...] = _head_norm(seg(wk_ref, 0, WIDTH_A), kg_ref[...], gm_ref[...])
    kb_ref[...] = seg(wk_ref, WIDTH_A, WIDTH_B)


def _in_proj_sample(x, g, ws, wk, qg_t, kg_t, gm):
    n, d = x.shape
    tm = min(TOKEN_TILE, n)
    gate_w = ws.shape[1] - 2 * WIDTH_A - 2 * WIDTH_B
    row = lambda w: pl.BlockSpec((tm, w), lambda i: (i, 0))
    widths = [WIDTH_A, WIDTH_A, WIDTH_A, WIDTH_B, WIDTH_B, WIDTH_B, gate_w]
    return pl.pallas_call(
        _in_proj_sample_kernel,
        grid=(n // tm,),
        in_specs=[row(d), _const_spec(g.shape), _const_spec(ws.shape), _const_spec(wk.shape),
                  _const_spec(qg_t.shape), _const_spec(kg_t.shape), _const_spec(gm.shape)],
        out_specs=[row(w) for w in widths],
        out_shape=[jax.ShapeDtypeStruct((n, w), F32) for w in widths],
        compiler_params=_cparams(("arbitrary",)),
        name="in_proj_sample",
    )(x, g, ws, wk, qg_t, kg_t, gm)


def _lambda_value(wl, lam_init):
    a = jnp.sum(wl[0:1] * wl[1:2], axis=1, keepdims=True)
    b = jnp.sum(wl[2:3] * wl[3:4], axis=1, keepdims=True)
    return jnp.exp(a) - jnp.exp(b) + lam_init


def _sub_norm(d, gain, lam_init):
    y = d * lax.rsqrt(jnp.mean(d * d, axis=-1, keepdims=True) + NORM_EPS)
    return y * gain * (1.0 - lam_init)


def _softmax_probs(s, m_sc, l_sc):
    m_prev = m_sc[...]
    m_new = jnp.maximum(m_prev, jnp.max(s, axis=-1, keepdims=True))
    alpha = jnp.exp2(m_prev - m_new)
    p = jnp.exp2(s - m_new)
    l_sc[...] = alpha * l_sc[...] + jnp.sum(p, axis=-1, keepdims=True)
    m_sc[...] = m_new
    return alpha, p.astype(BF16)


def _lane_tile(x, width):
    reps = width // x.shape[1]
    return x if reps == 1 else jnp.concatenate([x] * reps, axis=1)


def _stick_weights(z, u_hi_lo, carry_sc, valid):
    width = z.shape[1]
    w = jnp.maximum(z, 0.0) + jnp.log(1.0 + jnp.exp(-jnp.abs(z)))
    if valid is not None:
        w = jnp.where(valid, w, 0.0)
    hi = w.astype(BF16)
    lo = (w - hi.astype(F32)).astype(BF16)
    later = jnp.dot(jnp.concatenate([hi, lo], axis=1), u_hi_lo, preferred_element_type=F32)
    carry = carry_sc[...]
    a = jnp.exp(z - w + later + _lane_tile(carry, width))
    if valid is not None:
        a = jnp.where(valid, a, 0.0)
    carry_sc[...] = carry - jnp.sum(w, axis=-1, keepdims=True)
    return a.astype(BF16)


def _split_lanes(q):
    lane = lax.broadcasted_iota(jnp.int32, q.shape, 1)
    zero = jnp.zeros_like(q)
    half = LANES // 2
    return jnp.concatenate([jnp.where(lane < half, q, zero), jnp.where(lane >= half, q, zero)], axis=0)


def _da_prompt_kernel(q_ref, kt_ref, v_ref, bias_ref, wl_ref, sg_ref, o_ref, m_sc, acc_sc,
                      *, tq, lam_init):
    ones = jnp.ones((tq, LANES), BF16)
    lam = _lambda_value(wl_ref[...], lam_init)

    def query_tile(qi, carry):
        rows = pl.ds(pl.multiple_of(qi * tq, tq), tq)
        qq = _split_lanes(q_ref[rows, :])
        m_sc[...] = jnp.full(m_sc.shape, NEG_BIG, F32)
        acc_sc[...] = jnp.zeros(acc_sc.shape, F32)

        def block(ki, bias):
            v_ext = jnp.concatenate([v_ref[pl.ds(pl.multiple_of(ki * tq, tq), tq), :], ones], axis=1)
            s = jnp.dot(qq, kt_ref[ki], preferred_element_type=F32)
            if bias is not None:
                s = (s.reshape(2, tq, tq) + bias[None]).reshape(2 * tq, tq)
            m_prev = m_sc[...]
            m_new = jnp.maximum(m_prev, jnp.max(s, axis=-1, keepdims=True))
            alpha = jnp.exp2(m_prev - m_new)
            p = jnp.concatenate([jnp.exp2(s[:, j * LANES:(j + 1) * LANES] - m_new).astype(BF16)
                                 for j in range(tq // LANES)], axis=1)
            acc_sc[...] = (_lane_tile(alpha, 2 * LANES) * acc_sc[...]
                           + jnp.dot(p, v_ext, preferred_element_type=F32))
            m_sc[...] = m_new

        def far(ki, c):
            block(ki, None)
            return c

        lax.fori_loop(0, qi - 1, far, 0)

        @pl.when(qi >= 1)
        def _():
            block(qi - 1, bias_ref[0, 0])

        block(qi, bias_ref[0, 1])

        acc = acc_sc[...]
        o = acc[:, :LANES] / acc[:, LANES:]
        d = o[:tq] - lam * o[tq:]
        o_ref[rows, :] = _sub_norm(d, sg_ref[...], lam_init).astype(o_ref.dtype)
        return carry

    lax.fori_loop(0, q_ref.shape[0] // tq, query_tile, 0)


def _da_prompt(q, kt, v, bias, wl, sg, batch, seq, lam_init):
    n = q.shape[0]
    tq = kt.shape[2]
    nq = seq // tq
    kern = functools.partial(_da_prompt_kernel, tq=tq, lam_init=lam_init)
    head = pl.BlockSpec((seq, LANES), lambda b, h: (b, h))
    return pl.pallas_call(
        kern,
        grid=(batch, N_HEADS_A),
        in_specs=[head, pl.BlockSpec((nq, LANES, tq), lambda b, h: (b, h, 0)), head,
                  pl.BlockSpec((1, 2, tq, tq), lambda b, h: (h, 0, 0, 0)),
                  _const_spec(wl.shape), _const_spec(sg.shape)],
        out_specs=head,
        out_shape=jax.ShapeDtypeStruct((n, WIDTH_A), BF16),
        scratch_shapes=[pltpu.VMEM((2 * tq, LANES), F32), pltpu.VMEM((2 * tq, 2 * LANES), F32)],
        compiler_params=_cparams(("arbitrary", "arbitrary")),
        name="da_prompt",
    )(q, kt, v, bias, wl, sg)


def _sb_prompt_kernel(q_ref, kt_ref, v_ref, u_ref, o_ref, carry_sc, acc_sc, *, tq, pp):
    qi = pl.program_id(2)
    q = q_ref[...]
    qqs = [_split_lanes(q[:, p * LANES:(p + 1) * LANES]) for p in range(pp)]
    carry_sc[...] = jnp.zeros(carry_sc.shape, F32)
    acc_sc[...] = jnp.zeros(acc_sc.shape, F32)

    def block(ki, valid):
        k0 = pl.multiple_of(ki * tq, tq)
        kt = kt_ref[ki]
        top = None
        zs = [jnp.dot(qqs[p], kt[p * LANES:(p + 1) * LANES, :], preferred_element_type=F32) for p in range(pp)]
        weights = [_stick_weights(zs[p], u_ref[...], carry_sc.at[p], valid) for p in range(pp)]
        for p in range(pp):
            lanes = slice(p * LANES, (p + 1) * LANES)
            acc_sc[p] = acc_sc[p] + jnp.dot(weights[p], v_ref[pl.ds(k0, tq), lanes], preferred_element_type=F32)
            c = jnp.max(carry_sc[p])
            top = c if top is None else jnp.maximum(top, c)
        return top

    row = lax.broadcasted_iota(jnp.int32, (2, tq, tq), 1).reshape(2 * tq, tq)
    col = lax.broadcasted_iota(jnp.int32, (2 * tq, tq), 1)
    top = block(qi, col < row)

    def more(c):
        return jnp.logical_and(c[0] < qi, c[1] > STICK_UNDERFLOW)

    def left(c):
        return c[0] + 1, block(qi - 1 - c[0], None)

    lax.while_loop(more, left, (jnp.int32(0), top))

    lane = lax.broadcasted_iota(jnp.int32, (tq, LANES), 1)
    for p in range(pp):
        acc = acc_sc[p]
        o_ref[:, p * LANES:(p + 1) * LANES] = jnp.where(lane < LANES // 2, acc[:tq], acc[tq:]).astype(o_ref.dtype)


def _sb_prompt(q, kt, v, u_hi_lo, batch, seq):
    n = q.shape[0]
    tq = kt.shape[2]
    nq = seq // tq
    pp = SB_PAIRS_PER_STEP
    width = pp * LANES
    kern = functools.partial(_sb_prompt_kernel, tq=tq, pp=pp)
    return pl.pallas_call(
        kern,
        grid=(batch, WIDTH_B // width, nq),
        in_specs=[pl.BlockSpec((tq, width), lambda b, h, i: (b * nq + i, h)),
                  pl.BlockSpec((nq, width, tq), lambda b, h, i: (b, h, 0)),
                  pl.BlockSpec((seq, width), lambda b, h, i: (b, h)),
                  _const_spec(u_hi_lo.shape)],
        out_specs=pl.BlockSpec((tq, width), lambda b, h, i: (b * nq + i, h)),
        out_shape=jax.ShapeDtypeStruct((n, WIDTH_B), BF16),
        scratch_shapes=[pltpu.VMEM((pp, 2 * tq, LANES), F32), pltpu.VMEM((pp, 2 * tq, LANES), F32)],
        compiler_params=_cparams(("arbitrary", "arbitrary", "arbitrary")),
        name="sb_prompt",
    )(q, kt, v, u_hi_lo)


def _block_diag_queries(q, groups, width):
    t = q.shape[0]
    rep = jnp.concatenate([q] * groups, axis=0)
    r = lax.broadcasted_iota(jnp.int32, rep.shape, 0)
    c = lax.broadcasted_iota(jnp.int32, rep.shape, 1)
    keep = (c // width) == (r // t)
    return jnp.where(keep, rep, 0.0).astype(BF16)


def _pad_rows(x, rows):
    return jnp.concatenate([x, jnp.zeros((rows - x.shape[0], x.shape[1]), x.dtype)], axis=0)


def _da_sample_kernel(pt_ref, q_ref, knew_ref, vnew_ref, blast_ref, bnew_ref, wl_ref, sg_ref, *rest,
                      pages, page, lam_init):
    del pt_ref
    kt_refs, v_refs = rest[:pages], rest[pages:2 * pages]
    o_ref, qbd_sc, m_sc, l_sc, acc_sc = rest[2 * pages:]
    g = pl.program_id(1)
    last = pl.num_programs(1) - 1
    t = q_ref.shape[0]
    rows_h = 2 * t

    @pl.when(g == 0)
    def _():
        qbd_sc[...] = _block_diag_queries(q_ref[...], 2 * N_HEADS_A, HEAD_DIM_A)
        m_sc[...] = jnp.full(m_sc.shape, NEG_BIG, F32)
        l_sc[...] = jnp.zeros(l_sc.shape, F32)
        acc_sc[...] = jnp.zeros(acc_sc.shape, F32)

    qbd = qbd_sc[...]
    ktcat = jnp.concatenate([r[0, 0].astype(BF16) for r in kt_refs], axis=1)
    s = jnp.dot(qbd, ktcat, preferred_element_type=F32)
    s = s + blast_ref[...] * (g == last).astype(F32)
    alpha, p = _softmax_probs(s, m_sc, l_sc)
    for h in range(N_HEADS_A):
        vh = jnp.concatenate([r[0, 0, pl.ds(h, page, stride=N_HEADS_A), :] for r in v_refs], axis=0)
        rs = slice(h * rows_h, (h + 1) * rows_h)
        acc_sc[rs, :] = alpha[rs] * acc_sc[rs, :] + jnp.dot(p[rs], vh.astype(BF16), preferred_element_type=F32)

    @pl.when(g == last)
    def _():
        kn = _pad_rows(knew_ref[...], page).astype(BF16)
        vn = _pad_rows(vnew_ref[...], page).astype(BF16)
        s2 = lax.dot_general(qbd, kn, _NT, preferred_element_type=F32) + bnew_ref[...]
        alpha2, p2 = _softmax_probs(s2, m_sc, l_sc)
        o_all = jnp.dot(p2, vn, preferred_element_type=F32)
        inv_l = 1.0 / l_sc[...]
        lam = _lambda_value(wl_ref[...], lam_init)
        outs = []
        for h in range(N_HEADS_A):
            rs = slice(h * rows_h, (h + 1) * rows_h)
            cols = slice(h * 2 * HEAD_DIM_A, (h + 1) * 2 * HEAD_DIM_A)
            o = (alpha2[rs] * acc_sc[rs, :] + o_all[rs, cols]) * inv_l[rs]
            outs.append(_sub_norm(o[:t] - lam * o[t:], sg_ref[...], lam_init))
        o_ref[...] = jnp.concatenate(outs, axis=1)


def _page_specs(layer, n_pages, pages, block, reverse):
    specs = []
    for j in range(pages):
        if reverse:
            idx = lambda b, g, pt, j=j: (layer, pt[b * n_pages + n_pages - (g + 1) * pages + j], 0, 0)
        else:
            idx = lambda b, g, pt, j=j: (layer, pt[b * n_pages + g * pages + j], 0, 0)
        specs.append(pl.BlockSpec((1, 1) + block, idx))
    return specs


def _da_sample(pt, q, knew, vnew, blast, bnew, wl, sg, cache_kt, cache_vi, layer, dec_b, dec_t, lam_init):
    n_pages = pt.shape[0] // dec_b
    page = cache_kt.shape[3]
    pages = min(PAGES_PER_STEP, n_pages)
    rows = 2 * N_HEADS_A * dec_t
    tok = lambda w: pl.BlockSpec((dec_t, w), lambda b, g, pt: (b, 0))
    const = lambda a: pl.BlockSpec(a.shape, lambda b, g, pt: (0,) * a.ndim)
    kern = functools.partial(_da_sample_kernel, pages=pages, page=page, lam_init=lam_init)
    grid_spec = pltpu.PrefetchScalarGridSpec(
        num_scalar_prefetch=1,
        grid=(dec_b, n_pages // pages),
        in_specs=[tok(WIDTH_A), tok(WIDTH_A), tok(WIDTH_A), const(blast), const(bnew), const(wl), const(sg)]
        + _page_specs(layer, n_pages, pages, (WIDTH_A, page), False)
        + _page_specs(layer, n_pages, pages, (page * N_HEADS_A, 2 * HEAD_DIM_A), False),
        out_specs=tok(WIDTH_A),
        scratch_shapes=[pltpu.VMEM((rows, WIDTH_A), BF16), pltpu.VMEM((rows, 1), F32),
                        pltpu.VMEM((rows, 1), F32), pltpu.VMEM((rows, 2 * HEAD_DIM_A), F32)],
    )
    return pl.pallas_call(
        kern,
        grid_spec=grid_spec,
        out_shape=jax.ShapeDtypeStruct((dec_b * dec_t, WIDTH_A), F32),
        compiler_params=_cparams(("arbitrary", "arbitrary")),
        name="da_sample",
    )(pt, q, knew, vnew, blast, bnew, wl, sg, *([cache_kt] * pages), *([cache_vi] * pages))


def _sb_sweep_pages(qbd, kt_refs, vt_refs, u_ref, carry_ref, acc_ref):
    blk = u_ref.shape[1]
    ktcat = jnp.concatenate([r[0, 0].astype(BF16) for r in kt_refs], axis=1)
    vtcat = jnp.concatenate([r[0, 0].astype(BF16) for r in vt_refs], axis=1)
    z = jnp.dot(qbd, ktcat, preferred_element_type=F32)
    for b in reversed(range(ktcat.shape[1] // blk)):
        sl = slice(b * blk, (b + 1) * blk)
        a = _stick_weights(z[:, sl], u_ref[...], carry_ref, None)
        acc_ref[...] = acc_ref[...] + lax.dot_general(a, vtcat[:, sl], _NT, preferred_element_type=F32)


def _sb_near_kernel(pt_ref, q_ref, knew_ref, vnew_ref, u_ref, *rest, pages, page):
    del pt_ref
    kt_refs, vt_refs = rest[:pages], rest[pages:2 * pages]
    acc_ref, carry_ref, o_ref = rest[2 * pages:]
    t = q_ref.shape[0]
    qbd = _block_diag_queries(q_ref[...], N_HEADS_B, HEAD_DIM_B)
    carry_ref[...] = jnp.zeros(carry_ref.shape, F32)
    kn = _pad_rows(knew_ref[...], page).astype(BF16)
    vn = _pad_rows(vnew_ref[...], page).astype(BF16)
    z = lax.dot_general(qbd, kn, _NT, preferred_element_type=F32)
    row_t = lax.broadcasted_iota(jnp.int32, z.shape, 0) % t
    col = lax.broadcasted_iota(jnp.int32, z.shape, 1)
    u_page = jnp.concatenate([u_ref[0:page, 0:page], u_ref[0:page, 0:page]], axis=0)
    a = _stick_weights(z, u_page, carry_ref, col < row_t)
    acc_ref[...] = jnp.dot(a, vn, preferred_element_type=F32)
    _sb_sweep_pages(qbd, kt_refs, vt_refs, u_ref, carry_ref, acc_ref)
    o_ref[...] = _own_head_columns(acc_ref[...], t)


def _own_head_columns(acc, t):
    col_h = lax.broadcasted_iota(jnp.int32, (t, WIDTH_B), 1) // HEAD_DIM_B
    out = jnp.zeros((t, WIDTH_B), F32)
    for h in range(N_HEADS_B):
        out = out + jnp.where(col_h == h, acc[h * t:(h + 1) * t, :], 0.0)
    return out


def _sb_far_kernel(pt_ref, need_ref, q_ref, acc_in_ref, carry_in_ref, u_ref, *rest, pages):
    del pt_ref
    kt_refs, vt_refs = rest[:pages], rest[pages:2 * pages]
    o_ref, qbd_sc, carry_sc, acc_sc = rest[2 * pages:]
    b = pl.program_id(0)
    g = pl.program_id(1)
    last = pl.num_programs(1) - 1
    t = q_ref.shape[0]

    @pl.when(g == 0)
    def _():
        qbd_sc[...] = _block_diag_queries(q_ref[...], N_HEADS_B, HEAD_DIM_B)
        carry_sc[...] = carry_in_ref[...]
        acc_sc[...] = acc_in_ref[...]

    @pl.when(need_ref[b] > 0)
    def _():
        _sb_sweep_pages(qbd_sc[...], kt_refs, vt_refs, u_ref, carry_sc, acc_sc)

    @pl.when(g == last)
    def _():
        o_ref[...] = _own_head_columns(acc_sc[...], t)


def _sb_sample(pt, q, knew, vnew, u_hi_lo, cache_kt, cache_vt, layer, dec_b, dec_t):
    n_pages = pt.shape[0] // dec_b
    page = cache_kt.shape[3]
    near = min(SB_NEAR_PAGES, n_pages)
    n_far = n_pages - near
    assert n_far > 0, "the far sweep also does the final head extraction"
    far = max(p for p in range(1, SB_FAR_PAGES + 1) if n_far % p == 0)
    rows = N_HEADS_B * dec_t
    blk = (WIDTH_B, page)

    tok1 = lambda w: pl.BlockSpec((dec_t, w), lambda b, pt: (b, 0))
    state1 = lambda w: pl.BlockSpec((rows, w), lambda b, pt: (b, 0))
    near_specs = [pl.BlockSpec((1, 1) + blk, lambda b, pt, j=j: (layer, pt[b * n_pages + n_far + j], 0, 0))
                  for j in range(near)]
    acc, carry, o_near = pl.pallas_call(
        functools.partial(_sb_near_kernel, pages=near, page=page),
        grid_spec=pltpu.PrefetchScalarGridSpec(
            num_scalar_prefetch=1,
            grid=(dec_b,),
            in_specs=[tok1(WIDTH_B), tok1(WIDTH_B), tok1(WIDTH_B),
                      pl.BlockSpec(u_hi_lo.shape, lambda b, pt: (0, 0))] + near_specs + near_specs,
            out_specs=[state1(WIDTH_B), state1(LANES), tok1(WIDTH_B)],
        ),
        out_shape=[jax.ShapeDtypeStruct((dec_b * rows, WIDTH_B), F32),
                   jax.ShapeDtypeStruct((dec_b * rows, LANES), F32),
                   jax.ShapeDtypeStruct((dec_b * dec_t, WIDTH_B), F32)],
        compiler_params=_cparams(("arbitrary",)),
        name="sb_sample_near",
    )(pt, q, knew, vnew, u_hi_lo, *([cache_kt] * near), *([cache_vt] * near))

    need = (jnp.max(carry.reshape(dec_b, rows * LANES), axis=1) > STICK_UNDERFLOW).astype(jnp.int32)
    tok2 = lambda w: pl.BlockSpec((dec_t, w), lambda b, g, pt, nd: (b, 0))
    state2 = lambda w: pl.BlockSpec((rows, w), lambda b, g, pt, nd: (b, 0))

    def far_spec(j):
        def idx(b, g, pt, nd):
            return (layer, jnp.where(nd[b] > 0, pt[b * n_pages + n_far - (g + 1) * far + j], 0), 0, 0)
        return pl.BlockSpec((1, 1) + blk, idx)

    far_specs = [far_spec(j) for j in range(far)]

    def sweep_far():
        return pl.pallas_call(
            functools.partial(_sb_far_kernel, pages=far),
            grid_spec=pltpu.PrefetchScalarGridSpec(
                num_scalar_prefetch=2,
                grid=(dec_b, n_far // far),
                in_specs=[tok2(WIDTH_B), state2(WIDTH_B), state2(LANES),
                          pl.BlockSpec(u_hi_lo.shape, lambda b, g, pt, nd: (0, 0))] + far_specs + far_specs,
                out_specs=tok2(WIDTH_B),
                scratch_shapes=[pltpu.VMEM((rows, WIDTH_B), BF16), pltpu.VMEM((rows, LANES), F32),
                                pltpu.VMEM((rows, WIDTH_B), F32)],
            ),
            out_shape=jax.ShapeDtypeStruct((dec_b * dec_t, WIDTH_B), F32),
            compiler_params=_cparams(("arbitrary", "arbitrary")),
            name="sb_sample_far",
        )(pt, need, q, acc, carry, u_hi_lo, *([cache_kt] * far), *([cache_vt] * far))

    return lax.cond(jnp.any(need > 0), sweep_far, lambda: o_near)


def _merge_out_kernel(x_ref, oa_ref, ob_ref, gates_ref, wpa_ref, wpb_ref, wo_ref, o_ref):
    d = x_ref.shape[1]
    ga = jax.nn.sigmoid(gates_ref[:, :d])
    gb = jax.nn.sigmoid(gates_ref[:, d:])
    pa = jnp.dot(oa_ref[...].astype(BF16), wpa_ref[...], preferred_element_type=F32)
    pb = jnp.dot(ob_ref[...].astype(BF16), wpb_ref[...], preferred_element_type=F32)
    merged = ga * pa + gb * pb
    o_ref[...] = x_ref[...] + jnp.dot(merged.astype(BF16), wo_ref[...], preferred_element_type=F32)


def _merge_out(x, oa, ob, gates, wpa, wpb, wo):
    n, d = x.shape
    tm = min(WIDE_TOKEN_TILE, n)
    row = lambda w: pl.BlockSpec((tm, w), lambda i: (i, 0))
    return pl.pallas_call(
        _merge_out_kernel,
        grid=(n // tm,),
        in_specs=[row(d), row(WIDTH_A), row(WIDTH_B), row(2 * d),
                  _const_spec(wpa.shape), _const_spec(wpb.shape), _const_spec(wo.shape)],
        out_specs=row(d),
        out_shape=jax.ShapeDtypeStruct((n, d), F32),
        compiler_params=_cparams(("arbitrary",)),
        name="merge_out",
    )(x, oa, ob, gates, wpa, wpb, wo)


def _gelu_exact(x):
    return 0.5 * x * (1.0 + lax.erf(x * np.float32(math.sqrt(0.5))))


def _ffn_up(x_ref, g_ref, wup_ref, ff):
    x = x_ref[...]
    n2 = _rms_rows(x, g_ref[...]).astype(BF16)
    a = jnp.dot(n2, wup_ref[:, :ff], preferred_element_type=F32)
    b = jnp.dot(n2, wup_ref[:, ff:], preferred_element_type=F32)
    return x, a, b


def _ffn_down(x, a_m1, a_m2, a, b, cw_ref, cb_ref, wdn_ref):
    conv = cb_ref[...] + a_m2 * cw_ref[0:1, :] + a_m1 * cw_ref[1:2, :] + a * cw_ref[2:3, :]
    h = (_gelu_exact(conv) * b).astype(BF16)
    return x + jnp.dot(h, wdn_ref[...], preferred_element_type=F32)


def _ffn_prompt_kernel(x_ref, g_ref, wup_ref, cw_ref, cb_ref, wdn_ref, y_ref, tail_ref, carry_sc,
                       *, tiles_per_seq):
    ff = wdn_ref.shape[0]
    tm = x_ref.shape[0]
    x, a, b = _ffn_up(x_ref, g_ref, wup_ref, ff)
    first = (pl.program_id(0) % tiles_per_seq) == 0
    prev = jnp.where(first, 0.0, carry_sc[...])
    p6 = prev[SUBLANES - 2:SUBLANES - 1, :]
    p7 = prev[SUBLANES - 1:SUBLANES, :]
    row = lax.broadcasted_iota(jnp.int32, a.shape, 0)
    a_m1 = jnp.where(row == 0, p7, pltpu.roll(a, 1, 0))
    a_m2 = jnp.where(row == 0, p6, jnp.where(row == 1, p7, pltpu.roll(a, 2, 0)))
    last_rows = a[tm - SUBLANES:, :]
    carry_sc[...] = last_rows
    tail_ref[0] = last_rows
    y_ref[...] = _ffn_down(x, a_m1, a_m2, a, b, cw_ref, cb_ref, wdn_ref)


def _ffn_sample_kernel(x_ref, g_ref, wup_ref, cw_ref, cb_ref, wdn_ref, s1_ref, s2_ref, y_ref, a_ref,
                       *, dec_t):
    ff = wdn_ref.shape[0]
    x, a, b = _ffn_up(x_ref, g_ref, wup_ref, ff)
    t = lax.broadcasted_iota(jnp.int32, a.shape, 0) % dec_t
    a_m1 = jnp.where(t >= 1, pltpu.roll(a, 1, 0), s1_ref[...])
    a_m2 = jnp.where(t >= 2, pltpu.roll(a, 2, 0), s2_ref[...])
    a_ref[...] = a
    y_ref[...] = _ffn_down(x, a_m1, a_m2, a, b, cw_ref, cb_ref, wdn_ref)


def _ffn_prompt(x, g, wup, cw, cb, wdn, batch, seq):
    n, d = x.shape
    ff = wdn.shape[0]
    tm = min(WIDE_TOKEN_TILE, seq)
    tiles_per_seq = seq // tm
    row = lambda w: pl.BlockSpec((tm, w), lambda i: (i, 0))
    kern = functools.partial(_ffn_prompt_kernel, tiles_per_seq=tiles_per_seq)
    return pl.pallas_call(
        kern,
        grid=(n // tm,),
        in_specs=[row(d), _const_spec(g.shape), _const_spec(wup.shape), _const_spec(cw.shape),
                  _const_spec(cb.shape), _const_spec(wdn.shape)],
        out_specs=[row(d), pl.BlockSpec((1, SUBLANES, ff), lambda i: (i // tiles_per_seq, 0, 0))],
        out_shape=[jax.ShapeDtypeStruct((n, d), F32), jax.ShapeDtypeStruct((batch, SUBLANES, ff), F32)],
        scratch_shapes=[pltpu.VMEM((SUBLANES, ff), F32)],
        compiler_params=_cparams(("arbitrary",)),
        name="ffn_prompt",
    )(x, g, wup, cw, cb, wdn)


def _ffn_sample(x, g, wup, cw, cb, wdn, s1, s2, dec_t):
    n, d = x.shape
    ff = wdn.shape[0]
    tm = min(TOKEN_TILE, n)
    row = lambda w: pl.BlockSpec((tm, w), lambda i: (i, 0))
    kern = functools.partial(_ffn_sample_kernel, dec_t=dec_t)
    return pl.pallas_call(
        kern,
        grid=(n // tm,),
        in_specs=[row(d), _const_spec(g.shape), _const_spec(wup.shape), _const_spec(cw.shape),
                  _const_spec(cb.shape), _const_spec(wdn.shape), row(ff), row(ff)],
        out_specs=[row(d), row(ff)],
        out_shape=[jax.ShapeDtypeStruct((n, d), F32), jax.ShapeDtypeStruct((n, ff), F32)],
        compiler_params=_cparams(("arbitrary",)),
        name="ffn_sample",
    )(x, g, wup, cw, cb, wdn, s1, s2)


def _bucket_table(max_dist):
    n = np.arange(max_dist + 1, dtype=np.int32)
    max_exact = N_BUCKETS // 2
    nf = np.maximum(n, 1).astype(np.float32)
    large = max_exact + (np.log(nf / np.float32(max_exact)) / np.float32(math.log(MAX_DISTANCE / max_exact))
                         * np.float32(N_BUCKETS - max_exact)).astype(np.int32)
    large = np.minimum(large, N_BUCKETS - 1)
    return np.where(n < max_exact, n, large).astype(np.int32)


def _bias_kernel(rs_ref, prompt_ref, last_ref, new_ref, *, upper, tq, dec_t, page):
    h = pl.program_id(0)

    def lookup(dist):
        val = jnp.zeros(dist.shape, F32) + rs_ref[N_BUCKETS - 1, h]
        for b in range(N_BUCKETS - 2, -1, -1):
            val = jnp.where(dist <= upper[b], rs_ref[b, h], val)
        return jnp.where(dist >= 0, val, NEG_BIG)

    ql = lax.broadcasted_iota(jnp.int32, (tq, tq), 0)
    kl = lax.broadcasted_iota(jnp.int32, (tq, tq), 1)
    prompt_ref[0, 0] = lookup(ql - kl + tq)
    prompt_ref[0, 1] = lookup(ql - kl)
    t = lax.broadcasted_iota(jnp.int32, (2 * dec_t, page), 0) % dec_t
    j = lax.broadcasted_iota(jnp.int32, (2 * dec_t, page), 1)
    last_ref[...] = lookup(page + t - j)
    new_ref[...] = lookup(t - j)


def _bias_tiles(rel_shift, tq, dec_t, page):
    table = _bucket_table(2 * MAX_DISTANCE)
    upper = tuple(int(np.nonzero(table <= b)[0].max()) for b in range(N_BUCKETS - 1))
    heads = rel_shift.shape[1]
    kern = functools.partial(_bias_kernel, upper=upper, tq=tq, dec_t=dec_t, page=page)
    return pl.pallas_call(
        kern,
        grid=(heads,),
        in_specs=[pl.BlockSpec(memory_space=pltpu.SMEM)],
        out_specs=[pl.BlockSpec((1, 2, tq, tq), lambda h: (h, 0, 0, 0)),
                   pl.BlockSpec((2 * dec_t, page), lambda h: (h, 0)),
                   pl.BlockSpec((2 * dec_t, page), lambda h: (h, 0))],
        out_shape=[jax.ShapeDtypeStruct((heads, 2, tq, tq), F32),
                   jax.ShapeDtypeStruct((heads * 2 * dec_t, page), F32),
                   jax.ShapeDtypeStruct((heads * 2 * dec_t, page), F32)],
        compiler_params=_cparams(("arbitrary",)),
        name="bias_tiles",
    )(rel_shift)


def _lower_triangles(w):
    j = np.arange(w)[:, None]
    s = np.arange(w)[None, :]
    u = -(j > s).astype(np.float32)
    return jnp.asarray(np.concatenate([u, u], axis=0), dtype=BF16)


def _group_mean_matrix():
    i = np.arange(MXU_EDGE)
    m = (i[:, None] // HEAD_DIM_A == i[None, :] // HEAD_DIM_A).astype(np.float32) / HEAD_DIM_A
    return jnp.asarray(m, dtype=BF16)


def kernel(x_prompt, x_sample, cache_da_k, cache_da_v, cache_sb_k, cache_sb_v, state_conv, page_table,
           rel_bias, attn_norm_g, w_in, q_norm_g, k_norm_g, w_lambda, subln_g, w_proj_a, w_proj_b, w_out,
           ffn_norm_g, w_up, conv_w, conv_b, w_down):
    batch, seq, d = x_prompt.shape
    dec_b, dec_t, _ = x_sample.shape
    depth = w_in.shape[0]
    n_pool, page = cache_da_k.shape[1], cache_da_k.shape[2]
    n_pages = page_table.shape[1]
    past_len = n_pages * page
    ff = w_down.shape[1]

    pt = page_table.reshape(-1).astype(jnp.int32)
    cda_kt = jnp.swapaxes(cache_da_k.reshape(depth, n_pool, page, WIDTH_A), 2, 3)
    cda_vi = cache_da_v.reshape(depth, n_pool, page * N_HEADS_A, 2 * HEAD_DIM_A)
    csb_kt = jnp.swapaxes(cache_sb_k.reshape(depth, n_pool, page, WIDTH_B), 2, 3)
    csb_vt = jnp.swapaxes(cache_sb_v.reshape(depth, n_pool, page, WIDTH_B), 2, 3)

    rel_shift = ((rel_bias - rel_bias[N_BUCKETS - 1]) * LOG2E).astype(F32)
    pages = min(PAGES_PER_STEP, n_pages)
    bias_prompt, blast, bnew = _bias_tiles(rel_shift, min(DA_TILE, seq), dec_t, page)
    blast = jnp.pad(blast, ((0, 0), ((pages - 1) * page, 0)))

    u_hi_lo = _lower_triangles(min(SB_TILE, seq))
    gm = _group_mean_matrix()

    c_qa, c_ka, c_va = 0, WIDTH_A, 2 * WIDTH_A
    c_qb, c_kb, c_vb, c_g = 3 * WIDTH_A, 3 * WIDTH_A + WIDTH_B, 3 * WIDTH_A + 2 * WIDTH_B, 3 * WIDTH_A + 3 * WIDTH_B

    xp = x_prompt.reshape(batch * seq, d)
    xs = x_sample.reshape(dec_b * dec_t, d)
    kv_prompt, tails_p = None, []
    rows_s = [[] for _ in range(5)]
    for l in range(depth):
        lam_init = LAMBDA_BASE - LAMBDA_SCALE * math.exp(-LAMBDA_RATE * l)
        g_attn = attn_norm_g[l].reshape(1, d)
        wl_in = w_in[l].astype(BF16)
        w_std = jnp.concatenate([wl_in[:, c_qa:c_ka], wl_in[:, c_va:c_qb], wl_in[:, c_qb:c_kb],
                                 wl_in[:, c_vb:c_g], wl_in[:, c_g:]], axis=1)
        w_keys = jnp.concatenate([wl_in[:, c_ka:c_va], wl_in[:, c_kb:c_vb]], axis=1)
        w_t = jnp.concatenate([w_keys, wl_in[:, c_vb:c_g]], axis=1).T
        qg_t = jnp.tile(q_norm_g[l], WIDTH_A // HEAD_DIM_A).reshape(1, WIDTH_A)
        kg_t = jnp.tile(k_norm_g[l], WIDTH_A // HEAD_DIM_A).reshape(1, WIDTH_A)
        kg_col = k_norm_g[l].reshape(HEAD_DIM_A, 1)
        wl = w_lambda[l]
        sg = subln_g[l].reshape(1, 2 * HEAD_DIM_A)
        wpa, wpb, wo = w_proj_a[l].astype(BF16), w_proj_b[l].astype(BF16), w_out[l].astype(BF16)
        g_ffn = ffn_norm_g[l].reshape(1, d)
        wup, wdn = w_up[l].astype(BF16), w_down[l].astype(BF16)
        cw, cb = conv_w[l], conv_b[l].reshape(1, ff)

        (qa, kat, katb, vai, vab, qb, kbt, kbtb, vbt, vbb, gates) = _in_proj_prompt(
            xp, g_attn, w_std, w_t, qg_t, kg_col, gm, batch, seq, kv_prompt)
        kv_prompt = (kat, vai, kbt, vbt)
        oa = _da_prompt(qa, katb, vab, bias_prompt, wl, sg, batch, seq, lam_init)
        ob = _sb_prompt(qb, kbtb, vbb, u_hi_lo, batch, seq)
        xp = _merge_out(xp, oa, ob, gates, wpa, wpb, wo)
        xp, tail = _ffn_prompt(xp, g_ffn, wup, cw, cb, wdn, batch, seq)
        tails_p.append(tail[:, SUBLANES - (CONV_WIDTH - 1):, :])

        qa, ka, va, qb, kb, vb, gates = _in_proj_sample(xs, g_attn, w_std, w_keys, qg_t, kg_t, gm)
        oa = _da_sample(pt, qa, ka, va, blast, bnew, wl, sg, cda_kt, cda_vi, l, dec_b, dec_t, lam_init)
        ob = _sb_sample(pt, qb, kb, vb, u_hi_lo, csb_kt, csb_vt, l, dec_b, dec_t)
        xs = _merge_out(xs, oa, ob, gates, wpa, wpb, wo)
        st = state_conv[l]
        zeros = jnp.zeros((dec_b, dec_t - 1, ff), F32)
        s1 = jnp.concatenate([st[:, 1:2], zeros], axis=1).reshape(dec_b * dec_t, ff)
        s2 = jnp.concatenate([st, zeros[:, 1:]], axis=1).reshape(dec_b * dec_t, ff)
        xs, a_s = _ffn_sample(xs, g_ffn, wup, cw, cb, wdn, s1, s2, dec_t)
        new_s = (ka.reshape(dec_b, dec_t, N_HEADS_A, 2, HEAD_DIM_A),
                 va.reshape(dec_b, dec_t, N_HEADS_A, 2 * HEAD_DIM_A),
                 kb.reshape(dec_b, dec_t, N_HEADS_B, HEAD_DIM_B),
                 vb.reshape(dec_b, dec_t, N_HEADS_B, HEAD_DIM_B),
                 a_s.reshape(dec_b, dec_t, ff)[:, dec_t - (CONV_WIDTH - 1):, :])
        for lst, r in zip(rows_s, new_s):
            lst.append(r)

    kat, vai, kbt, vbt = kv_prompt
    out_p = [jnp.transpose(kat.reshape(depth, batch, N_HEADS_A, 2, HEAD_DIM_A, seq), (0, 1, 5, 2, 3, 4)),
             vai.reshape(depth, batch, seq, N_HEADS_A, 2 * HEAD_DIM_A),
             jnp.transpose(kbt.reshape(depth, batch, N_HEADS_B, HEAD_DIM_B, seq), (0, 1, 4, 2, 3)),
             jnp.transpose(vbt.reshape(depth, batch, N_HEADS_B, HEAD_DIM_B, seq), (0, 1, 4, 2, 3)),
             jnp.stack(tails_p)]
    out_s = [jnp.stack(r) for r in rows_s]
    return (xp.reshape(batch, seq, d), xs.reshape(dec_b, dec_t, d), *out_p, *out_s)
```

```python
import functools
import math

import numpy as np
import jax
import jax.numpy as jnp
from jax import lax
from jax.experimental import pallas as pl
from jax.experimental.pallas import tpu as pltpu

F32 = jnp.float32
BF16 = jnp.bfloat16

N_HEADS_A = 8
HEAD_DIM_A = 64
WIDTH_A = N_HEADS_A * 2 * HEAD_DIM_A
N_HEADS_B = 8
HEAD_DIM_B = 64
WIDTH_B = N_HEADS_B * HEAD_DIM_B
N_BUCKETS = 32
MAX_DISTANCE = 128
LAMBDA_BASE = 0.8
LAMBDA_SCALE = 0.6
LAMBDA_RATE = 0.3
NORM_EPS = 1e-6
NEG_BIG = -1e30
CONV_WIDTH = 3

LANES = 128
SUBLANES = 8
MXU_EDGE = 256
VMEM_LIMIT_BYTES = 56 * 1024 * 1024

TOKEN_TILE = 256
WIDE_TOKEN_TILE = 512
DA_TILE = 512
SB_TILE = MXU_EDGE
PAGES_PER_STEP = 16
SB_PAIRS_PER_STEP = 4
SB_NEAR_PAGES = 4
SB_FAR_PAGES = 14

LOG2E = math.log2(math.e)
STICK_UNDERFLOW = -104.0


def _cparams(sem):
    return pltpu.CompilerParams(dimension_semantics=sem, vmem_limit_bytes=VMEM_LIMIT_BYTES)


def _const_spec(shape):
    nd = len(shape)
    return pl.BlockSpec(shape, lambda *_: (0,) * nd)


_NT = (((1,), (1,)), ((), ()))


def _rms_rows(x, g):
    return x * lax.rsqrt(jnp.mean(x * x, axis=-1, keepdims=True) + NORM_EPS) * g


def _head_norm(y, gain, gm):
    sq = (y * y).astype(BF16)
    outs = []
    for j in range(WIDTH_A // MXU_EDGE):
        sl = slice(j * MXU_EDGE, (j + 1) * MXU_EDGE)
        ms = jnp.dot(sq[:, sl], gm, preferred_element_type=F32)
        outs.append(y[:, sl] * lax.rsqrt(ms + NORM_EPS))
    return jnp.concatenate(outs, axis=1) * gain


def _in_proj_prompt_kernel(x_ref, g_ref, ws_ref, wt_ref, qg_ref, kgc_ref, gm_ref, *refs, n_prev):
    prev_refs, refs = refs[:4 if n_prev else 0], refs[4 if n_prev else 0:]
    qa_ref, kat_ref, katb_ref, vai_ref, vab_ref, qb_ref, kbt_ref, kbtb_ref, vbt_ref, vbb_ref, gates_ref = refs
    for dst, src in zip((kat_ref, vai_ref, kbt_ref, vbt_ref), prev_refs):
        dst[0:n_prev] = src[...]
    tm = x_ref.shape[0]
    nb = _rms_rows(x_ref[...], g_ref[...]).astype(BF16)

    def seg(lo, width):
        return jnp.dot(nb, ws_ref[:, lo:lo + width], preferred_element_type=F32)

    def seg_t(lo, width):
        return lax.dot_general(wt_ref[lo:lo + width, :], nb, _NT, preferred_element_type=F32)

    qa = _head_norm(seg(0, WIDTH_A), qg_ref[...], gm_ref[...]) * (HEAD_DIM_A ** -0.5 * LOG2E)
    qa_ref[...] = qa.astype(BF16)
    va = seg(WIDTH_A, WIDTH_A)
    vab_ref[...] = va.astype(BF16)
    for h in range(N_HEADS_A):
        vai_ref[n_prev, pl.ds(h, tm, stride=N_HEADS_A), :] = va[:, h * LANES:(h + 1) * LANES]
    off = 2 * WIDTH_A
    qb_ref[...] = (seg(off, WIDTH_B) * (HEAD_DIM_B ** -0.5)).astype(BF16)
    vbb_ref[...] = seg(off + WIDTH_B, WIDTH_B).astype(BF16)
    gates_ref[...] = seg(off + 2 * WIDTH_B, gates_ref.shape[1])

    kat = seg_t(0, WIDTH_A).reshape(WIDTH_A // HEAD_DIM_A, HEAD_DIM_A, tm)
    ms = jnp.mean(kat * kat, axis=1, keepdims=True)
    kat = (kat * lax.rsqrt(ms + NORM_EPS) * kgc_ref[...].reshape(1, HEAD_DIM_A, 1)).reshape(WIDTH_A, tm)
    kat_ref[n_prev, 0] = kat
    katb_ref[0] = kat.astype(BF16)
    kbt = seg_t(WIDTH_A, WIDTH_B)
    kbt_ref[n_prev, 0] = kbt
    kbtb_ref[0] = kbt.astype(BF16)
    vbt_ref[n_prev, 0] = seg_t(WIDTH_A + WIDTH_B, WIDTH_B)


def _in_proj_prompt(x, g, ws, wt, qg_t, kg_col, gm, batch, seq, prev):
    n, d = x.shape
    tm = min(TOKEN_TILE, seq)
    tps = seq // tm
    n_prev = 0 if prev is None else prev[0].shape[0]
    gate_w = ws.shape[1] - 2 * WIDTH_A - 2 * WIDTH_B
    row = lambda w: pl.BlockSpec((tm, w), lambda i: (i, 0))
    t_blk = lambda layers, w: pl.BlockSpec((layers, 1, w, tm), lambda i: (0, i // tps, 0, i % tps))
    vi_blk = lambda layers: pl.BlockSpec((layers, tm * N_HEADS_A, LANES), lambda i: (0, i, 0))
    t_out = lambda w: t_blk(n_prev + 1, w)
    prev_specs = [] if prev is None else [t_blk(n_prev, WIDTH_A), vi_blk(n_prev),
                                          t_blk(n_prev, WIDTH_B), t_blk(n_prev, WIDTH_B)]
    da_tile, sb_tile = min(DA_TILE, seq), min(SB_TILE, seq)
    t_tile = lambda w, kt: pl.BlockSpec((1, w, tm), lambda i: (i // (kt // tm), 0, i % (kt // tm)))
    out_specs = [row(WIDTH_A), t_out(WIDTH_A), t_tile(WIDTH_A, da_tile),
                 vi_blk(n_prev + 1), row(WIDTH_A),
                 row(WIDTH_B), t_out(WIDTH_B), t_tile(WIDTH_B, sb_tile), t_out(WIDTH_B), row(WIDTH_B), row(gate_w)]
    layers = n_prev + 1
    out_shape = [jax.ShapeDtypeStruct((n, WIDTH_A), BF16),
                 jax.ShapeDtypeStruct((layers, batch, WIDTH_A, seq), F32),
                 jax.ShapeDtypeStruct((n // da_tile, WIDTH_A, da_tile), BF16),
                 jax.ShapeDtypeStruct((layers, n * N_HEADS_A, LANES), F32),
                 jax.ShapeDtypeStruct((n, WIDTH_A), BF16),
                 jax.ShapeDtypeStruct((n, WIDTH_B), BF16),
                 jax.ShapeDtypeStruct((layers, batch, WIDTH_B, seq), F32),
                 jax.ShapeDtypeStruct((n // sb_tile, WIDTH_B, sb_tile), BF16),
                 jax.ShapeDtypeStruct((layers, batch, WIDTH_B, seq), F32),
                 jax.ShapeDtypeStruct((n, WIDTH_B), BF16),
                 jax.ShapeDtypeStruct((n, gate_w), F32)]
    return pl.pallas_call(
        functools.partial(_in_proj_prompt_kernel, n_prev=n_prev),
        grid=(n // tm,),
        in_specs=[row(d), _const_spec(g.shape), _const_spec(ws.shape), _const_spec(wt.shape),
                  _const_spec(qg_t.shape), _const_spec(kg_col.shape), _const_spec(gm.shape)] + prev_specs,
        out_specs=out_specs,
        out_shape=out_shape,
        compiler_params=_cparams(("arbitrary",)),
        name="in_proj_prompt",
    )(x, g, ws, wt, qg_t, kg_col, gm, *(prev or ()))


def _in_proj_sample_kernel(x_ref, g_ref, ws_ref, wk_ref, qg_ref, kg_ref, gm_ref,
                           qa_ref, ka_ref, va_ref, qb_ref, kb_ref, vb_ref, gates_ref):
    nb = _rms_rows(x_ref[...], g_ref[...]).astype(BF16)

    def seg(w_ref, lo, width):
        return jnp.dot(nb, w_ref[:, lo:lo + width], preferred_element_type=F32)

    qa_ref[...] = _head_norm(seg(ws_ref, 0, WIDTH_A), qg_ref[...], gm_ref[...]) * (HEAD_DIM_A ** -0.5 * LOG2E)
    va_ref[...] = seg(ws_ref, WIDTH_A, WIDTH_A)
    off = 2 * WIDTH_A
    qb_ref[...] = seg(ws_ref, off, WIDTH_B) * (HEAD_DIM_B ** -0.5)
    vb_ref[...] = seg(ws_ref, off + WIDTH_B, WIDTH_B)
    gates_ref[...] = seg(ws_ref, off + 2 * WIDTH_B, gates_ref.shape[1])
    ka_ref[...] = _head_norm(seg(wk_ref, 0, WIDTH_A), kg_ref[...], gm_ref[...])
    kb_ref[...] = seg(wk_ref, WIDTH_A, WIDTH_B)


def _in_proj_sample(x, g, ws, wk, qg_t, kg_t, gm):
    n, d = x.shape
    tm = min(TOKEN_TILE, n)
    gate_w = ws.shape[1] - 2 * WIDTH_A - 2 * WIDTH_B
    row = lambda w: pl.BlockSpec((tm, w), lambda i: (i, 0))
    widths = [WIDTH_A, WIDTH_A, WIDTH_A, WIDTH_B, WIDTH_B, WIDTH_B, gate_w]
    return pl.pallas_call(
        _in_proj_sample_kernel,
        grid=(n // tm,),
        in_specs=[row(d), _const_spec(g.shape), _const_spec(ws.shape), _const_spec(wk.shape),
                  _const_spec(qg_t.shape), _const_spec(kg_t.shape), _const_spec(gm.shape)],
        out_specs=[row(w) for w in widths],
        out_shape=[jax.ShapeDtypeStruct((n, w), F32) for w in widths],
        compiler_params=_cparams(("arbitrary",)),
        name="in_proj_sample",
    )(x, g, ws, wk, qg_t, kg_t, gm)


def _lambda_value(wl, lam_init):
    a = jnp.sum(wl[0:1] * wl[1:2], axis=1, keepdims=True)
    b = jnp.sum(wl[2:3] * wl[3:4], axis=1, keepdims=True)
    return jnp.exp(a) - jnp.exp(b) + lam_init


def _sub_norm(d, gain, lam_init):
    y = d * lax.rsqrt(jnp.mean(d * d, axis=-1, keepdims=True) + NORM_EPS)
    return y * gain * (1.0 - lam_init)


def _softmax_probs(s, m_sc, l_sc):
    m_prev = m_sc[...]
    m_new = jnp.maximum(m_prev, jnp.max(s, axis=-1, keepdims=True))
    alpha = jnp.exp2(m_prev - m_new)
    p = jnp.exp2(s - m_new)
    l_sc[...] = alpha * l_sc[...] + jnp.sum(p, axis=-1, keepdims=True)
    m_sc[...] = m_new
    return alpha, p.astype(BF16)


def _lane_tile(x, width):
    reps = width // x.shape[1]
    return x if reps == 1 else jnp.concatenate([x] * reps, axis=1)


def _stick_weights(z, u_hi_lo, carry_sc, valid):
    width = z.shape[1]
    w = jnp.maximum(z, 0.0) + jnp.log(1.0 + jnp.exp(-jnp.abs(z)))
    if valid is not None:
        w = jnp.where(valid, w, 0.0)
    hi = w.astype(BF16)
    lo = (w - hi.astype(F32)).astype(BF16)
    later = jnp.dot(jnp.concatenate([hi, lo], axis=1), u_hi_lo, preferred_element_type=F32)
    carry = carry_sc[...]
    a = jnp.exp(z - w + later + _lane_tile(carry, width))
    if valid is not None:
        a = jnp.where(valid, a, 0.0)
    carry_sc[...] = carry - jnp.sum(w, axis=-1, keepdims=True)
    return a.astype(BF16)


def _split_lanes(q):
    lane = lax.broadcasted_iota(jnp.int32, q.shape, 1)
    zero = jnp.zeros_like(q)
    half = LANES // 2
    return jnp.concatenate([jnp.where(lane < half, q, zero), jnp.where(lane >= half, q, zero)], axis=0)


def _da_prompt_kernel(q_ref, kt_ref, v_ref, bias_ref, wl_ref, sg_ref, o_ref, m_sc, acc_sc,
                      *, tq, lam_init):
    ones = jnp.ones((tq, LANES), BF16)
    lam = _lambda_value(wl_ref[...], lam_init)

    def query_tile(qi, carry):
        rows = pl.ds(pl.multiple_of(qi * tq, tq), tq)
        qq = _split_lanes(q_ref[rows, :])
        m_sc[...] = jnp.full(m_sc.shape, NEG_BIG, F32)
        acc_sc[...] = jnp.zeros(acc_sc.shape, F32)

        def block(ki, bias):
            v_ext = jnp.concatenate([v_ref[pl.ds(pl.multiple_of(ki * tq, tq), tq), :], ones], axis=1)
            s = jnp.dot(qq, kt_ref[ki], preferred_element_type=F32)
            if bias is not None:
                s = (s.reshape(2, tq, tq) + bias[None]).reshape(2 * tq, tq)
            m_prev = m_sc[...]
            m_new = jnp.maximum(m_prev, jnp.max(s, axis=-1, keepdims=True))
            alpha = jnp.exp2(m_prev - m_new)
            p = jnp.concatenate([jnp.exp2(s[:, j * LANES:(j + 1) * LANES] - m_new).astype(BF16)
                                 for j in range(tq // LANES)], axis=1)
            acc_sc[...] = (_lane_tile(alpha, 2 * LANES) * acc_sc[...]
                           + jnp.dot(p, v_ext, preferred_element_type=F32))
            m_sc[...] = m_new

        def far(ki, c):
            block(ki, None)
            return c

        lax.fori_loop(0, qi - 1, far, 0)

        @pl.when(qi >= 1)
        def _():
            block(qi - 1, bias_ref[0, 0])

        block(qi, bias_ref[0, 1])

        acc = acc_sc[...]
        o = acc[:, :LANES] / acc[:, LANES:]
        d = o[:tq] - lam * o[tq:]
        o_ref[rows, :] = _sub_norm(d, sg_ref[...], lam_init).astype(o_ref.dtype)
        return carry

    lax.fori_loop(0, q_ref.shape[0] // tq, query_tile, 0)


def _da_prompt(q, kt, v, bias, wl, sg, batch, seq, lam_init):
    n = q.shape[0]
    tq = kt.shape[2]
    nq = seq // tq
    kern = functools.partial(_da_prompt_kernel, tq=tq, lam_init=lam_init)
    head = pl.BlockSpec((seq, LANES), lambda b, h: (b, h))
    return pl.pallas_call(
        kern,
        grid=(batch, N_HEADS_A),
        in_specs=[head, pl.BlockSpec((nq, LANES, tq), lambda b, h: (b, h, 0)), head,
                  pl.BlockSpec((1, 2, tq, tq), lambda b, h: (h, 0, 0, 0)),
                  _const_spec(wl.shape), _const_spec(sg.shape)],
        out_specs=head,
        out_shape=jax.ShapeDtypeStruct((n, WIDTH_A), BF16),
        scratch_shapes=[pltpu.VMEM((2 * tq, LANES), F32), pltpu.VMEM((2 * tq, 2 * LANES), F32)],
        compiler_params=_cparams(("arbitrary", "arbitrary")),
        name="da_prompt",
    )(q, kt, v, bias, wl, sg)


def _sb_prompt_kernel(q_ref, kt_ref, v_ref, u_ref, o_ref, carry_sc, acc_sc, *, tq, pp):
    qi = pl.program_id(2)
    q = q_ref[...]
    qqs = [_split_lanes(q[:, p * LANES:(p + 1) * LANES]) for p in range(pp)]
    carry_sc[...] = jnp.zeros(carry_sc.shape, F32)
    acc_sc[...] = jnp.zeros(acc_sc.shape, F32)

    def block(ki, valid):
        k0 = pl.multiple_of(ki * tq, tq)
        kt = kt_ref[ki]
        top = None
        zs = [jnp.dot(qqs[p], kt[p * LANES:(p + 1) * LANES, :], preferred_element_type=F32) for p in range(pp)]
        weights = [_stick_weights(zs[p], u_ref[...], carry_sc.at[p], valid) for p in range(pp)]
        for p in range(pp):
            lanes = slice(p * LANES, (p + 1) * LANES)
            acc_sc[p] = acc_sc[p] + jnp.dot(weights[p], v_ref[pl.ds(k0, tq), lanes], preferred_element_type=F32)
            c = jnp.max(carry_sc[p])
            top = c if top is None else jnp.maximum(top, c)
        return top

    row = lax.broadcasted_iota(jnp.int32, (2, tq, tq), 1).reshape(2 * tq, tq)
    col = lax.broadcasted_iota(jnp.int32, (2 * tq, tq), 1)
    top = block(qi, col < row)

    def more(c):
        return jnp.logical_and(c[0] < qi, c[1] > STICK_UNDERFLOW)

    def left(c):
        return c[0] + 1, block(qi - 1 - c[0], None)

    lax.while_loop(more, left, (jnp.int32(0), top))

    lane = lax.broadcasted_iota(jnp.int32, (tq, LANES), 1)
    for p in range(pp):
        acc = acc_sc[p]
        o_ref[:, p * LANES:(p + 1) * LANES] = jnp.where(lane < LANES // 2, acc[:tq], acc[tq:]).astype(o_ref.dtype)


def _sb_prompt(q, kt, v, u_hi_lo, batch, seq):
    n = q.shape[0]
    tq = kt.shape[2]
    nq = seq // tq
    pp = SB_PAIRS_PER_STEP
    width = pp * LANES
    kern = functools.partial(_sb_prompt_kernel, tq=tq, pp=pp)
    return pl.pallas_call(
        kern,
        grid=(batch, WIDTH_B // width, nq),
        in_specs=[pl.BlockSpec((tq, width), lambda b, h, i: (b * nq + i, h)),
                  pl.BlockSpec((nq, width, tq), lambda b, h, i: (b, h, 0)),
                  pl.BlockSpec((seq, width), lambda b, h, i: (b, h)),
                  _const_spec(u_hi_lo.shape)],
        out_specs=pl.BlockSpec((tq, width), lambda b, h, i: (b * nq + i, h)),
        out_shape=jax.ShapeDtypeStruct((n, WIDTH_B), BF16),
        scratch_shapes=[pltpu.VMEM((pp, 2 * tq, LANES), F32), pltpu.VMEM((pp, 2 * tq, LANES), F32)],
        compiler_params=_cparams(("arbitrary", "arbitrary", "arbitrary")),
        name="sb_prompt",
    )(q, kt, v, u_hi_lo)


def _block_diag_queries(q, groups, width):
    t = q.shape[0]
    rep = jnp.concatenate([q] * groups, axis=0)
    r = lax.broadcasted_iota(jnp.int32, rep.shape, 0)
    c = lax.broadcasted_iota(jnp.int32, rep.shape, 1)
    keep = (c // width) == (r // t)
    return jnp.where(keep, rep, 0.0).astype(BF16)


def _pad_rows(x, rows):
    return jnp.concatenate([x, jnp.zeros((rows - x.shape[0], x.shape[1]), x.dtype)], axis=0)


def _da_sample_kernel(pt_ref, q_ref, knew_ref, vnew_ref, blast_ref, bnew_ref, wl_ref, sg_ref, *rest,
                      pages, page, lam_init):
    del pt_ref
    kt_refs, v_refs = rest[:pages], rest[pages:2 * pages]
    o_ref, qbd_sc, m_sc, l_sc, acc_sc = rest[2 * pages:]
    g = pl.program_id(1)
    last = pl.num_programs(1) - 1
    t = q_ref.shape[0]
    rows_h = 2 * t

    @pl.when(g == 0)
    def _():
        qbd_sc[...] = _block_diag_queries(q_ref[...], 2 * N_HEADS_A, HEAD_DIM_A)
        m_sc[...] = jnp.full(m_sc.shape, NEG_BIG, F32)
        l_sc[...] = jnp.zeros(l_sc.shape, F32)
        acc_sc[...] = jnp.zeros(acc_sc.shape, F32)

    qbd = qbd_sc[...]
    ktcat = jnp.concatenate([r[0, 0].astype(BF16) for r in kt_refs], axis=1)
    s = jnp.dot(qbd, ktcat, preferred_element_type=F32)
    s = s + blast_ref[...] * (g == last).astype(F32)
    alpha, p = _softmax_probs(s, m_sc, l_sc)
    for h in range(N_HEADS_A):
        vh = jnp.concatenate([r[0, 0, pl.ds(h, page, stride=N_HEADS_A), :] for r in v_refs], axis=0)
        rs = slice(h * rows_h, (h + 1) * rows_h)
        acc_sc[rs, :] = alpha[rs] * acc_sc[rs, :] + jnp.dot(p[rs], vh.astype(BF16), preferred_element_type=F32)

    @pl.when(g == last)
    def _():
        kn = _pad_rows(knew_ref[...], page).astype(BF16)
        vn = _pad_rows(vnew_ref[...], page).astype(BF16)
        s2 = lax.dot_general(qbd, kn, _NT, preferred_element_type=F32) + bnew_ref[...]
        alpha2, p2 = _softmax_probs(s2, m_sc, l_sc)
        o_all = jnp.dot(p2, vn, preferred_element_type=F32)
        inv_l = 1.0 / l_sc[...]
        lam = _lambda_value(wl_ref[...], lam_init)
        outs = []
        for h in range(N_HEADS_A):
            rs = slice(h * rows_h, (h + 1) * rows_h)
            cols = slice(h * 2 * HEAD_DIM_A, (h + 1) * 2 * HEAD_DIM_A)
            o = (alpha2[rs] * acc_sc[rs, :] + o_all[rs, cols]) * inv_l[rs]
            outs.append(_sub_norm(o[:t] - lam * o[t:], sg_ref[...], lam_init))
        o_ref[...] = jnp.concatenate(outs, axis=1)


def _page_specs(layer, n_pages, pages, block, reverse):
    specs = []
    for j in range(pages):
        if reverse:
            idx = lambda b, g, pt, j=j: (layer, pt[b * n_pages + n_pages - (g + 1) * pages + j], 0, 0)
        else:
            idx = lambda b, g, pt, j=j: (layer, pt[b * n_pages + g * pages + j], 0, 0)
        specs.append(pl.BlockSpec((1, 1) + block, idx))
    return specs


def _da_sample(pt, q, knew, vnew, blast, bnew, wl, sg, cache_kt, cache_vi, layer, dec_b, dec_t, lam_init):
    n_pages = pt.shape[0] // dec_b
    page = cache_kt.shape[3]
    pages = min(PAGES_PER_STEP, n_pages)
    rows = 2 * N_HEADS_A * dec_t
    tok = lambda w: pl.BlockSpec((dec_t, w), lambda b, g, pt: (b, 0))
    const = lambda a: pl.BlockSpec(a.shape, lambda b, g, pt: (0,) * a.ndim)
    kern = functools.partial(_da_sample_kernel, pages=pages, page=page, lam_init=lam_init)
    grid_spec = pltpu.PrefetchScalarGridSpec(
        num_scalar_prefetch=1,
        grid=(dec_b, n_pages // pages),
        in_specs=[tok(WIDTH_A), tok(WIDTH_A), tok(WIDTH_A), const(blast), const(bnew), const(wl), const(sg)]
        + _page_specs(layer, n_pages, pages, (WIDTH_A, page), False)
        + _page_specs(layer, n_pages, pages, (page * N_HEADS_A, 2 * HEAD_DIM_A), False),
        out_specs=tok(WIDTH_A),
        scratch_shapes=[pltpu.VMEM((rows, WIDTH_A), BF16), pltpu.VMEM((rows, 1), F32),
                        pltpu.VMEM((rows, 1), F32), pltpu.VMEM((rows, 2 * HEAD_DIM_A), F32)],
    )
    return pl.pallas_call(
        kern,
        grid_spec=grid_spec,
        out_shape=jax.ShapeDtypeStruct((dec_b * dec_t, WIDTH_A), F32),
        compiler_params=_cparams(("arbitrary", "arbitrary")),
        name="da_sample",
    )(pt, q, knew, vnew, blast, bnew, wl, sg, *([cache_kt] * pages), *([cache_vi] * pages))


def _sb_sweep_pages(qbd, kt_refs, vt_refs, u_ref, carry_ref, acc_ref):
    blk = u_ref.shape[1]
    ktcat = jnp.concatenate([r[0, 0].astype(BF16) for r in kt_refs], axis=1)
    vtcat = jnp.concatenate([r[0, 0].astype(BF16) for r in vt_refs], axis=1)
    z = jnp.dot(qbd, ktcat, preferred_element_type=F32)
    for b in reversed(range(ktcat.shape[1] // blk)):
        sl = slice(b * blk, (b + 1) * blk)
        a = _stick_weights(z[:, sl], u_ref[...], carry_ref, None)
        acc_ref[...] = acc_ref[...] + lax.dot_general(a, vtcat[:, sl], _NT, preferred_element_type=F32)


def _sb_near_kernel(pt_ref, q_ref, knew_ref, vnew_ref, u_ref, *rest, pages, page):
    del pt_ref
    kt_refs, vt_refs = rest[:pages], rest[pages:2 * pages]
    acc_ref, carry_ref, o_ref = rest[2 * pages:]
    t = q_ref.shape[0]
    qbd = _block_diag_queries(q_ref[...], N_HEADS_B, HEAD_DIM_B)
    carry_ref[...] = jnp.zeros(carry_ref.shape, F32)
    kn = _pad_rows(knew_ref[...], page).astype(BF16)
    vn = _pad_rows(vnew_ref[...], page).astype(BF16)
    z = lax.dot_general(qbd, kn, _NT, preferred_element_type=F32)
    row_t = lax.broadcasted_iota(jnp.int32, z.shape, 0) % t
    col = lax.broadcasted_iota(jnp.int32, z.shape, 1)
    u_page = jnp.concatenate([u_ref[0:page, 0:page], u_ref[0:page, 0:page]], axis=0)
    a = _stick_weights(z, u_page, carry_ref, col < row_t)
    acc_ref[...] = jnp.dot(a, vn, preferred_element_type=F32)
    _sb_sweep_pages(qbd, kt_refs, vt_refs, u_ref, carry_ref, acc_ref)
    o_ref[...] = _own_head_columns(acc_ref[...], t)


def _own_head_columns(acc, t):
    col_h = lax.broadcasted_iota(jnp.int32, (t, WIDTH_B), 1) // HEAD_DIM_B
    out = jnp.zeros((t, WIDTH_B), F32)
    for h in range(N_HEADS_B):
        out = out + jnp.where(col_h == h, acc[h * t:(h + 1) * t, :], 0.0)
    return out


def _sb_far_kernel(pt_ref, need_ref, q_ref, acc_in_ref, carry_in_ref, u_ref, *rest, pages):
    del pt_ref
    kt_refs, vt_refs = rest[:pages], rest[pages:2 * pages]
    o_ref, qbd_sc, carry_sc, acc_sc = rest[2 * pages:]
    b = pl.program_id(0)
    g = pl.program_id(1)
    last = pl.num_programs(1) - 1
    t = q_ref.shape[0]

    @pl.when(g == 0)
    def _():
        qbd_sc[...] = _block_diag_queries(q_ref[...], N_HEADS_B, HEAD_DIM_B)
        carry_sc[...] = carry_in_ref[...]
        acc_sc[...] = acc_in_ref[...]

    @pl.when(need_ref[b] > 0)
    def _():
        _sb_sweep_pages(qbd_sc[...], kt_refs, vt_refs, u_ref, carry_sc, acc_sc)

    @pl.when(g == last)
    def _():
        o_ref[...] = _own_head_columns(acc_sc[...], t)


def _sb_sample(pt, q, knew, vnew, u_hi_lo, cache_kt, cache_vt, layer, dec_b, dec_t):
    n_pages = pt.shape[0] // dec_b
    page = cache_kt.shape[3]
    near = min(SB_NEAR_PAGES, n_pages)
    n_far = n_pages - near
    assert n_far > 0, "the far sweep also does the final head extraction"
    far = max(p for p in range(1, SB_FAR_PAGES + 1) if n_far % p == 0)
    rows = N_HEADS_B * dec_t
    blk = (WIDTH_B, page)

    tok1 = lambda w: pl.BlockSpec((dec_t, w), lambda b, pt: (b, 0))
    state1 = lambda w: pl.BlockSpec((rows, w), lambda b, pt: (b, 0))
    near_specs = [pl.BlockSpec((1, 1) + blk, lambda b, pt, j=j: (layer, pt[b * n_pages + n_far + j], 0, 0))
                  for j in range(near)]
    acc, carry, o_near = pl.pallas_call(
        functools.partial(_sb_near_kernel, pages=near, page=page),
        grid_spec=pltpu.PrefetchScalarGridSpec(
            num_scalar_prefetch=1,
            grid=(dec_b,),
            in_specs=[tok1(WIDTH_B), tok1(WIDTH_B), tok1(WIDTH_B),
                      pl.BlockSpec(u_hi_lo.shape, lambda b, pt: (0, 0))] + near_specs + near_specs,
            out_specs=[state1(WIDTH_B), state1(LANES), tok1(WIDTH_B)],
        ),
        out_shape=[jax.ShapeDtypeStruct((dec_b * rows, WIDTH_B), F32),
                   jax.ShapeDtypeStruct((dec_b * rows, LANES), F32),
                   jax.ShapeDtypeStruct((dec_b * dec_t, WIDTH_B), F32)],
        compiler_params=_cparams(("arbitrary",)),
        name="sb_sample_near",
    )(pt, q, knew, vnew, u_hi_lo, *([cache_kt] * near), *([cache_vt] * near))

    need = (jnp.max(carry.reshape(dec_b, rows * LANES), axis=1) > STICK_UNDERFLOW).astype(jnp.int32)
    tok2 = lambda w: pl.BlockSpec((dec_t, w), lambda b, g, pt, nd: (b, 0))
    state2 = lambda w: pl.BlockSpec((rows, w), lambda b, g, pt, nd: (b, 0))

    def far_spec(j):
        def idx(b, g, pt, nd):
            return (layer, jnp.where(nd[b] > 0, pt[b * n_pages + n_far - (g + 1) * far + j], 0), 0, 0)
        return pl.BlockSpec((1, 1) + blk, idx)

    far_specs = [far_spec(j) for j in range(far)]

    def sweep_far():
        return pl.pallas_call(
            functools.partial(_sb_far_kernel, pages=far),
            grid_spec=pltpu.PrefetchScalarGridSpec(
                num_scalar_prefetch=2,
                grid=(dec_b, n_far // far),
                in_specs=[tok2(WIDTH_B), state2(WIDTH_B), state2(LANES),
                          pl.BlockSpec(u_hi_lo.shape, lambda b, g, pt, nd: (0, 0))] + far_specs + far_specs,
                out_specs=tok2(WIDTH_B),
                scratch_shapes=[pltpu.VMEM((rows, WIDTH_B), BF16), pltpu.VMEM((rows, LANES), F32),
                                pltpu.VMEM((rows, WIDTH_B), F32)],
            ),
            out_shape=jax.ShapeDtypeStruct((dec_b * dec_t, WIDTH_B), F32),
            compiler_params=_cparams(("arbitrary", "arbitrary")),
            name="sb_sample_far",
        )(pt, need, q, acc, carry, u_hi_lo, *([cache_kt] * far), *([cache_vt] * far))

    return lax.cond(jnp.any(need > 0), sweep_far, lambda: o_near)


def _merge_out_kernel(x_ref, oa_ref, ob_ref, gates_ref, wpa_ref, wpb_ref, wo_ref, o_ref):
    d = x_ref.shape[1]
    ga = jax.nn.sigmoid(gates_ref[:, :d])
    gb = jax.nn.sigmoid(gates_ref[:, d:])
    pa = jnp.dot(oa_ref[...].astype(BF16), wpa_ref[...], preferred_element_type=F32)
    pb = jnp.dot(ob_ref[...].astype(BF16), wpb_ref[...], preferred_element_type=F32)
    merged = ga * pa + gb * pb
    o_ref[...] = x_ref[...] + jnp.dot(merged.astype(BF16), wo_ref[...], preferred_element_type=F32)


def _merge_out(x, oa, ob, gates, wpa, wpb, wo):
    n, d = x.shape
    tm = min(WIDE_TOKEN_TILE, n)
    row = lambda w: pl.BlockSpec((tm, w), lambda i: (i, 0))
    return pl.pallas_call(
        _merge_out_kernel,
        grid=(n // tm,),
        in_specs=[row(d), row(WIDTH_A), row(WIDTH_B), row(2 * d),
                  _const_spec(wpa.shape), _const_spec(wpb.shape), _const_spec(wo.shape)],
        out_specs=row(d),
        out_shape=jax.ShapeDtypeStruct((n, d), F32),
        compiler_params=_cparams(("arbitrary",)),
        name="merge_out",
    )(x, oa, ob, gates, wpa, wpb, wo)


def _gelu_exact(x):
    return 0.5 * x * (1.0 + lax.erf(x * np.float32(math.sqrt(0.5))))


def _ffn_up(x_ref, g_ref, wup_ref, ff):
    x = x_ref[...]
    n2 = _rms_rows(x, g_ref[...]).astype(BF16)
    a = jnp.dot(n2, wup_ref[:, :ff], preferred_element_type=F32)
    b = jnp.dot(n2, wup_ref[:, ff:], preferred_element_type=F32)
    return x, a, b


def _ffn_down(x, a_m1, a_m2, a, b, cw_ref, cb_ref, wdn_ref):
    conv = cb_ref[...] + a_m2 * cw_ref[0:1, :] + a_m1 * cw_ref[1:2, :] + a * cw_ref[2:3, :]
    h = (_gelu_exact(conv) * b).astype(BF16)
    return x + jnp.dot(h, wdn_ref[...], preferred_element_type=F32)


def _ffn_prompt_kernel(x_ref, g_ref, wup_ref, cw_ref, cb_ref, wdn_ref, y_ref, tail_ref, carry_sc,
                       *, tiles_per_seq):
    ff = wdn_ref.shape[0]
    tm = x_ref.shape[0]
    x, a, b = _ffn_up(x_ref, g_ref, wup_ref, ff)
    first = (pl.program_id(0) % tiles_per_seq) == 0
    prev = jnp.where(first, 0.0, carry_sc[...])
    p6 = prev[SUBLANES - 2:SUBLANES - 1, :]
    p7 = prev[SUBLANES - 1:SUBLANES, :]
    row = lax.broadcasted_iota(jnp.int32, a.shape, 0)
    a_m1 = jnp.where(row == 0, p7, pltpu.roll(a, 1, 0))
    a_m2 = jnp.where(row == 0, p6, jnp.where(row == 1, p7, pltpu.roll(a, 2, 0)))
    last_rows = a[tm - SUBLANES:, :]
    carry_sc[...] = last_rows
    tail_ref[0] = last_rows
    y_ref[...] = _ffn_down(x, a_m1, a_m2, a, b, cw_ref, cb_ref, wdn_ref)


def _merge_ffn_prompt_kernel(x_ref, oa_ref, ob_ref, gates_ref, wpa_ref, wpb_ref, wo_ref,
                             g_ref, wup_ref, cw_ref, cb_ref, wdn_ref, y_ref, tail_ref, mid_sc, carry_sc,
                             *, tiles_per_seq):
    _merge_out_kernel(x_ref, oa_ref, ob_ref, gates_ref, wpa_ref, wpb_ref, wo_ref, mid_sc)
    _ffn_prompt_kernel(mid_sc, g_ref, wup_ref, cw_ref, cb_ref, wdn_ref, y_ref, tail_ref, carry_sc,
                       tiles_per_seq=tiles_per_seq)


def _merge_ffn_prompt(x, oa, ob, gates, wpa, wpb, wo, g, wup, cw, cb, wdn, batch, seq):
    n, d = x.shape
    ff = wdn.shape[0]
    tm = min(WIDE_TOKEN_TILE, seq)
    tiles_per_seq = seq // tm
    row = lambda w: pl.BlockSpec((tm, w), lambda i: (i, 0))
    kern = functools.partial(_merge_ffn_prompt_kernel, tiles_per_seq=tiles_per_seq)
    consts = (wpa, wpb, wo, g, wup, cw, cb, wdn)
    return pl.pallas_call(
        kern,
        grid=(n // tm,),
        in_specs=[row(d), row(WIDTH_A), row(WIDTH_B), row(2 * d)] + [_const_spec(c.shape) for c in consts],
        out_specs=[row(d), pl.BlockSpec((1, SUBLANES, ff), lambda i: (i // tiles_per_seq, 0, 0))],
        out_shape=[jax.ShapeDtypeStruct((n, d), F32), jax.ShapeDtypeStruct((batch, SUBLANES, ff), F32)],
        scratch_shapes=[pltpu.VMEM((tm, d), F32), pltpu.VMEM((SUBLANES, ff), F32)],
        compiler_params=_cparams(("arbitrary",)),
        name="merge_ffn_prompt",
    )(x, oa, ob, gates, *consts)


def _ffn_sample_kernel(x_ref, g_ref, wup_ref, cw_ref, cb_ref, wdn_ref, s1_ref, s2_ref, y_ref, a_ref,
                       *, dec_t):
    ff = wdn_ref.shape[0]
    x, a, b = _ffn_up(x_ref, g_ref, wup_ref, ff)
    t = lax.broadcasted_iota(jnp.int32, a.shape, 0) % dec_t
    a_m1 = jnp.where(t >= 1, pltpu.roll(a, 1, 0), s1_ref[...])
    a_m2 = jnp.where(t >= 2, pltpu.roll(a, 2, 0), s2_ref[...])
    a_ref[...] = a
    y_ref[...] = _ffn_down(x, a_m1, a_m2, a, b, cw_ref, cb_ref, wdn_ref)


def _ffn_prompt(x, g, wup, cw, cb, wdn, batch, seq):
    n, d = x.shape
    ff = wdn.shape[0]
    tm = min(WIDE_TOKEN_TILE, seq)
    tiles_per_seq = seq // tm
    row = lambda w: pl.BlockSpec((tm, w), lambda i: (i, 0))
    kern = functools.partial(_ffn_prompt_kernel, tiles_per_seq=tiles_per_seq)
    return pl.pallas_call(
        kern,
        grid=(n // tm,),
        in_specs=[row(d), _const_spec(g.shape), _const_spec(wup.shape), _const_spec(cw.shape),
                  _const_spec(cb.shape), _const_spec(wdn.shape)],
        out_specs=[row(d), pl.BlockSpec((1, SUBLANES, ff), lambda i: (i // tiles_per_seq, 0, 0))],
        out_shape=[jax.ShapeDtypeStruct((n, d), F32), jax.ShapeDtypeStruct((batch, SUBLANES, ff), F32)],
        scratch_shapes=[pltpu.VMEM((SUBLANES, ff), F32)],
        compiler_params=_cparams(("arbitrary",)),
        name="ffn_prompt",
    )(x, g, wup, cw, cb, wdn)


def _ffn_sample(x, g, wup, cw, cb, wdn, s1, s2, dec_t):
    n, d = x.shape
    ff = wdn.shape[0]
    tm = min(TOKEN_TILE, n)
    row = lambda w: pl.BlockSpec((tm, w), lambda i: (i, 0))
    kern = functools.partial(_ffn_sample_kernel, dec_t=dec_t)
    return pl.pallas_call(
        kern,
        grid=(n // tm,),
        in_specs=[row(d), _const_spec(g.shape), _const_spec(wup.shape), _const_spec(cw.shape),
                  _const_spec(cb.shape), _const_spec(wdn.shape), row(ff), row(ff)],
        out_specs=[row(d), row(ff)],
        out_shape=[jax.ShapeDtypeStruct((n, d), F32), jax.ShapeDtypeStruct((n, ff), F32)],
        compiler_params=_cparams(("arbitrary",)),
        name="ffn_sample",
    )(x, g, wup, cw, cb, wdn, s1, s2)


def _bucket_table(max_dist):
    n = np.arange(max_dist + 1, dtype=np.int32)
    max_exact = N_BUCKETS // 2
    nf = np.maximum(n, 1).astype(np.float32)
    large = max_exact + (np.log(nf / np.float32(max_exact)) / np.float32(math.log(MAX_DISTANCE / max_exact))
                         * np.float32(N_BUCKETS - max_exact)).astype(np.int32)
    large = np.minimum(large, N_BUCKETS - 1)
    return np.where(n < max_exact, n, large).astype(np.int32)


def _bias_kernel(rs_ref, prompt_ref, last_ref, new_ref, *, upper, tq, dec_t, page):
    h = pl.program_id(0)

    def lookup(dist):
        val = jnp.zeros(dist.shape, F32) + rs_ref[N_BUCKETS - 1, h]
        for b in range(N_BUCKETS - 2, -1, -1):
            val = jnp.where(dist <= upper[b], rs_ref[b, h], val)
        return jnp.where(dist >= 0, val, NEG_BIG)

    ql = lax.broadcasted_iota(jnp.int32, (tq, tq), 0)
    kl = lax.broadcasted_iota(jnp.int32, (tq, tq), 1)
    prompt_ref[0, 0] = lookup(ql - kl + tq)
    prompt_ref[0, 1] = lookup(ql - kl)
    t = lax.broadcasted_iota(jnp.int32, (2 * dec_t, page), 0) % dec_t
    j = lax.broadcasted_iota(jnp.int32, (2 * dec_t, page), 1)
    last_ref[...] = lookup(page + t - j)
    new_ref[...] = lookup(t - j)


def _bias_tiles(rel_shift, tq, dec_t, page):
    table = _bucket_table(2 * MAX_DISTANCE)
    upper = tuple(int(np.nonzero(table <= b)[0].max()) for b in range(N_BUCKETS - 1))
    heads = rel_shift.shape[1]
    kern = functools.partial(_bias_kernel, upper=upper, tq=tq, dec_t=dec_t, page=page)
    return pl.pallas_call(
        kern,
        grid=(heads,),
        in_specs=[pl.BlockSpec(memory_space=pltpu.SMEM)],
        out_specs=[pl.BlockSpec((1, 2, tq, tq), lambda h: (h, 0, 0, 0)),
                   pl.BlockSpec((2 * dec_t, page), lambda h: (h, 0)),
                   pl.BlockSpec((2 * dec_t, page), lambda h: (h, 0))],
        out_shape=[jax.ShapeDtypeStruct((heads, 2, tq, tq), F32),
                   jax.ShapeDtypeStruct((heads * 2 * dec_t, page), F32),
                   jax.ShapeDtypeStruct((heads * 2 * dec_t, page), F32)],
        compiler_params=_cparams(("arbitrary",)),
        name="bias_tiles",
    )(rel_shift)


def _lower_triangles(w):
    j = np.arange(w)[:, None]
    s = np.arange(w)[None, :]
    u = -(j > s).astype(np.float32)
    return jnp.asarray(np.concatenate([u, u], axis=0), dtype=BF16)


def _group_mean_matrix():
    i = np.arange(MXU_EDGE)
    m = (i[:, None] // HEAD_DIM_A == i[None, :] // HEAD_DIM_A).astype(np.float32) / HEAD_DIM_A
    return jnp.asarray(m, dtype=BF16)


def kernel(x_prompt, x_sample, cache_da_k, cache_da_v, cache_sb_k, cache_sb_v, state_conv, page_table,
           rel_bias, attn_norm_g, w_in, q_norm_g, k_norm_g, w_lambda, subln_g, w_proj_a, w_proj_b, w_out,
           ffn_norm_g, w_up, conv_w, conv_b, w_down):
    batch, seq, d = x_prompt.shape
    dec_b, dec_t, _ = x_sample.shape
    depth = w_in.shape[0]
    n_pool, page = cache_da_k.shape[1], cache_da_k.shape[2]
    n_pages = page_table.shape[1]
    past_len = n_pages * page
    ff = w_down.shape[1]

    pt = page_table.reshape(-1).astype(jnp.int32)
    cda_kt = jnp.swapaxes(cache_da_k.reshape(depth, n_pool, page, WIDTH_A), 2, 3)
    cda_vi = cache_da_v.reshape(depth, n_pool, page * N_HEADS_A, 2 * HEAD_DIM_A)
    csb_kt = jnp.swapaxes(cache_sb_k.reshape(depth, n_pool, page, WIDTH_B), 2, 3)
    csb_vt = jnp.swapaxes(cache_sb_v.reshape(depth, n_pool, page, WIDTH_B), 2, 3)

    rel_shift = ((rel_bias - rel_bias[N_BUCKETS - 1]) * LOG2E).astype(F32)
    pages = min(PAGES_PER_STEP, n_pages)
    bias_prompt, blast, bnew = _bias_tiles(rel_shift, min(DA_TILE, seq), dec_t, page)
    blast = jnp.pad(blast, ((0, 0), ((pages - 1) * page, 0)))

    u_hi_lo = _lower_triangles(min(SB_TILE, seq))
    gm = _group_mean_matrix()

    c_qa, c_ka, c_va = 0, WIDTH_A, 2 * WIDTH_A
    c_qb, c_kb, c_vb, c_g = 3 * WIDTH_A, 3 * WIDTH_A + WIDTH_B, 3 * WIDTH_A + 2 * WIDTH_B, 3 * WIDTH_A + 3 * WIDTH_B

    xp = x_prompt.reshape(batch * seq, d)
    xs = x_sample.reshape(dec_b * dec_t, d)
    kv_prompt, tails_p = None, []
    rows_s = [[] for _ in range(5)]
    for l in range(depth):
        lam_init = LAMBDA_BASE - LAMBDA_SCALE * math.exp(-LAMBDA_RATE * l)
        g_attn = attn_norm_g[l].reshape(1, d)
        wl_in = w_in[l].astype(BF16)
        w_std = jnp.concatenate([wl_in[:, c_qa:c_ka], wl_in[:, c_va:c_qb], wl_in[:, c_qb:c_kb],
                                 wl_in[:, c_vb:c_g], wl_in[:, c_g:]], axis=1)
        w_keys = jnp.concatenate([wl_in[:, c_ka:c_va], wl_in[:, c_kb:c_vb]], axis=1)
        w_t = jnp.concatenate([w_keys, wl_in[:, c_vb:c_g]], axis=1).T
        qg_t = jnp.tile(q_norm_g[l], WIDTH_A // HEAD_DIM_A).reshape(1, WIDTH_A)
        kg_t = jnp.tile(k_norm_g[l], WIDTH_A // HEAD_DIM_A).reshape(1, WIDTH_A)
        kg_col = k_norm_g[l].reshape(HEAD_DIM_A, 1)
        wl = w_lambda[l]
        sg = subln_g[l].reshape(1, 2 * HEAD_DIM_A)
        wpa, wpb, wo = w_proj_a[l].astype(BF16), w_proj_b[l].astype(BF16), w_out[l].astype(BF16)
        g_ffn = ffn_norm_g[l].reshape(1, d)
        wup, wdn = w_up[l].astype(BF16), w_down[l].astype(BF16)
        cw, cb = conv_w[l], conv_b[l].reshape(1, ff)

        (qa, kat, katb, vai, vab, qb, kbt, kbtb, vbt, vbb, gates) = _in_proj_prompt(
            xp, g_attn, w_std, w_t, qg_t, kg_col, gm, batch, seq, kv_prompt)
        kv_prompt = (kat, vai, kbt, vbt)
        oa = _da_prompt(qa, katb, vab, bias_prompt, wl, sg, batch, seq, lam_init)
        ob = _sb_prompt(qb, kbtb, vbb, u_hi_lo, batch, seq)
        xp, tail = _merge_ffn_prompt(xp, oa, ob, gates, wpa, wpb, wo, g_ffn, wup, cw, cb, wdn, batch, seq)
        tails_p.append(tail[:, SUBLANES - (CONV_WIDTH - 1):, :])

        qa, ka, va, qb, kb, vb, gates = _in_proj_sample(xs, g_attn, w_std, w_keys, qg_t, kg_t, gm)
        oa = _da_sample(pt, qa, ka, va, blast, bnew, wl, sg, cda_kt, cda_vi, l, dec_b, dec_t, lam_init)
        ob = _sb_sample(pt, qb, kb, vb, u_hi_lo, csb_kt, csb_vt, l, dec_b, dec_t)
        xs = _merge_out(xs, oa, ob, gates, wpa, wpb, wo)
        st = state_conv[l]
        zeros = jnp.zeros((dec_b, dec_t - 1, ff), F32)
        s1 = jnp.concatenate([st[:, 1:2], zeros], axis=1).reshape(dec_b * dec_t, ff)
        s2 = jnp.concatenate([st, zeros[:, 1:]], axis=1).reshape(dec_b * dec_t, ff)
        xs, a_s = _ffn_sample(xs, g_ffn, wup, cw, cb, wdn, s1, s2, dec_t)
        new_s = (ka.reshape(dec_b, dec_t, N_HEADS_A, 2, HEAD_DIM_A),
                 va.reshape(dec_b, dec_t, N_HEADS_A, 2 * HEAD_DIM_A),
                 kb.reshape(dec_b, dec_t, N_HEADS_B, HEAD_DIM_B),
                 vb.reshape(dec_b, dec_t, N_HEADS_B, HEAD_DIM_B),
                 a_s.reshape(dec_b, dec_t, ff)[:, dec_t - (CONV_WIDTH - 1):, :])
        for lst, r in zip(rows_s, new_s):
            lst.append(r)

    kat, vai, kbt, vbt = kv_prompt
    out_p = [jnp.transpose(kat.reshape(depth, batch, N_HEADS_A, 2, HEAD_DIM_A, seq), (0, 1, 5, 2, 3, 4)),
             vai.reshape(depth, batch, seq, N_HEADS_A, 2 * HEAD_DIM_A),
             jnp.transpose(kbt.reshape(depth, batch, N_HEADS_B, HEAD_DIM_B, seq), (0, 1, 4, 2, 3)),
             jnp.transpose(vbt.reshape(depth, batch, N_HEADS_B, HEAD_DIM_B, seq), (0, 1, 4, 2, 3)),
             jnp.stack(tails_p)]
    out_s = [jnp.stack(r) for r in rows_s]
    return (xp.reshape(batch, seq, d), xs.reshape(dec_b, dec_t, d), *out_p, *out_s)
```
